```python
import math
import jax, jax.numpy as jnp
from jax import lax
import numpy as np

D_MODEL = 1024
BATCH = 8
SEQ = 2048
DEPTH = 4
DEC_BATCH = 32
DEC_SEQ = 8
PAST_LEN = 16384
PAGE_SIZE = 128

SSM_GROUPS = 16
SSM_GROUP_CH = 16
SSM_WIDTH = SSM_GROUPS * SSM_GROUP_CH
SSM_STATE = 64
SSM_DT_MIN = 1e-3
SSM_DT_MAX = 1e-1
MLA_HEADS = 8
MLA_NOPE = 64
MLA_ROPE = 32
MLA_V = 64
MLA_Q_LORA = 384
MLA_KV_LORA = 256
MLA_WIDTH = MLA_HEADS * MLA_V
MLA_SCALE = (MLA_NOPE + MLA_ROPE) ** -0.5
ROPE_THETA = 10000.0
Q_BLOCK = 128
HG_HEADS = 4
HG_K = 64
HG_V = 64
HG_KW = HG_HEADS * HG_K
HG_VW = HG_HEADS * HG_V
HG_CHUNK = 64
FF_DIM = 2816
N_EXPERTS = 8
TOP_K = 2
N_DENSE = (DEPTH + 1) // 2
N_MOE = DEPTH // 2
N_BRANCH = 3
EPS = 1e-6
SEG_SIZES = (SSM_WIDTH, MLA_Q_LORA, MLA_KV_LORA, MLA_ROPE, HG_KW, HG_KW, HG_VW, HG_VW, N_BRANCH * D_MODEL)
N_IN = SSM_WIDTH + MLA_Q_LORA + MLA_KV_LORA + MLA_ROPE + 2 * HG_KW + 2 * HG_VW + N_BRANCH * D_MODEL

kernel_name = 'hybrid_s5_mla_hgrn2_adaln_step'


def _rmsnorm(x, g):
    xf = x.astype(jnp.float32)
    y = xf * lax.rsqrt(jnp.mean(xf * xf, axis=-1, keepdims=True) + EPS)
    return (y * g.astype(jnp.float32)).astype(x.dtype)


def _rope(x, pos):
    half = x.shape[-1] // 2
    freq = ROPE_THETA ** (-jnp.arange(half, dtype=jnp.float32) / half)
    ang = pos.astype(jnp.float32)[:, None] * freq[None, :]
    shp = (1, pos.shape[0]) + (1,) * (x.ndim - 3) + (half,)
    cos = jnp.cos(ang).reshape(shp).astype(x.dtype)
    sin = jnp.sin(ang).reshape(shp).astype(x.dtype)
    x1, x2 = x[..., :half], x[..., half:]
    return jnp.concatenate([x1 * cos - x2 * sin, x1 * sin + x2 * cos], axis=-1)


def _complex_affine_combine(e1, e2):
    a1r, a1i, b1r, b1i = e1
    a2r, a2i, b2r, b2i = e2
    return (a1r * a2r - a1i * a2i,
            a1r * a2i + a1i * a2r,
            a2r * b1r - a2i * b1i + b2r,
            a2r * b1i + a2i * b1r + b2i)


def _s5_branch(u, h0r, h0i, a_re, a_im, log_dt, b_re, b_im, c_re, c_im, d, glu_w, glu_b):
    f32 = jnp.float32
    bsz, L, _ = u.shape
    uf = u.astype(f32).reshape(bsz, L, SSM_GROUPS, SSM_GROUP_CH)
    ar, ai = a_re.astype(f32), a_im.astype(f32)
    dt = jnp.exp(log_dt.astype(f32))[:, None]
    mag = jnp.exp(dt * ar)
    abr, abi = mag * jnp.cos(dt * ai), mag * jnp.sin(dt * ai)
    den = ar * ar + ai * ai
    cr = ((abr - 1.0) * ar + abi * ai) / den
    ci = (abi * ar - (abr - 1.0) * ai) / den
    bur = jnp.einsum('blgc,gpc->blgp', uf, b_re.astype(f32))
    bui = jnp.einsum('blgc,gpc->blgp', uf, b_im.astype(f32))
    xr = cr * bur - ci * bui
    xi = cr * bui + ci * bur
    h0r, h0i = h0r.astype(f32), h0i.astype(f32)
    xr = xr.at[:, 0].add(abr * h0r - abi * h0i)
    xi = xi.at[:, 0].add(abr * h0i + abi * h0r)
    shp = xr.shape
    elems = (jnp.broadcast_to(abr, shp), jnp.broadcast_to(abi, shp), xr, xi)
    _, _, hr, hi = lax.associative_scan(_complex_affine_combine, elems, axis=1)
    y = (jnp.einsum('blgp,gcp->blgc', hr, c_re.astype(f32))
         - jnp.einsum('blgp,gcp->blgc', hi, c_im.astype(f32))
         + d.astype(f32).reshape(SSM_GROUPS, SSM_GROUP_CH) * uf)
    y = jax.nn.gelu(y.reshape(bsz, L, SSM_WIDTH))
    y = y * jax.nn.sigmoid(y @ glu_w.astype(f32) + glu_b.astype(f32))
    return y.astype(u.dtype), hr[:, -1], hi[:, -1]


def _mla_attend(q_lat, q_rope, lat, krope, q_pos):
    bsz, lq = q_lat.shape[:2]
    qb = Q_BLOCK if lq % Q_BLOCK == 0 else lq
    nb = lq // qb
    k_pos = jnp.arange(lat.shape[1], dtype=jnp.int32)

    def block(args):
        ql, qr, qp = args
        s = (jnp.einsum('bqhr,bkr->bhqk', ql, lat)
             + jnp.einsum('bqhe,bke->bhqk', qr, krope)).astype(jnp.float32) * MLA_SCALE
        s = jnp.where(k_pos[None, None, None, :] <= qp[None, None, :, None], s, -jnp.inf)
        p = jax.nn.softmax(s, axis=-1).astype(lat.dtype)
        return jnp.einsum('bhqk,bkr->bqhr', p, lat)

    def split(t):
        return jnp.swapaxes(t.reshape((bsz, nb, qb) + t.shape[2:]), 0, 1)

    ctx = lax.map(block, (split(q_lat), split(q_rope), q_pos.reshape(nb, qb)))
    return jnp.swapaxes(ctx, 0, 1).reshape(q_lat.shape)


def _mla_branch(cq, ckv, kr, pos, past_lat, past_kr, q_norm_g, w_uq, kv_norm_g, w_uk, w_uv):
    bsz, L, _ = cq.shape
    q = (_rmsnorm(cq, q_norm_g) @ w_uq).reshape(bsz, L, MLA_HEADS, MLA_NOPE + MLA_ROPE)
    q_nope = q[..., :MLA_NOPE]
    q_rope = _rope(q[..., MLA_NOPE:], pos)
    lat = _rmsnorm(ckv, kv_norm_g)
    kr = _rope(kr, pos)
    q_lat = jnp.einsum('blhd,rhd->blhr', q_nope, w_uk.reshape(MLA_KV_LORA, MLA_HEADS, MLA_NOPE))
    if past_lat is None:
        lat_all, kr_all = lat, kr
    else:
        lat_all = jnp.concatenate([past_lat, lat], axis=1)
        kr_all = jnp.concatenate([past_kr, kr], axis=1)
    ctx = _mla_attend(q_lat, q_rope, lat_all, kr_all, pos)
    y = jnp.einsum('blhr,rhd->blhd', ctx, w_uv.reshape(MLA_KV_LORA, MLA_HEADS, MLA_V))
    return y.reshape(bsz, L, MLA_WIDTH), lat, kr


def _hgrn2_branch(q, fp, i, g, S0, lb, norm_g):
    f32 = jnp.float32
    bsz, L, _ = q.shape
    qf = q.astype(f32).reshape(bsz, L, HG_HEADS, HG_K)
    fpf = fp.astype(f32).reshape(bsz, L, HG_HEADS, HG_K)
    v = i.astype(f32).reshape(bsz, L, HG_HEADS, HG_V)
    lb = lb.reshape(HG_HEADS, HG_K)
    logf = jnp.logaddexp(jnp.log(lb), jnp.log1p(-lb) + jax.nn.log_sigmoid(fpf))
    k = (1.0 - lb) * jax.nn.sigmoid(-fpf)
    C = min(HG_CHUNK, L)
    pad = (-L) % C
    n = (L + pad) // C

    def chunks(t):
        t = jnp.pad(t, ((0, 0), (0, pad), (0, 0), (0, 0)))
        return t.reshape(bsz, n, C, HG_HEADS, t.shape[-1]).transpose(1, 0, 3, 2, 4)

    causal = jnp.tril(jnp.ones((C, C), dtype=bool))

    def step(S, inp):
        qc, kc, lfc, vc = inp
        b = jnp.cumsum(lfc, axis=2)
        o = jnp.einsum('bhtk,bhkv->bhtv', qc * jnp.exp(b), S)
        diff = b[:, :, :, None, :] - b[:, :, None, :, :]
        w = jnp.exp(jnp.where(causal[:, :, None], diff, -jnp.inf))
        att = jnp.einsum('bhtk,bhsk,bhtsk->bhts', qc, kc, w)
        o = o + jnp.einsum('bhts,bhsv->bhtv', att, vc)
        bl = b[:, :, -1]
        S = jnp.exp(bl)[..., None] * S + jnp.einsum('bhsk,bhsv->bhkv', kc * jnp.exp(bl[:, :, None] - b), vc)
        return S, o

    S_T, o = lax.scan(step, S0.astype(f32), (chunks(qf), chunks(k), chunks(logf), chunks(v)))
    o = o.transpose(1, 0, 3, 2, 4).reshape(bsz, n * C, HG_HEADS, HG_V)[:, :L]
    o = _rmsnorm(o, norm_g).reshape(bsz, L, HG_VW) * jax.nn.silu(g.astype(f32))
    return o.astype(q.dtype), S_T


def _swiglu(h, wg, wu, wd):
    return (jax.nn.silu(h @ wg) * (h @ wu)) @ wd


def _moe_ffn(h, rw, rb, wg, wu, wd):
    logits = (h @ rw + rb).astype(jnp.float32)
    top_v, top_i = lax.top_k(logits, TOP_K)
    top_w = jax.nn.softmax(top_v, axis=-1)
    gate = jnp.sum(jax.nn.one_hot(top_i, N_EXPERTS, dtype=jnp.float32) * top_w[..., None], axis=-2).astype(h.dtype)
    out = jnp.zeros_like(h)
    for e in range(N_EXPERTS):
        out = out + gate[..., e:e + 1] * _swiglu(h, wg[e], wu[e], wd[e])
    return out


def _trunk(x, c, W, pos0, ssm_re0, ssm_im0, hgrn0, cache_lat, cache_kr, page_table):
    f32 = jnp.float32
    bsz, L, _ = x.shape
    pos = pos0 + jnp.arange(L, dtype=jnp.int32)
    lbs = jnp.cumsum(jax.nn.softmax(W['hgrn_lb_logits'].astype(f32), axis=0), axis=0)
    lbs = lbs - lbs[0:1]
    pts = tuple(int(v) for v in np.cumsum(SEG_SIZES)[:-1])
    lat_l, kr_l, sre_l, sim_l, hg_l = [], [], [], [], []
    for l in range(DEPTH):
        mod = (jax.nn.silu(c) @ W['ada_w'][l] + W['ada_b'][l])[:, None, :]
        sh1, sc1, g1, sh2, sc2, g2 = jnp.split(mod, 6, axis=-1)
        h = _rmsnorm(x, W['norm1_g'][l]) * (1 + sc1) + sh1
        z = h @ W['w_in'][l]
        u_a, cq, ckv, kr, hq, hf, hi, hg, gates = jnp.split(z, pts, axis=-1)
        if ssm_re0 is None:
            h0r = jnp.zeros((bsz, SSM_GROUPS, SSM_STATE), f32)
            h0i = jnp.zeros((bsz, SSM_GROUPS, SSM_STATE), f32)
            S0 = jnp.zeros((bsz, HG_HEADS, HG_K, HG_V), f32)
        else:
            h0r, h0i, S0 = ssm_re0[l], ssm_im0[l], hgrn0[l]
        if cache_lat is None:
            past_lat, past_kr = None, None
        else:
            past_lat = cache_lat[l][page_table].reshape(bsz, -1, MLA_KV_LORA)
            past_kr = cache_kr[l][page_table].reshape(bsz, -1, MLA_ROPE)
        y_a, hTr, hTi = _s5_branch(u_a, h0r, h0i, W['ssm_a_re'][l], W['ssm_a_im'][l], W['ssm_log_dt'][l],
                                   W['ssm_b_re'][l], W['ssm_b_im'][l], W['ssm_c_re'][l], W['ssm_c_im'][l],
                                   W['ssm_d'][l], W['ssm_glu_w'][l], W['ssm_glu_b'][l])
        y_b, lat, krot = _mla_branch(cq, ckv, kr, pos, past_lat, past_kr, W['mla_q_norm_g'][l], W['mla_w_uq'][l],
                                     W['mla_kv_norm_g'][l], W['mla_w_uk'][l], W['mla_w_uv'][l])
        y_c, S_T = _hgrn2_branch(hq, hf, hi, hg, S0, lbs[l], W['hgrn_norm_g'][l])
        ga, gb, gc = jnp.split(jax.nn.sigmoid(gates), N_BRANCH, axis=-1)
        mix = ga * (y_a @ W['proj_a'][l]) + gb * (y_b @ W['proj_b'][l]) + gc * (y_c @ W['proj_c'][l])
        x = x + g1 * (mix @ W['w_out'][l])
        h2 = _rmsnorm(x, W['norm2_g'][l]) * (1 + sc2) + sh2
        if l % 2 == 0:
            j = l // 2
            f = _swiglu(h2, W['ffn_w_gate'][j], W['ffn_w_up'][j], W['ffn_w_down'][j])
        else:
            j = l // 2
            f = _moe_ffn(h2, W['moe_router_w'][j], W['moe_router_b'][j], W['moe_w_gate'][j],
                         W['moe_w_up'][j], W['moe_w_down'][j])
        x = x + g2 * f
        lat_l.append(lat)
        kr_l.append(krot)
        sre_l.append(hTr.astype(x.dtype))
        sim_l.append(hTi.astype(x.dtype))
        hg_l.append(S_T.astype(x.dtype))
    y = _rmsnorm(x, W['final_norm_g'])
    return y, jnp.stack(lat_l), jnp.stack(kr_l), jnp.stack(sre_l), jnp.stack(sim_l), jnp.stack(hg_l)


def setup_inputs(seed: int = 0) -> dict:
    key = jax.random.key(seed)
    ks = iter(jax.random.split(key, 64))
    f32 = jnp.float32

    def nrm(shape, scale):
        return scale * jax.random.normal(next(ks), shape, f32)

    def gain(shape):
        return 1.0 + nrm(shape, 0.02)

    n_pages = PAST_LEN // PAGE_SIZE
    n_used = DEC_BATCH * n_pages
    n_pool = n_used + (n_used + 3) // 4
    D = D_MODEL
    inp = {}
    inp['x_prompt'] = nrm((BATCH, SEQ, D), 1.0)
    inp['x_sample'] = nrm((DEC_BATCH, DEC_SEQ, D), 1.0)
    inp['c_prompt'] = nrm((BATCH, D), 1.0)
    inp['c_sample'] = nrm((DEC_BATCH, D), 1.0)
    inp['cache_kv_latent'] = nrm((DEPTH, n_pool, PAGE_SIZE, MLA_KV_LORA), 1.0)
    inp['cache_k_rope'] = nrm((DEPTH, n_pool, PAGE_SIZE, MLA_ROPE), 1.0)
    inp['state_ssm_re'] = nrm((DEPTH, DEC_BATCH, SSM_GROUPS, SSM_STATE), 0.5)
    inp['state_ssm_im'] = nrm((DEPTH, DEC_BATCH, SSM_GROUPS, SSM_STATE), 0.5)
    inp['state_hgrn'] = nrm((DEPTH, DEC_BATCH, HG_HEADS, HG_K, HG_V), 0.3)
    inp['page_table'] = jax.random.permutation(next(ks), n_pool)[:n_used].reshape(DEC_BATCH, n_pages).astype(jnp.int32)
    inp['ada_w'] = nrm((DEPTH, D, 6 * D), 0.5 * D ** -0.5)
    inp['ada_b'] = nrm((DEPTH, 6 * D), 0.02)
    inp['norm1_g'] = gain((DEPTH, D))
    inp['norm2_g'] = gain((DEPTH, D))
    inp['w_in'] = nrm((DEPTH, D, N_IN), D ** -0.5)
    inp['ssm_a_re'] = -0.5 + nrm((DEPTH, SSM_GROUPS, SSM_STATE), 0.01)
    inp['ssm_a_im'] = (np.pi * jnp.broadcast_to(jnp.arange(SSM_STATE, dtype=f32), (DEPTH, SSM_GROUPS, SSM_STATE))
                       + nrm((DEPTH, SSM_GROUPS, SSM_STATE), 0.01))
    inp['ssm_log_dt'] = jax.random.uniform(next(ks), (DEPTH, SSM_GROUPS), f32,
                                           math.log(SSM_DT_MIN), math.log(SSM_DT_MAX))
    inp['ssm_b_re'] = nrm((DEPTH, SSM_GROUPS, SSM_STATE, SSM_GROUP_CH), (2 * SSM_GROUP_CH) ** -0.5)
    inp['ssm_b_im'] = nrm((DEPTH, SSM_GROUPS, SSM_STATE, SSM_GROUP_CH), (2 * SSM_GROUP_CH) ** -0.5)
    inp['ssm_c_re'] = nrm((DEPTH, SSM_GROUPS, SSM_GROUP_CH, SSM_STATE), (2 * SSM_STATE) ** -0.5)
    inp['ssm_c_im'] = nrm((DEPTH, SSM_GROUPS, SSM_GROUP_CH, SSM_STATE), (2 * SSM_STATE) ** -0.5)
    inp['ssm_d'] = nrm((DEPTH, SSM_WIDTH), 1.0)
    inp['ssm_glu_w'] = nrm((DEPTH, SSM_WIDTH, SSM_WIDTH), SSM_WIDTH ** -0.5)
    inp['ssm_glu_b'] = nrm((DEPTH, SSM_WIDTH), 0.02)
    inp['mla_q_norm_g'] = gain((DEPTH, MLA_Q_LORA))
    inp['mla_w_uq'] = nrm((DEPTH, MLA_Q_LORA, MLA_HEADS * (MLA_NOPE + MLA_ROPE)), MLA_Q_LORA ** -0.5)
    inp['mla_kv_norm_g'] = gain((DEPTH, MLA_KV_LORA))
    inp['mla_w_uk'] = nrm((DEPTH, MLA_KV_LORA, MLA_HEADS * MLA_NOPE), MLA_KV_LORA ** -0.5)
    inp['mla_w_uv'] = nrm((DEPTH, MLA_KV_LORA, MLA_HEADS * MLA_V), MLA_KV_LORA ** -0.5)
    inp['hgrn_lb_logits'] = nrm((DEPTH, HG_KW), 0.5)
    inp['hgrn_norm_g'] = gain((DEPTH, HG_V))
    inp['proj_a'] = nrm((DEPTH, SSM_WIDTH, D), SSM_WIDTH ** -0.5)
    inp['proj_b'] = nrm((DEPTH, MLA_WIDTH, D), MLA_WIDTH ** -0.5)
    inp['proj_c'] = nrm((DEPTH, HG_VW, D), HG_VW ** -0.5)
    inp['w_out'] = nrm((DEPTH, D, D), D ** -0.5)
    inp['ffn_w_gate'] = nrm((N_DENSE, D, FF_DIM), D ** -0.5)
    inp['ffn_w_up'] = nrm((N_DENSE, D, FF_DIM), D ** -0.5)
    inp['ffn_w_down'] = nrm((N_DENSE, FF_DIM, D), FF_DIM ** -0.5)
    inp['moe_router_w'] = nrm((N_MOE, D, N_EXPERTS), D ** -0.5)
    inp['moe_router_b'] = nrm((N_MOE, N_EXPERTS), 0.01)
    inp['moe_w_gate'] = nrm((N_MOE, N_EXPERTS, D, FF_DIM), D ** -0.5)
    inp['moe_w_up'] = nrm((N_MOE, N_EXPERTS, D, FF_DIM), D ** -0.5)
    inp['moe_w_down'] = nrm((N_MOE, N_EXPERTS, FF_DIM, D), FF_DIM ** -0.5)
    inp['final_norm_g'] = gain((D,))
    return inp


def reference(x_prompt, x_sample, c_prompt, c_sample, cache_kv_latent, cache_k_rope, state_ssm_re, state_ssm_im,
              state_hgrn, page_table, ada_w, ada_b, norm1_g, norm2_g, w_in, ssm_a_re, ssm_a_im, ssm_log_dt,
              ssm_b_re, ssm_b_im, ssm_c_re, ssm_c_im, ssm_d, ssm_glu_w, ssm_glu_b, mla_q_norm_g, mla_w_uq,
              mla_kv_norm_g, mla_w_uk, mla_w_uv, hgrn_lb_logits, hgrn_norm_g, proj_a, proj_b, proj_c, w_out,
              ffn_w_gate, ffn_w_up, ffn_w_down, moe_router_w, moe_router_b, moe_w_gate, moe_w_up, moe_w_down,
              final_norm_g):
    W = dict(ada_w=ada_w, ada_b=ada_b, norm1_g=norm1_g, norm2_g=norm2_g, w_in=w_in,
             ssm_a_re=ssm_a_re, ssm_a_im=ssm_a_im, ssm_log_dt=ssm_log_dt, ssm_b_re=ssm_b_re, ssm_b_im=ssm_b_im,
             ssm_c_re=ssm_c_re, ssm_c_im=ssm_c_im, ssm_d=ssm_d, ssm_glu_w=ssm_glu_w, ssm_glu_b=ssm_glu_b,
             mla_q_norm_g=mla_q_norm_g, mla_w_uq=mla_w_uq, mla_kv_norm_g=mla_kv_norm_g, mla_w_uk=mla_w_uk,
             mla_w_uv=mla_w_uv, hgrn_lb_logits=hgrn_lb_logits, hgrn_norm_g=hgrn_norm_g,
             proj_a=proj_a, proj_b=proj_b, proj_c=proj_c, w_out=w_out,
             ffn_w_gate=ffn_w_gate, ffn_w_up=ffn_w_up, ffn_w_down=ffn_w_down,
             moe_router_w=moe_router_w, moe_router_b=moe_router_b, moe_w_gate=moe_w_gate,
             moe_w_up=moe_w_up, moe_w_down=moe_w_down, final_norm_g=final_norm_g)
    y_p, lat_p, kr_p, sre_p, sim_p, hg_p = _trunk(x_prompt, c_prompt, W, 0, None, None, None, None, None, None)
    past_len = page_table.shape[1] * cache_kv_latent.shape[2]
    y_s, lat_s, kr_s, sre_s, sim_s, hg_s = _trunk(x_sample, c_sample, W, past_len, state_ssm_re, state_ssm_im,
                                                  state_hgrn, cache_kv_latent, cache_k_rope, page_table)
    return (y_p, y_s, lat_p, kr_p, sre_p, sim_p, hg_p, lat_s, kr_s, sre_s, sim_s, hg_s)
```

```python
import functools

import jax
import jax.numpy as jnp
from jax import lax
from jax.experimental import pallas as pl
from jax.experimental.pallas import tpu as pltpu

F32 = jnp.float32
BF16 = jnp.bfloat16

D_MODEL = 1024
SSM_GROUPS = 16
SSM_GROUP_CH = 16
SSM_WIDTH = 256
SSM_STATE = 64
SSM_FLAT = SSM_GROUPS * SSM_STATE
MLA_HEADS = 8
MLA_NOPE = 64
MLA_ROPE = 32
MLA_V = 64
MLA_Q_LORA = 384
MLA_KV_LORA = 256
MLA_SCALE = (MLA_NOPE + MLA_ROPE) ** -0.5
ROPE_THETA = 10000.0
HG_HEADS = 4
HG_K = 64
HG_W = 256
FF_DIM = 2816
N_EXPERTS = 8
EPS = 1e-6

LANES = 128
VMEM_LIMIT = 52 * 1024 * 1024

C_U, C_CKV, C_CQ, C_HZ, C_GATES, C_KR, C_END = 0, 256, 512, 1024, 2048, 5120, 5248


def _mm(a, b):
    return jnp.dot(a, b, preferred_element_type=F32)


def _mm_nt(a, b):
    return lax.dot_general(a, b, (((1,), (1,)), ((), ())), preferred_element_type=F32)


def _mm_tn(a, b):
    return lax.dot_general(a, b, (((0,), (0,)), ((), ())), preferred_element_type=F32)


def _mm_f32(a, b):
    return jnp.dot(a, b, preferred_element_type=F32, precision=lax.Precision.HIGHEST)


def _rms(x, g):
    return x * lax.rsqrt(jnp.mean(x * x, axis=-1, keepdims=True) + EPS) * g


def _params(sem):
    return pltpu.CompilerParams(dimension_semantics=sem, vmem_limit_bytes=VMEM_LIMIT)


def _const_spec(shape):
    nd = len(shape)
    return pl.BlockSpec(shape, lambda *_: (0,) * nd)


class _Group:
    def __init__(self, bz, length, tm, per_row):
        self.bz, self.length, self.tm, self.per_row = bz, length, tm, per_row
        self.m = bz * length
        if per_row:
            self.nb = self.m // tm
            self.grid = (1, self.nb)
        else:
            self.nb = length // tm
            self.grid = (bz, self.nb)

    def rows(self, width):
        nb = self.nb
        return pl.BlockSpec((self.tm, width), lambda b, i, *_: (b * nb + i, 0))

    def mod(self, col):
        if self.per_row:
            return pl.BlockSpec((self.tm, D_MODEL), lambda b, i, *_: (i, col))
        return pl.BlockSpec((None, 1, D_MODEL), lambda b, i, *_: (b, 0, col))

    def pos(self, width):
        return pl.BlockSpec((self.tm, width), lambda b, i, *_: (i, 0))

    def tmajor(self, width):
        if self.per_row:
            return self.rows(width)
        return pl.BlockSpec((self.tm, width), lambda b, i, *_: (i, b))

    def tmajor_shape(self, width):
        return (self.m, width) if self.per_row else (self.length, self.bz * width)


def _param_kernel(ar_ref, ai_ref, ldt_ref, lbl_ref, abr_ref, abi_ref, cr_ref, ci_ref, lbs_ref):
    ar, ai = ar_ref[...], ai_ref[...]
    dt = jnp.exp(ldt_ref[...])
    mag = jnp.exp(dt * ar)
    abr, abi = mag * jnp.cos(dt * ai), mag * jnp.sin(dt * ai)
    den = ar * ar + ai * ai
    abr_ref[...] = abr
    abi_ref[...] = abi
    cr_ref[...] = ((abr - 1.0) * ar + abi * ai) / den
    ci_ref[...] = (abi * ar - (abr - 1.0) * ai) / den
    x = lbl_ref[...]
    e = jnp.exp(x - jnp.max(x, axis=0, keepdims=True))
    p = e / jnp.sum(e, axis=0, keepdims=True)
    rows, acc = [], p[0:1]
    for l in range(x.shape[0]):
        if l:
            acc = acc + p[l:l + 1]
        rows.append(acc - p[0:1])
    lbs_ref[...] = jnp.concatenate(rows, axis=0)


def _prep_params(a_re, a_im, log_dt, lb_logits):
    depth = a_re.shape[0]
    ar = a_re.reshape(depth, SSM_FLAT)
    ai = a_im.reshape(depth, SSM_FLAT)
    ldt = jnp.broadcast_to(log_dt[:, :, None], (depth, SSM_GROUPS, SSM_STATE)).reshape(depth, SSM_FLAT)
    flat = jax.ShapeDtypeStruct((depth, SSM_FLAT), F32)
    return pl.pallas_call(
        _param_kernel,
        out_shape=(flat, flat, flat, flat, jax.ShapeDtypeStruct((depth, HG_W), F32)),
        name="param_prep",
    )(ar, ai, ldt, lb_logits)


def _ada_kernel(c_ref, w_ref, b_ref, o_ref):
    c = c_ref[...]
    s = (c * jax.nn.sigmoid(c)).astype(BF16)
    o_ref[...] = _mm(s, w_ref[...].astype(BF16)) + b_ref[...]


def _ada_mod(c_all, ada_w, ada_b):
    depth, d, n = ada_w.shape
    rows = c_all.shape[0]
    tn = 1536
    return pl.pallas_call(
        _ada_kernel,
        grid=(depth, n // tn),
        in_specs=[pl.BlockSpec((rows, d), lambda l, j: (0, 0)),
                  pl.BlockSpec((None, d, tn), lambda l, j: (l, 0, j)),
                  pl.BlockSpec((None, 1, tn), lambda l, j: (l, 0, j))],
        out_specs=pl.BlockSpec((None, rows, tn), lambda l, j: (l, 0, j)),
        out_shape=jax.ShapeDtypeStruct((depth, rows, n), F32),
        compiler_params=_params(("arbitrary", "arbitrary")),
        name="ada_mod",
    )(c_all, ada_w, ada_b.reshape(depth, 1, n))


def _in_kernel(x_ref, sh_ref, sc_ref, g_ref, w_ref, qg_ref, wuq_ref, wuk_ref, kvg_ref,
               cq_ref, sq_ref, ck_ref, sk_ref,
               u_ref, hz_ref, gates_ref, ql_ref, qr_ref, lat_ref, latb_ref, kr_ref, krb_ref):
    x = x_ref[...]
    h = (_rms(x, g_ref[...]) * (1.0 + sc_ref[...]) + sh_ref[...]).astype(BF16)

    def seg(a, b):
        return _mm(h, w_ref[:, a:b])

    u_ref[...] = seg(C_U, C_CKV)
    hz_ref[...] = seg(C_HZ, C_GATES)
    gates_ref[...] = seg(C_GATES, C_KR)
    cqn = _rms(seg(C_CQ, C_CQ + MLA_Q_LORA), qg_ref[...]).astype(BF16)
    q = _mm(cqn, wuq_ref[...])
    n_nope = MLA_HEADS * MLA_NOPE
    n_rope = MLA_HEADS * MLA_ROPE
    ql_ref[...] = _mm(q[:, :n_nope].astype(BF16), wuk_ref[...]).astype(BF16)
    qr = q[:, n_nope:n_nope + n_rope] * cq_ref[...] + q[:, n_nope + n_rope:] * sq_ref[...]
    qr_ref[...] = qr.astype(BF16)
    lat = _rms(seg(C_CKV, C_CQ), kvg_ref[...])
    lat_ref[...] = lat
    latb_ref[...] = lat.astype(BF16)
    krz = seg(C_KR, C_END)
    kr = krz[:, :MLA_ROPE] * ck_ref[...] + krz[:, MLA_ROPE:2 * MLA_ROPE] * sk_ref[...]
    kr_ref[...] = kr
    krb_ref[...] = kr.astype(BF16)


def _in_proj(grp, x, mod, g1, w_in, qg, wuq, wuk, kvg, cosq, sinq, cosk, sink):
    m, tm = grp.m, grp.tm
    n_rope = MLA_HEADS * MLA_ROPE
    outs = [
        (grp.tmajor_shape(SSM_WIDTH), F32, grp.tmajor(SSM_WIDTH)),
        ((m, 4 * HG_W), F32, grp.rows(4 * HG_W)),
        ((m, 3 * D_MODEL), F32, grp.rows(3 * D_MODEL)),
        ((m, MLA_HEADS * MLA_KV_LORA), BF16, grp.rows(MLA_HEADS * MLA_KV_LORA)),
        ((m, n_rope), BF16, grp.rows(n_rope)),
        ((m, MLA_KV_LORA), F32, grp.rows(MLA_KV_LORA)),
        ((m, MLA_KV_LORA), BF16, grp.rows(MLA_KV_LORA)),
        ((m, MLA_ROPE), F32, grp.rows(MLA_ROPE)),
        ((m, MLA_ROPE), BF16, grp.rows(MLA_ROPE)),
    ]
    return pl.pallas_call(
        _in_kernel,
        grid=grp.grid,
        in_specs=[grp.rows(D_MODEL), grp.mod(0), grp.mod(1), _const_spec((1, D_MODEL)),
                  _const_spec(w_in.shape), _const_spec((1, MLA_Q_LORA)), _const_spec(wuq.shape),
                  _const_spec(wuk.shape), _const_spec((1, MLA_KV_LORA)),
                  grp.pos(n_rope), grp.pos(n_rope), grp.pos(MLA_ROPE), grp.pos(MLA_ROPE)],
        out_specs=[o[2] for o in outs],
        out_shape=[jax.ShapeDtypeStruct(o[0], o[1]) for o in outs],
        compiler_params=_params(("arbitrary", "arbitrary")),
        name="in_proj",
    )(x, mod, mod, g1, w_in, qg, wuq, wuk, kvg, cosq, sinq, cosk, sink)


def _s5_kernel(u_ref, h0r_ref, h0i_ref, abr_ref, abi_ref, cr_ref, ci_ref, bre_ref, bim_ref,
               cre_ref, cim_ref, d_ref, gw_ref, gb_ref,
               y_ref, htr_ref, hti_ref, hr_s, hi_s, xr_s, xi_s, *, tl, bz):
    i = pl.program_id(0)

    @pl.when(i == 0)
    def _():
        hr_s[...] = h0r_ref[...]
        hi_s[...] = h0i_ref[...]

    u = u_ref[...].reshape(tl * bz, SSM_WIDTH)
    ub = u.astype(BF16)
    bur, bui = _mm(ub, bre_ref[...]), _mm(ub, bim_ref[...])
    cr, ci = cr_ref[...], ci_ref[...]
    xr_s[...] = cr * bur - ci * bui
    xi_s[...] = cr * bui + ci * bur
    abr, abi = abr_ref[...], abi_ref[...]

    def step(t, carry):
        hr, hi = carry
        r0 = pl.multiple_of(t * bz, bz)
        nr = abr * hr - abi * hi + xr_s[pl.ds(r0, bz), :]
        ni = abr * hi + abi * hr + xi_s[pl.ds(r0, bz), :]
        xr_s[pl.ds(r0, bz), :] = nr
        xi_s[pl.ds(r0, bz), :] = ni
        return nr, ni

    hr, hi = lax.fori_loop(0, tl, step, (hr_s[...], hi_s[...]))
    hr_s[...] = hr
    hi_s[...] = hi
    htr_ref[...] = hr
    hti_ref[...] = hi
    y = (_mm(xr_s[...].astype(BF16), cre_ref[...]) - _mm(xi_s[...].astype(BF16), cim_ref[...])
         + d_ref[...] * u)
    y = jax.nn.gelu(y)
    y = y * jax.nn.sigmoid(_mm(y.astype(BF16), gw_ref[...]) + gb_ref[...])
    y_ref[...] = y.reshape(tl, bz, SSM_WIDTH)


def _s5(u_tm, h0r, h0i, abr, abi, cr, ci, bre, bim, cre, cim, d, gw, gb, tl):
    length, bz, _ = u_tm.shape
    row = _const_spec((1, SSM_FLAT))
    st = _const_spec((bz, SSM_FLAT))
    blk = pl.BlockSpec((tl, bz, SSM_WIDTH), lambda i: (i, 0, 0))
    return pl.pallas_call(
        functools.partial(_s5_kernel, tl=tl, bz=bz),
        grid=(length // tl,),
        in_specs=[blk, st, st, row, row, row, row, _const_spec(bre.shape), _const_spec(bim.shape),
                  _const_spec(cre.shape), _const_spec(cim.shape), _const_spec((1, SSM_WIDTH)),
                  _const_spec(gw.shape), _const_spec((1, SSM_WIDTH))],
        out_specs=[blk, st, st],
        out_shape=[jax.ShapeDtypeStruct(u_tm.shape, F32), jax.ShapeDtypeStruct((bz, SSM_FLAT), F32),
                   jax.ShapeDtypeStruct((bz, SSM_FLAT), F32)],
        scratch_shapes=[pltpu.VMEM((bz, SSM_FLAT), F32), pltpu.VMEM((bz, SSM_FLAT), F32),
                        pltpu.VMEM((tl * bz, SSM_FLAT), F32), pltpu.VMEM((tl * bz, SSM_FLAT), F32)],
        compiler_params=_params(("arbitrary",)),
        name="s5_scan",
    )(u_tm, h0r, h0i, abr, abi, cr, ci, bre, bim, cre, cim, d, gw, gb)


def _attn_prompt_kernel(ql_ref, qr_ref, lat_ref, kr_ref, wuv_ref, y_ref, m_s, l_s, acc_s, *, tq):
    i = pl.program_id(1)
    row_pos = i * tq + lax.broadcasted_iota(jnp.int32, (tq, tq), 0)
    col_iota = lax.broadcasted_iota(jnp.int32, (tq, tq), 1)
    ys = []
    for h in range(MLA_HEADS):
        q_l = ql_ref[:, h * MLA_KV_LORA:(h + 1) * MLA_KV_LORA]
        q_r = qr_ref[:, h * MLA_ROPE:(h + 1) * MLA_ROPE]
        m_s[...] = jnp.full(m_s.shape, -jnp.inf, F32)
        l_s[...] = jnp.zeros(l_s.shape, F32)
        acc_s[...] = jnp.zeros(acc_s.shape, F32)

        def kv_step(j, carry, q_l=q_l, q_r=q_r):
            k0 = pl.multiple_of(j * tq, tq)
            kl = lat_ref[pl.ds(k0, tq), :]
            kr = kr_ref[pl.ds(k0, tq), :]
            s = (_mm_nt(q_l, kl) + _mm_nt(q_r, kr)) * MLA_SCALE
            s = jnp.where(k0 + col_iota <= row_pos, s, -jnp.inf)
            m_old = m_s[...]
            m_new = jnp.maximum(m_old, jnp.max(s, axis=-1, keepdims=True))
            alpha = jnp.exp(m_old - m_new)
            p = jnp.exp(s - m_new)
            l_s[...] = alpha * l_s[...] + jnp.sum(p, axis=-1, keepdims=True)
            acc_s[...] = alpha * acc_s[...] + _mm(p.astype(BF16), kl)
            m_s[...] = m_new
            return carry

        lax.fori_loop(0, i + 1, kv_step, 0)
        ctx = acc_s[...] / l_s[...]
        ys.append(_mm(ctx.astype(BF16), wuv_ref[h]))
    y_ref[...] = jnp.concatenate(ys, axis=-1).astype(BF16)


def _attn_prompt(grp_bz, length, ql, qr, latb, krb, wuv, tq):
    nb = length // tq
    rows = lambda w: pl.BlockSpec((tq, w), lambda b, i: (b * nb + i, 0))
    return pl.pallas_call(
        functools.partial(_attn_prompt_kernel, tq=tq),
        grid=(grp_bz, nb),
        in_specs=[rows(MLA_HEADS * MLA_KV_LORA), rows(MLA_HEADS * MLA_ROPE),
                  pl.BlockSpec((None, length, MLA_KV_LORA), lambda b, i: (b, 0, 0)),
                  pl.BlockSpec((None, length, MLA_ROPE), lambda b, i: (b, 0, 0)),
                  _const_spec(wuv.shape)],
        out_specs=rows(MLA_HEADS * MLA_V),
        out_shape=jax.ShapeDtypeStruct((grp_bz * length, MLA_HEADS * MLA_V), BF16),
        scratch_shapes=[pltpu.VMEM((tq, 1), F32), pltpu.VMEM((tq, 1), F32),
                        pltpu.VMEM((tq, MLA_KV_LORA), F32)],
        compiler_params=_params(("arbitrary", "arbitrary")),
        name="attn_prompt",
    )(ql, qr, latb.reshape(grp_bz, length, MLA_KV_LORA), krb.reshape(grp_bz, length, MLA_ROPE), wuv)


def _attn_sample_kernel(pt_ref, ql_ref, qr_ref, nl_ref, nk_ref, wuv_ref, *rest, pg, lq):
    lat_refs, kr_refs = rest[:pg], rest[pg:2 * pg]
    y_ref, m_s, l_s, acc_s = rest[2 * pg:]
    s_idx = pl.program_id(1)
    n_steps = pl.num_programs(1)

    @pl.when(s_idx == 0)
    def _():
        m_s[...] = jnp.full(m_s.shape, -jnp.inf, F32)
        l_s[...] = jnp.zeros(l_s.shape, F32)
        acc_s[...] = jnp.zeros(acc_s.shape, F32)

    q_l, q_r = ql_ref[...], qr_ref[...]

    def update(scores, keys):
        m_old = m_s[...]
        m_new = m_old
        for s in scores:
            m_new = jnp.maximum(m_new, jnp.max(s, axis=-1, keepdims=True))
        alpha = jnp.exp(m_old - m_new)
        l_new = alpha * l_s[...]
        acc = alpha * acc_s[...]
        for s, k in zip(scores, keys):
            p = jnp.exp(s - m_new)
            l_new = l_new + jnp.sum(p, axis=-1, keepdims=True)
            acc = acc + _mm(p.astype(BF16), k)
        m_s[...] = m_new
        l_s[...] = l_new
        acc_s[...] = acc

    keys = [r[...].astype(BF16) for r in lat_refs]
    scores = [(_mm_nt(q_l, k) + _mm_nt(q_r, kr[...].astype(BF16))) * MLA_SCALE
              for k, kr in zip(keys, kr_refs)]
    update(scores, keys)

    @pl.when(s_idx == n_steps - 1)
    def _():
        rows = MLA_HEADS * lq
        nl, nk = nl_ref[...], nk_ref[...]
        s = (_mm_nt(q_l, nl) + _mm_nt(q_r, nk)) * MLA_SCALE
        q_t = lax.broadcasted_iota(jnp.int32, (rows, lq), 0) % lq
        k_t = lax.broadcasted_iota(jnp.int32, (rows, lq), 1)
        update([jnp.where(k_t <= q_t, s, -jnp.inf)], [nl])
        ctx = acc_s[...] / l_s[...]
        ys = [_mm(ctx[h * lq:(h + 1) * lq, :].astype(BF16), wuv_ref[h]) for h in range(MLA_HEADS)]
        y_ref[...] = jnp.concatenate(ys, axis=-1).astype(BF16)


def _attn_sample(layer, page_table, ql, qr, latb, krb, cache_lat, cache_kr, wuv, bz, lq):
    n_pages = page_table.shape[1]
    pg = min(8, n_pages)
    page = cache_lat.shape[2]
    rows = MLA_HEADS * lq
    qlh = ql.reshape(bz, lq, MLA_HEADS, MLA_KV_LORA).transpose(0, 2, 1, 3).reshape(bz, rows, MLA_KV_LORA)
    qrh = qr.reshape(bz, lq, MLA_HEADS, MLA_ROPE).transpose(0, 2, 1, 3).reshape(bz, rows, MLA_ROPE)

    def seq(n, w):
        return pl.BlockSpec((None, n, w), lambda b, s, pt: (b, 0, 0))

    def page_spec(i, w):
        return pl.BlockSpec((None, None, page, w), lambda b, s, pt: (layer, pt[b, s * pg + i], 0, 0))

    in_specs = ([seq(rows, MLA_KV_LORA), seq(rows, MLA_ROPE), seq(lq, MLA_KV_LORA), seq(lq, MLA_ROPE),
                 pl.BlockSpec(wuv.shape, lambda b, s, pt: (0, 0, 0))]
                + [page_spec(i, MLA_KV_LORA) for i in range(pg)]
                + [page_spec(i, MLA_ROPE) for i in range(pg)])
    y = pl.pallas_call(
        functools.partial(_attn_sample_kernel, pg=pg, lq=lq),
        grid_spec=pltpu.PrefetchScalarGridSpec(
            num_scalar_prefetch=1,
            grid=(bz, n_pages // pg),
            in_specs=in_specs,
            out_specs=pl.BlockSpec((None, lq, MLA_HEADS * MLA_V), lambda b, s, pt: (b, 0, 0)),
            scratch_shapes=[pltpu.VMEM((rows, 1), F32), pltpu.VMEM((rows, 1), F32),
                            pltpu.VMEM((rows, MLA_KV_LORA), F32)]),
        out_shape=jax.ShapeDtypeStruct((bz, lq, MLA_HEADS * MLA_V), BF16),
        compiler_params=_params(("arbitrary", "arbitrary")),
        name="attn_sample",
    )(page_table, qlh, qrh, latb.reshape(bz, lq, MLA_KV_LORA), krb.reshape(bz, lq, MLA_ROPE), wuv,
      *([cache_lat] * pg), *([cache_kr] * pg))
    return y.reshape(bz * lq, MLA_HEADS * MLA_V)


def _hgrn_kernel(hz_ref, st0_ref, lb_ref, ng_ref, y_ref, stt_ref, st_s, w_s, *, ch, tb):
    j = pl.program_id(1)

    @pl.when(j == 0)
    def _():
        st_s[...] = st0_ref[...]

    lb = lb_ref[...]
    log_lb, log1m_lb, one_m_lb = jnp.log(lb), jnp.log1p(-lb), 1.0 - lb
    tri = (lax.broadcasted_iota(jnp.int32, (ch, ch), 0)
           >= lax.broadcasted_iota(jnp.int32, (ch, ch), 1)).astype(F32)
    same_head = (lax.broadcasted_iota(jnp.int32, (HG_W, HG_W), 0) // HG_K
                 == lax.broadcasted_iota(jnp.int32, (HG_W, HG_W), 1) // HG_K)
    head_ones = same_head.astype(F32)
    head_ones_b = same_head.astype(BF16)
    t_idx = lax.broadcasted_iota(jnp.int32, (ch, HG_W), 0)
    ng = ng_ref[...]

    def chunk(c, carry):
        r0 = pl.multiple_of(c * ch, ch)
        q = hz_ref[pl.ds(r0, ch), 0:HG_W]
        fp = hz_ref[pl.ds(r0, ch), HG_W:2 * HG_W]
        v = hz_ref[pl.ds(r0, ch), 2 * HG_W:3 * HG_W]
        g = hz_ref[pl.ds(r0, ch), 3 * HG_W:4 * HG_W]
        logf = jnp.logaddexp(log_lb, log1m_lb + jax.nn.log_sigmoid(fp))
        k = one_m_lb * jax.nn.sigmoid(-fp)
        b = _mm_f32(tri, logf)
        st = st_s[...]
        o = _mm_nt((q * jnp.exp(b)).astype(BF16), st.astype(BF16))
        for s in range(ch):
            e = jnp.exp(jnp.where(t_idx >= s, b - b[s:s + 1, :], -jnp.inf))
            w_s[s * ch:(s + 1) * ch, :] = q * k[s:s + 1, :] * e
        att = _mm(w_s[...].astype(BF16), head_ones_b)
        for s in range(ch):
            o = o + att[s * ch:(s + 1) * ch, :] * v[s:s + 1, :]
        bl = b[ch - 1:ch, :]
        kd = k * jnp.exp(bl - b)
        upd = _mm_tn(v.astype(BF16), kd.astype(BF16))
        st_s[...] = st * jnp.exp(bl) + jnp.where(same_head, upd, 0.0)
        ms = _mm_f32(o * o, head_ones) * (1.0 / HG_K)
        on = o * lax.rsqrt(ms + EPS) * ng
        y_ref[pl.ds(r0, ch), :] = on * (g * jax.nn.sigmoid(g))
        return carry

    lax.fori_loop(0, tb // ch, chunk, 0)

    @pl.when(j == pl.num_programs(1) - 1)
    def _():
        stt_ref[...] = st_s[...]


def _hgrn(bz, length, hz, st0, lb, ng, ch, tb):
    nb = length // tb
    st_spec = pl.BlockSpec((None, HG_W, HG_W), lambda b, j: (b, 0, 0))
    return pl.pallas_call(
        functools.partial(_hgrn_kernel, ch=ch, tb=tb),
        grid=(bz, nb),
        in_specs=[pl.BlockSpec((tb, 4 * HG_W), lambda b, j: (b * nb + j, 0)), st_spec,
                  _const_spec((1, HG_W)), _const_spec((1, HG_W))],
        out_specs=[pl.BlockSpec((tb, HG_W), lambda b, j: (b * nb + j, 0)), st_spec],
        out_shape=[jax.ShapeDtypeStruct((bz * length, HG_W), F32),
                   jax.ShapeDtypeStruct((bz, HG_W, HG_W), F32)],
        scratch_shapes=[pltpu.VMEM((HG_W, HG_W), F32), pltpu.VMEM((ch * ch, HG_W), F32)],
        compiler_params=_params(("arbitrary", "arbitrary")),
        name="hgrn2",
    )(hz, st0, lb, ng)


def _mix_kernel(*refs, moe):
    if moe:
        (x_ref, ya_ref, yb_ref, yc_ref, gates_ref, g1_ref, sh2_ref, sc2_ref, n2_ref,
         pa_ref, pb_ref, pc_ref, wo_ref, rw_ref, rb_ref, xo_ref, h2_ref, gate_ref) = refs
    else:
        (x_ref, ya_ref, yb_ref, yc_ref, gates_ref, g1_ref, sh2_ref, sc2_ref, n2_ref,
         pa_ref, pb_ref, pc_ref, wo_ref, xo_ref, h2_ref) = refs
    d = D_MODEL
    mix = (jax.nn.sigmoid(gates_ref[:, 0:d]) * _mm(ya_ref[...].astype(BF16), pa_ref[...])
           + jax.nn.sigmoid(gates_ref[:, d:2 * d]) * _mm(yb_ref[...], pb_ref[...])
           + jax.nn.sigmoid(gates_ref[:, 2 * d:3 * d]) * _mm(yc_ref[...].astype(BF16), pc_ref[...]))
    xo = x_ref[...] + g1_ref[...] * _mm(mix.astype(BF16), wo_ref[...])
    xo_ref[...] = xo
    h2 = _rms(xo, n2_ref[...]) * (1.0 + sc2_ref[...]) + sh2_ref[...]
    h2_ref[...] = h2.astype(BF16)
    if moe:
        logits = _mm_f32(h2, rw_ref[...]) + rb_ref[...]
        lane = lax.broadcasted_iota(jnp.int32, logits.shape, 1)
        lg = jnp.where(lane < N_EXPERTS, logits, -jnp.inf)
        m1 = jnp.max(lg, axis=-1, keepdims=True)
        i1 = jnp.min(jnp.where(lg == m1, lane, LANES), axis=-1, keepdims=True)
        lg2 = jnp.where(lane == i1, -jnp.inf, lg)
        m2 = jnp.max(lg2, axis=-1, keepdims=True)
        i2 = jnp.min(jnp.where(lg2 == m2, lane, LANES), axis=-1, keepdims=True)
        e2 = jnp.exp(m2 - m1)
        den = 1.0 + e2
        gate_ref[...] = jnp.where(lane == i1, 1.0 / den, 0.0) + jnp.where(lane == i2, e2 / den, 0.0)


def _mix(grp, x, ya, yb, yc, gates, mod, n2, pa, pb, pc, wo, router=None):
    m = grp.m
    moe = router is not None
    in_specs = [grp.rows(D_MODEL), grp.tmajor(SSM_WIDTH), grp.rows(MLA_HEADS * MLA_V), grp.rows(HG_W),
                grp.rows(3 * D_MODEL), grp.mod(2), grp.mod(3), grp.mod(4), _const_spec((1, D_MODEL)),
                _const_spec(pa.shape), _const_spec(pb.shape), _const_spec(pc.shape), _const_spec(wo.shape)]
    args = [x, ya, yb, yc, gates, mod, mod, mod, n2, pa, pb, pc, wo]
    out_specs = [grp.rows(D_MODEL), grp.rows(D_MODEL)]
    out_shape = [jax.ShapeDtypeStruct((m, D_MODEL), F32), jax.ShapeDtypeStruct((m, D_MODEL), BF16)]
    if moe:
        in_specs += [_const_spec(router[0].shape), _const_spec(router[1].shape)]
        args += list(router)
        out_specs.append(grp.rows(LANES))
        out_shape.append(jax.ShapeDtypeStruct((m, LANES), F32))
    return pl.pallas_call(
        functools.partial(_mix_kernel, moe=moe),
        grid=grp.grid, in_specs=in_specs, out_specs=out_specs, out_shape=out_shape,
        compiler_params=_params(("arbitrary", "arbitrary")),
        name="mix_moe" if moe else "mix",
    )(*args)


def _swiglu_tile(h2_ref, wg_ref, wu_ref, wd_ref):
    h = h2_ref[...]
    g = _mm(h, wg_ref[...].astype(BF16))
    u = _mm(h, wu_ref[...].astype(BF16))
    a = (g * jax.nn.sigmoid(g) * u).astype(BF16)
    return _mm(a, wd_ref[...].astype(BF16))


def _ffn_kernel(h2_ref, x_ref, g2_ref, wg_ref, wu_ref, wd_ref, o_ref, acc_s):
    f = pl.program_id(2)

    @pl.when(f == 0)
    def _():
        acc_s[...] = jnp.zeros(acc_s.shape, F32)

    acc_s[...] += _swiglu_tile(h2_ref, wg_ref, wu_ref, wd_ref)

    @pl.when(f == pl.num_programs(2) - 1)
    def _():
        o_ref[...] = x_ref[...] + g2_ref[...] * acc_s[...]


def _ffn(grp, h2, x, mod, wg, wu, wd, tf):
    nf = FF_DIM // tf
    return pl.pallas_call(
        _ffn_kernel,
        grid=grp.grid + (nf,),
        in_specs=[grp.rows(D_MODEL), grp.rows(D_MODEL), grp.mod(5),
                  pl.BlockSpec((D_MODEL, tf), lambda b, i, f: (0, f)),
                  pl.BlockSpec((D_MODEL, tf), lambda b, i, f: (0, f)),
                  pl.BlockSpec((tf, D_MODEL), lambda b, i, f: (f, 0))],
        out_specs=grp.rows(D_MODEL),
        out_shape=jax.ShapeDtypeStruct((grp.m, D_MODEL), F32),
        scratch_shapes=[pltpu.VMEM((grp.tm, D_MODEL), F32)],
        compiler_params=_params(("arbitrary", "arbitrary", "arbitrary")),
        name="ffn_dense",
    )(h2, x, mod, wg, wu, wd)


def _moe_kernel(h2_ref, x_ref, g2_ref, gate_ref, wg_ref, wu_ref, wd_ref, o_ref, acc_s):
    e, f = pl.program_id(2), pl.program_id(3)

    @pl.when((e == 0) & (f == 0))
    def _():
        acc_s[...] = jnp.zeros(acc_s.shape, F32)

    gate = gate_ref[...]
    lane = lax.broadcasted_iota(jnp.int32, gate.shape, 1)
    gcol = jnp.sum(jnp.where(lane == e, gate, 0.0), axis=-1, keepdims=True)
    acc_s[...] += gcol * _swiglu_tile(h2_ref, wg_ref, wu_ref, wd_ref)

    @pl.when((e == pl.num_programs(2) - 1) & (f == pl.num_programs(3) - 1))
    def _():
        o_ref[...] = x_ref[...] + g2_ref[...] * acc_s[...]


def _moe(grp, h2, x, mod, gate, wg, wu, wd, tf):
    nf = FF_DIM // tf
    return pl.pallas_call(
        _moe_kernel,
        grid=grp.grid + (N_EXPERTS, nf),
        in_specs=[grp.rows(D_MODEL), grp.rows(D_MODEL), grp.mod(5), grp.rows(LANES),
                  pl.BlockSpec((None, D_MODEL, tf), lambda b, i, e, f: (e, 0, f)),
                  pl.BlockSpec((None, D_MODEL, tf), lambda b, i, e, f: (e, 0, f)),
                  pl.BlockSpec((None, tf, D_MODEL), lambda b, i, e, f: (e, f, 0))],
        out_specs=grp.rows(D_MODEL),
        out_shape=jax.ShapeDtypeStruct((grp.m, D_MODEL), F32),
        scratch_shapes=[pltpu.VMEM((grp.tm, D_MODEL), F32)],
        compiler_params=_params(("arbitrary",) * 4),
        name="ffn_moe",
    )(h2, x, mod, gate, wg, wu, wd)


def _final_kernel(x_ref, g_ref, o_ref):
    o_ref[...] = _rms(x_ref[...], g_ref[...])


def _final_norm(grp, x, g):
    return pl.pallas_call(
        _final_kernel, grid=grp.grid,
        in_specs=[grp.rows(D_MODEL), _const_spec((1, D_MODEL))],
        out_specs=grp.rows(D_MODEL),
        out_shape=jax.ShapeDtypeStruct((grp.m, D_MODEL), F32),
        compiler_params=_params(("arbitrary", "arbitrary")),
        name="final_norm",
    )(x, g)


def _relayout_weights(w):
    bf = lambda a: a.astype(BF16)
    depth = w["w_in"].shape[0]
    wi = w["w_in"]
    o = [0, 256, 640, 896, 928, 1184, 1440, 1696, 1952, 5024]
    kr = wi[:, :, o[3]:o[4]]
    half = MLA_ROPE // 2
    kr_rot = jnp.concatenate([-kr[..., half:], kr[..., :half]], axis=-1)
    zeros = lambda n: jnp.zeros(wi.shape[:2] + (n,), wi.dtype)
    w_in = bf(jnp.concatenate([
        wi[:, :, o[0]:o[1]], wi[:, :, o[2]:o[3]], wi[:, :, o[1]:o[2]], zeros(C_HZ - C_CQ - MLA_Q_LORA),
        wi[:, :, o[4]:o[8]], wi[:, :, o[8]:o[9]], kr, kr_rot, zeros(C_END - C_KR - 2 * MLA_ROPE)], axis=-1))
    uq = w["mla_w_uq"].reshape(depth, MLA_Q_LORA, MLA_HEADS, MLA_NOPE + MLA_ROPE)
    nope = uq[..., :MLA_NOPE].reshape(depth, MLA_Q_LORA, MLA_HEADS * MLA_NOPE)
    r1 = uq[..., MLA_NOPE:MLA_NOPE + half]
    r2 = uq[..., MLA_NOPE + half:]
    rope = jnp.concatenate([r1, r2], axis=-1).reshape(depth, MLA_Q_LORA, MLA_HEADS * MLA_ROPE)
    rope_rot = jnp.concatenate([-r2, r1], axis=-1).reshape(depth, MLA_Q_LORA, MLA_HEADS * MLA_ROPE)
    wuq = bf(jnp.concatenate([nope, rope, rope_rot], axis=-1))
    eye_h = jnp.eye(MLA_HEADS, dtype=F32)
    uk = w["mla_w_uk"].reshape(depth, MLA_KV_LORA, MLA_HEADS, MLA_NOPE)
    wuk = bf(jnp.einsum("lrhd,hg->lhdgr", uk, eye_h).reshape(
        depth, MLA_HEADS * MLA_NOPE, MLA_HEADS * MLA_KV_LORA))
    wuv = bf(w["mla_w_uv"].reshape(depth, MLA_KV_LORA, MLA_HEADS, MLA_V).transpose(0, 2, 1, 3))
    eye_g = jnp.eye(SSM_GROUPS, dtype=F32)
    b_blk = lambda b: bf(jnp.einsum("lgpc,gh->lgchp", b, eye_g).reshape(depth, SSM_WIDTH, SSM_FLAT))
    c_blk = lambda c: bf(jnp.einsum("lgcp,gh->lgphc", c, eye_g).reshape(depth, SSM_FLAT, SSM_WIDTH))
    n_moe = w["moe_router_w"].shape[0]
    rw = jnp.concatenate([w["moe_router_w"],
                          jnp.zeros((n_moe, D_MODEL, LANES - N_EXPERTS), F32)], axis=-1)
    rb = jnp.concatenate([w["moe_router_b"], jnp.zeros((n_moe, LANES - N_EXPERTS), F32)],
                         axis=-1).reshape(n_moe, 1, LANES)
    return dict(
        w_in=w_in, wuq=wuq, wuk=wuk, wuv=wuv,
        bre=b_blk(w["ssm_b_re"]), bim=b_blk(w["ssm_b_im"]),
        cre=c_blk(w["ssm_c_re"]), cim=c_blk(w["ssm_c_im"]),
        glu_w=bf(w["ssm_glu_w"]), proj_a=bf(w["proj_a"]), proj_b=bf(w["proj_b"]),
        proj_c=bf(w["proj_c"]), w_out=bf(w["w_out"]), rw=rw, rb=rb,
        hg_norm=jnp.tile(w["hgrn_norm_g"], (1, HG_HEADS)))


def _rope_tables(pos):
    half = MLA_ROPE // 2
    freq = ROPE_THETA ** (-jnp.arange(half, dtype=F32) / half)
    ang = pos.astype(F32)[:, None] * freq[None, :]
    cos, sin = jnp.cos(ang), jnp.sin(ang)
    cosk = jnp.concatenate([cos, cos], axis=-1)
    sink = jnp.concatenate([sin, sin], axis=-1)
    return jnp.tile(cosk, (1, MLA_HEADS)), jnp.tile(sink, (1, MLA_HEADS)), cosk, sink


def _state_to_blocks(s):
    eye = jnp.eye(HG_HEADS, dtype=F32)
    return jnp.einsum("bhkv,hg->bhvgk", s, eye).reshape(s.shape[0], HG_W, HG_W)


def _blocks_to_state(st):
    b = st.shape[0]
    return jnp.einsum("bhvhk->bhkv", st.reshape(b, HG_HEADS, HG_K, HG_HEADS, HG_K))


def _trunk(x, mod_all, w, rw, prm, pos0, ssm_re0, ssm_im0, hgrn0, cache_lat, cache_kr, page_table):
    bz, length, _ = x.shape
    m = bz * length
    depth = w["w_in"].shape[0]
    prompt = cache_lat is None
    abr, abi, cr, ci, lbs = prm
    if prompt:
        grp = _Group(bz, length, min(256, length), per_row=False)
        grp_f = _Group(bz, length, min(1024, length), per_row=False)
        tl = min(128, length)
        ch = min(32, length)
        tb = min(256, length)
    else:
        grp = _Group(bz, length, m, per_row=True)
        grp_f = grp
        tl, ch, tb = length, length, length
    pos = pos0 + jnp.arange(length, dtype=jnp.int32)
    tables = _rope_tables(pos)
    if not prompt:
        tables = tuple(jnp.tile(t, (bz, 1)) for t in tables)
    cosq, sinq, cosk, sink = tables
    x = x.reshape(m, D_MODEL)
    lat_l, kr_l, sre_l, sim_l, hg_l = [], [], [], [], []
    row = lambda a: a.reshape(1, -1)
    for l in range(depth):
        mod = mod_all[l]
        mod = mod[:, None, :] if prompt else jnp.repeat(mod, length, axis=0)
        u, hz, gates, ql, qr, lat, latb, kr, krb = _in_proj(
            grp, x, mod, row(w["norm1_g"][l]), rw["w_in"][l], row(w["mla_q_norm_g"][l]), rw["wuq"][l],
            rw["wuk"][l], row(w["mla_kv_norm_g"][l]), cosq, sinq, cosk, sink)
        if prompt:
            h0r = jnp.zeros((bz, SSM_FLAT), F32)
            h0i = h0r
            u_tm = u.reshape(length, bz, SSM_WIDTH)
        else:
            h0r = ssm_re0[l].reshape(bz, SSM_FLAT)
            h0i = ssm_im0[l].reshape(bz, SSM_FLAT)
            u_tm = u.reshape(bz, length, SSM_WIDTH).transpose(1, 0, 2)
        ya, htr, hti = _s5(u_tm, h0r, h0i, row(abr[l]), row(abi[l]), row(cr[l]), row(ci[l]),
                           rw["bre"][l], rw["bim"][l], rw["cre"][l], rw["cim"][l], row(w["ssm_d"][l]),
                           rw["glu_w"][l], row(w["ssm_glu_b"][l]), tl)
        if prompt:
            ya = ya.reshape(length, bz * SSM_WIDTH)
        else:
            ya = ya.transpose(1, 0, 2).reshape(m, SSM_WIDTH)
        if prompt:
            yb = _attn_prompt(bz, length, ql, qr, latb, krb, rw["wuv"][l], grp.tm)
        else:
            yb = _attn_sample(l, page_table, ql, qr, latb, krb, cache_lat, cache_kr, rw["wuv"][l], bz, length)
        st0 = jnp.zeros((bz, HG_W, HG_W), F32) if prompt else _state_to_blocks(hgrn0[l])
        yc, st_t = _hgrn(bz, length, hz, st0, row(lbs[l]), row(rw["hg_norm"][l]), ch, tb)
        j = l // 2
        common = (grp, x, ya, yb, yc, gates, mod, row(w["norm2_g"][l]), rw["proj_a"][l], rw["proj_b"][l],
                  rw["proj_c"][l], rw["w_out"][l])
        if l % 2 == 0:
            x, h2 = _mix(*common)
            x = _ffn(grp_f, h2, x, mod, w["ffn_w_gate"][j], w["ffn_w_up"][j], w["ffn_w_down"][j], 256)
        else:
            x, h2, gate = _mix(*common, router=(rw["rw"][j], rw["rb"][j]))
            x = _moe(grp_f, h2, x, mod, gate, w["moe_w_gate"][j], w["moe_w_up"][j], w["moe_w_down"][j], 256)
        lat_l.append(lat.reshape(bz, length, MLA_KV_LORA))
        kr_l.append(kr.reshape(bz, length, MLA_ROPE))
        sre_l.append(htr.reshape(bz, SSM_GROUPS, SSM_STATE))
        sim_l.append(hti.reshape(bz, SSM_GROUPS, SSM_STATE))
        hg_l.append(_blocks_to_state(st_t))
    y = _final_norm(grp, x, row(w["final_norm_g"])).reshape(bz, length, D_MODEL)
    return y, jnp.stack(lat_l), jnp.stack(kr_l), jnp.stack(sre_l), jnp.stack(sim_l), jnp.stack(hg_l)


def kernel(x_prompt, x_sample, c_prompt, c_sample, cache_kv_latent, cache_k_rope, state_ssm_re, state_ssm_im,
           state_hgrn, page_table, ada_w, ada_b, norm1_g, norm2_g, w_in, ssm_a_re, ssm_a_im, ssm_log_dt,
           ssm_b_re, ssm_b_im, ssm_c_re, ssm_c_im, ssm_d, ssm_glu_w, ssm_glu_b, mla_q_norm_g, mla_w_uq,
           mla_kv_norm_g, mla_w_uk, mla_w_uv, hgrn_lb_logits, hgrn_norm_g, proj_a, proj_b, proj_c, w_out,
           ffn_w_gate, ffn_w_up, ffn_w_down, moe_router_w, moe_router_b, moe_w_gate, moe_w_up, moe_w_down,
           final_norm_g):
    w = dict(norm1_g=norm1_g, norm2_g=norm2_g, w_in=w_in, ssm_b_re=ssm_b_re, ssm_b_im=ssm_b_im,
             ssm_c_re=ssm_c_re, ssm_c_im=ssm_c_im, ssm_d=ssm_d, ssm_glu_w=ssm_glu_w, ssm_glu_b=ssm_glu_b,
             mla_q_norm_g=mla_q_norm_g, mla_w_uq=mla_w_uq, mla_kv_norm_g=mla_kv_norm_g, mla_w_uk=mla_w_uk,
             mla_w_uv=mla_w_uv, hgrn_norm_g=hgrn_norm_g, proj_a=proj_a, proj_b=proj_b, proj_c=proj_c,
             w_out=w_out, ffn_w_gate=ffn_w_gate, ffn_w_up=ffn_w_up, ffn_w_down=ffn_w_down,
             moe_router_w=moe_router_w, moe_router_b=moe_router_b, moe_w_gate=moe_w_gate,
             moe_w_up=moe_w_up, moe_w_down=moe_w_down, final_norm_g=final_norm_g)
    rw = _relayout_weights(w)
    prm = _prep_params(ssm_a_re, ssm_a_im, ssm_log_dt, hgrn_lb_logits)
    n_p = c_prompt.shape[0]
    mod_all = _ada_mod(jnp.concatenate([c_prompt, c_sample], axis=0), ada_w, ada_b)
    y_p, lat_p, kr_p, sre_p, sim_p, hg_p = _trunk(
        x_prompt, mod_all[:, :n_p], w, rw, prm, 0, None, None, None, None, None, None)
    past_len = page_table.shape[1] * cache_kv_latent.shape[2]
    y_s, lat_s, kr_s, sre_s, sim_s, hg_s = _trunk(
        x_sample, mod_all[:, n_p:], w, rw, prm, past_len, state_ssm_re, state_ssm_im, state_hgrn,
        cache_kv_latent, cache_k_rope, page_table)
    return (y_p, y_s, lat_p, kr_p, sre_p, sim_p, hg_p, lat_s, kr_s, sre_s, sim_s, hg_s)
```

```python
import functools

import jax
import jax.numpy as jnp
from jax import lax
from jax.experimental import pallas as pl
from jax.experimental.pallas import tpu as pltpu

F32 = jnp.float32
BF16 = jnp.bfloat16

D_MODEL = 1024
SSM_GROUPS = 16
SSM_GROUP_CH = 16
SSM_WIDTH = 256
SSM_STATE = 64
SSM_FLAT = SSM_GROUPS * SSM_STATE
MLA_HEADS = 8
MLA_NOPE = 64
MLA_ROPE = 32
MLA_V = 64
MLA_Q_LORA = 384
MLA_KV_LORA = 256
MLA_SCALE = (MLA_NOPE + MLA_ROPE) ** -0.5
ROPE_THETA = 10000.0
HG_HEADS = 4
HG_K = 64
HG_W = 256
FF_DIM = 2816
N_EXPERTS = 8
EPS = 1e-6

LANES = 128
SUBLANES = 8
VMEM_LIMIT = 52 * 1024 * 1024

C_U, C_CKV, C_CQ, C_HZ, C_GATES, C_KR, C_KR_ROT, C_END = 0, 256, 512, 1024, 2048, 5120, 5248, 5376
ROPE_PAD = LANES


def _mm(a, b):
    return jnp.dot(a, b, preferred_element_type=F32)


def _mm_nt(a, b):
    return lax.dot_general(a, b, (((1,), (1,)), ((), ())), preferred_element_type=F32)


def _mm_tn(a, b):
    return lax.dot_general(a, b, (((0,), (0,)), ((), ())), preferred_element_type=F32)


def _mm_f32(a, b):
    return jnp.dot(a, b, preferred_element_type=F32, precision=lax.Precision.HIGHEST)


def _mm_split(a, b):
    ah, bh = a.astype(BF16), b.astype(BF16)
    al = (a - ah.astype(F32)).astype(BF16)
    bl = (b - bh.astype(F32)).astype(BF16)
    return _mm(ah, bh) + (_mm(ah, bl) + _mm(al, bh))


def _rms(x, g):
    return x * lax.rsqrt(jnp.mean(x * x, axis=-1, keepdims=True) + EPS) * g


def _params(sem):
    return pltpu.CompilerParams(dimension_semantics=sem, vmem_limit_bytes=VMEM_LIMIT)


def _const_spec(shape):
    nd = len(shape)
    return pl.BlockSpec(shape, lambda *_: (0,) * nd)


class _Group:
    def __init__(self, bz, length, tm, per_row):
        self.bz, self.length, self.tm, self.per_row = bz, length, tm, per_row
        self.m = bz * length
        if per_row:
            self.nb = self.m // tm
            self.grid = (1, self.nb)
        else:
            self.nb = length // tm
            self.grid = (bz, self.nb)

    def rows(self, width):
        nb = self.nb
        return pl.BlockSpec((self.tm, width), lambda b, i, *_: (b * nb + i, 0))

    def mod(self, col):
        if self.per_row:
            return pl.BlockSpec((self.tm, D_MODEL), lambda b, i, *_: (i, col))
        return pl.BlockSpec((None, 1, D_MODEL), lambda b, i, *_: (b, 0, col))

    def pos(self, width):
        return pl.BlockSpec((self.tm, width), lambda b, i, *_: (i, 0))

    def tmajor(self, width):
        if self.per_row:
            return self.rows(width)
        return pl.BlockSpec((self.tm, width), lambda b, i, *_: (i, b))

    def tmajor_shape(self, width):
        return (self.m, width) if self.per_row else (self.length, self.bz * width)


def _param_kernel(ar_ref, ai_ref, ldt_ref, lbl_ref, abr_ref, abi_ref, cr_ref, ci_ref, lbs_ref):
    ar, ai = ar_ref[...], ai_ref[...]
    dt = jnp.exp(ldt_ref[...])
    mag = jnp.exp(dt * ar)
    abr, abi = mag * jnp.cos(dt * ai), mag * jnp.sin(dt * ai)
    den = ar * ar + ai * ai
    abr_ref[...] = abr
    abi_ref[...] = abi
    cr_ref[...] = ((abr - 1.0) * ar + abi * ai) / den
    ci_ref[...] = (abi * ar - (abr - 1.0) * ai) / den
    x = lbl_ref[...]
    e = jnp.exp(x - jnp.max(x, axis=0, keepdims=True))
    p = e / jnp.sum(e, axis=0, keepdims=True)
    rows, acc = [], p[0:1]
    for l in range(x.shape[0]):
        if l:
            acc = acc + p[l:l + 1]
        rows.append(acc - p[0:1])
    lbs_ref[...] = jnp.concatenate(rows, axis=0)


def _prep_params(a_re, a_im, log_dt, lb_logits):
    depth = a_re.shape[0]
    ar = a_re.reshape(depth, SSM_FLAT)
    ai = a_im.reshape(depth, SSM_FLAT)
    ldt = jnp.broadcast_to(log_dt[:, :, None], (depth, SSM_GROUPS, SSM_STATE)).reshape(depth, SSM_FLAT)
    flat = jax.ShapeDtypeStruct((depth, SSM_FLAT), F32)
    return pl.pallas_call(
        _param_kernel,
        out_shape=(flat, flat, flat, flat, jax.ShapeDtypeStruct((depth, HG_W), F32)),
        name="param_prep",
    )(ar, ai, ldt, lb_logits)


def _ada_kernel(c_ref, w_ref, b_ref, o_ref):
    c = c_ref[...]
    s = (c * jax.nn.sigmoid(c)).astype(BF16)
    o_ref[...] = _mm(s, w_ref[...].astype(BF16)) + b_ref[...]


def _ada_mod(c_all, ada_w, ada_b):
    depth, d, n = ada_w.shape
    rows = c_all.shape[0]
    tn = 1536
    return pl.pallas_call(
        _ada_kernel,
        grid=(depth, n // tn),
        in_specs=[pl.BlockSpec((rows, d), lambda l, j: (0, 0)),
                  pl.BlockSpec((None, d, tn), lambda l, j: (l, 0, j)),
                  pl.BlockSpec((None, 1, tn), lambda l, j: (l, 0, j))],
        out_specs=pl.BlockSpec((None, rows, tn), lambda l, j: (l, 0, j)),
        out_shape=jax.ShapeDtypeStruct((depth, rows, n), F32),
        compiler_params=_params(("arbitrary", "arbitrary")),
        name="ada_mod",
    )(c_all, ada_w, ada_b.reshape(depth, 1, n))


def _in_kernel(x_ref, sh_ref, sc_ref, g_ref, w_ref, qg_ref, wuq_ref, wuk_ref, kvg_ref,
               cq_ref, sq_ref, ck_ref, sk_ref,
               u_ref, hz_ref, gates_ref, ql_ref, qr_ref, lat_ref, latb_ref, kr_ref, krb_ref):
    x = x_ref[...]
    h = (_rms(x, g_ref[...]) * (1.0 + sc_ref[...]) + sh_ref[...]).astype(BF16)

    def seg(a, b):
        return _mm(h, w_ref[:, a:b])

    u_ref[...] = seg(C_U, C_CKV)
    hz_ref[...] = seg(C_HZ, C_GATES)
    gates_ref[...] = seg(C_GATES, C_KR)
    cqn = _rms(seg(C_CQ, C_CQ + MLA_Q_LORA), qg_ref[...]).astype(BF16)
    q = _mm(cqn, wuq_ref[...])
    n_nope = MLA_HEADS * MLA_NOPE
    n_rope = MLA_HEADS * ROPE_PAD
    ql = _mm(q[:, :n_nope].astype(BF16), wuk_ref[...]).astype(BF16)
    qr = (q[:, n_nope:n_nope + n_rope] * cq_ref[...] + q[:, n_nope + n_rope:] * sq_ref[...]).astype(BF16)
    for hd in range(MLA_HEADS):
        ql_ref[hd] = ql[:, hd * MLA_KV_LORA:(hd + 1) * MLA_KV_LORA]
        qr_ref[hd] = qr[:, hd * ROPE_PAD:(hd + 1) * ROPE_PAD]
    lat = _rms(seg(C_CKV, C_CQ), kvg_ref[...])
    lat_ref[...] = lat
    latb_ref[...] = lat.astype(BF16)
    kr = seg(C_KR, C_KR_ROT) * ck_ref[...] + seg(C_KR_ROT, C_END) * sk_ref[...]
    kr_ref[...] = kr[:, :MLA_ROPE]
    krb_ref[...] = kr.astype(BF16)


def _in_proj(grp, x, mod, g1, w_in, qg, wuq, wuk, kvg, cosq, sinq, cosk, sink):
    m, tm = grp.m, grp.tm
    n_rope = MLA_HEADS * ROPE_PAD
    nb = grp.nb

    def heads(width):
        return pl.BlockSpec((MLA_HEADS, tm, width), lambda b, i: (0, b * nb + i, 0))

    outs = [
        (grp.tmajor_shape(SSM_WIDTH), F32, grp.tmajor(SSM_WIDTH)),
        ((m, 4 * HG_W), F32, grp.rows(4 * HG_W)),
        ((m, 3 * D_MODEL), F32, grp.rows(3 * D_MODEL)),
        ((MLA_HEADS, m, MLA_KV_LORA), BF16, heads(MLA_KV_LORA)),
        ((MLA_HEADS, m, ROPE_PAD), BF16, heads(ROPE_PAD)),
        ((m, MLA_KV_LORA), F32, grp.rows(MLA_KV_LORA)),
        ((m, MLA_KV_LORA), BF16, grp.rows(MLA_KV_LORA)),
        ((m, MLA_ROPE), F32, grp.rows(MLA_ROPE)),
        ((m, ROPE_PAD), BF16, grp.rows(ROPE_PAD)),
    ]
    return pl.pallas_call(
        _in_kernel,
        grid=grp.grid,
        in_specs=[grp.rows(D_MODEL), grp.mod(0), grp.mod(1), _const_spec((1, D_MODEL)),
                  _const_spec(w_in.shape), _const_spec((1, MLA_Q_LORA)), _const_spec(wuq.shape),
                  _const_spec(wuk.shape), _const_spec((1, MLA_KV_LORA)),
                  grp.pos(n_rope), grp.pos(n_rope), grp.pos(ROPE_PAD), grp.pos(ROPE_PAD)],
        out_specs=[o[2] for o in outs],
        out_shape=[jax.ShapeDtypeStruct(o[0], o[1]) for o in outs],
        compiler_params=_params(("arbitrary", "arbitrary")),
        name="in_proj",
    )(x, mod, mod, g1, w_in, qg, wuq, wuk, kvg, cosq, sinq, cosk, sink)


def _s5_kernel(u_ref, h0r_ref, h0i_ref, abr_ref, abi_ref, cr_ref, ci_ref, bre_ref, bim_ref,
               cre_ref, cim_ref, d_ref, gw_ref, gb_ref,
               y_ref, htr_ref, hti_ref, hr_s, hi_s, xr_s, xi_s, *, tl, bz):
    i = pl.program_id(0)

    @pl.when(i == 0)
    def _():
        hr_s[...] = h0r_ref[...]
        hi_s[...] = h0i_ref[...]

    u = u_ref[...].reshape(tl * bz, SSM_WIDTH)
    ub = u.astype(BF16)
    bur, bui = _mm(ub, bre_ref[...]), _mm(ub, bim_ref[...])
    cr, ci = cr_ref[...], ci_ref[...]
    xr_s[...] = cr * bur - ci * bui
    xi_s[...] = cr * bui + ci * bur
    abr, abi = abr_ref[...], abi_ref[...]

    def step(t, carry):
        hr, hi = carry
        r0 = pl.multiple_of(t * bz, bz)
        nr = abr * hr - abi * hi + xr_s[pl.ds(r0, bz), :]
        ni = abr * hi + abi * hr + xi_s[pl.ds(r0, bz), :]
        xr_s[pl.ds(r0, bz), :] = nr
        xi_s[pl.ds(r0, bz), :] = ni
        return nr, ni

    hr, hi = lax.fori_loop(0, tl, step, (hr_s[...], hi_s[...]))
    hr_s[...] = hr
    hi_s[...] = hi
    htr_ref[...] = hr
    hti_ref[...] = hi
    y = (_mm(xr_s[...].astype(BF16), cre_ref[...]) - _mm(xi_s[...].astype(BF16), cim_ref[...])
         + d_ref[...] * u)
    y = jax.nn.gelu(y)
    y = y * jax.nn.sigmoid(_mm(y.astype(BF16), gw_ref[...]) + gb_ref[...])
    y_ref[...] = y.reshape(tl, bz, SSM_WIDTH)


def _s5(u_tm, h0r, h0i, abr, abi, cr, ci, bre, bim, cre, cim, d, gw, gb, tl):
    length, bz, _ = u_tm.shape
    row = _const_spec((1, SSM_FLAT))
    st = _const_spec((bz, SSM_FLAT))
    blk = pl.BlockSpec((tl, bz, SSM_WIDTH), lambda i: (i, 0, 0))
    return pl.pallas_call(
        functools.partial(_s5_kernel, tl=tl, bz=bz),
        grid=(length // tl,),
        in_specs=[blk, st, st, row, row, row, row, _const_spec(bre.shape), _const_spec(bim.shape),
                  _const_spec(cre.shape), _const_spec(cim.shape), _const_spec((1, SSM_WIDTH)),
                  _const_spec(gw.shape), _const_spec((1, SSM_WIDTH))],
        out_specs=[blk, st, st],
        out_shape=[jax.ShapeDtypeStruct(u_tm.shape, F32), jax.ShapeDtypeStruct((bz, SSM_FLAT), F32),
                   jax.ShapeDtypeStruct((bz, SSM_FLAT), F32)],
        scratch_shapes=[pltpu.VMEM((bz, SSM_FLAT), F32), pltpu.VMEM((bz, SSM_FLAT), F32),
                        pltpu.VMEM((tl * bz, SSM_FLAT), F32), pltpu.VMEM((tl * bz, SSM_FLAT), F32)],
        compiler_params=_params(("arbitrary",)),
        name="s5_scan",
    )(u_tm, h0r, h0i, abr, abi, cr, ci, bre, bim, cre, cim, d, gw, gb)


def _softmax_step(s, keys, m_ref, l_ref, acc_ref):
    m_old = m_ref[...]
    m_new = jnp.maximum(m_old, jnp.max(s, axis=-1, keepdims=True))
    alpha = jnp.exp(m_old - m_new)
    p = jnp.exp(s - jnp.tile(m_new, (1, s.shape[-1] // LANES)))
    l_ref[...] = alpha * l_ref[...] + jnp.sum(p, axis=-1, keepdims=True)
    acc_ref[...] = (acc_ref[...] * jnp.tile(alpha, (1, acc_ref.shape[-1] // LANES))
                    + _mm(p.astype(BF16), keys))
    m_ref[...] = m_new


def _attn_prompt_kernel(ql_ref, qr_ref, lat_ref, kr_ref, wuv_ref, y_ref, m_s, l_s, acc_s, *, tq, tk):
    i = pl.program_id(1)
    rows = MLA_HEADS * tq
    m_s[...] = jnp.full(m_s.shape, -jnp.inf, F32)
    l_s[...] = jnp.zeros(l_s.shape, F32)
    acc_s[...] = jnp.zeros(acc_s.shape, F32)

    def block(j, masked):
        k0 = pl.multiple_of(j * tk, tk)
        kl = lat_ref[pl.ds(k0, tk), :]
        kr = kr_ref[pl.ds(k0, tk), :]
        s = (_mm_nt(ql_ref[...].reshape(rows, MLA_KV_LORA), kl)
             + _mm_nt(qr_ref[...].reshape(rows, ROPE_PAD), kr)) * MLA_SCALE
        if masked:
            q_pos = i * tq + lax.broadcasted_iota(jnp.int32, (rows, tk), 0) % tq
            k_pos = k0 + lax.broadcasted_iota(jnp.int32, (rows, tk), 1)
            s = jnp.where(k_pos <= q_pos, s, -jnp.inf)
        _softmax_step(s, kl, m_s, l_s, acc_s)

    n_full = (i * tq) // tk

    def full_block(j, carry):
        block(j, False)
        return carry

    lax.fori_loop(0, n_full, full_block, 0)
    block(n_full, True)
    ctx = acc_s[...] / jnp.tile(l_s[...], (1, MLA_KV_LORA // LANES))
    ys = [_mm(ctx[h * tq:(h + 1) * tq, :].astype(BF16), wuv_ref[h]) for h in range(MLA_HEADS)]
    y_ref[...] = jnp.concatenate(ys, axis=-1).astype(BF16)


def _attn_prompt(grp_bz, length, ql, qr, latb, krb, wuv, tq, tk):
    nb = length // tq
    rows = MLA_HEADS * tq
    heads = lambda w: pl.BlockSpec((MLA_HEADS, tq, w), lambda b, i: (0, b * nb + i, 0))
    return pl.pallas_call(
        functools.partial(_attn_prompt_kernel, tq=tq, tk=tk),
        grid=(grp_bz, nb),
        in_specs=[heads(MLA_KV_LORA), heads(ROPE_PAD),
                  pl.BlockSpec((None, length, MLA_KV_LORA), lambda b, i: (b, 0, 0)),
                  pl.BlockSpec((None, length, ROPE_PAD), lambda b, i: (b, 0, 0)),
                  _const_spec(wuv.shape)],
        out_specs=pl.BlockSpec((tq, MLA_HEADS * MLA_V), lambda b, i: (b * nb + i, 0)),
        out_shape=jax.ShapeDtypeStruct((grp_bz * length, MLA_HEADS * MLA_V), BF16),
        scratch_shapes=[pltpu.VMEM((rows, LANES), F32), pltpu.VMEM((rows, LANES), F32),
                        pltpu.VMEM((rows, MLA_KV_LORA), F32)],
        compiler_params=_params(("arbitrary", "arbitrary")),
        name="attn_prompt",
    )(ql, qr, latb.reshape(grp_bz, length, MLA_KV_LORA), krb.reshape(grp_bz, length, ROPE_PAD), wuv)


def _attn_sample_kernel(pt_ref, ql_ref, qr_ref, nl_ref, nk_ref, wuv_ref, *rest, pg, ns, lq):
    lat_refs, krt_refs = rest[:pg], rest[pg:2 * pg]
    y_ref, m_s, l_s, acc_s = rest[2 * pg:]
    s_idx = pl.program_id(1)
    n_steps = pl.num_programs(1)

    @pl.when(s_idx == 0)
    def _():
        m_s[...] = jnp.full(m_s.shape, -jnp.inf, F32)
        l_s[...] = jnp.zeros(l_s.shape, F32)
        acc_s[...] = jnp.zeros(acc_s.shape, F32)

    q_l, q_r = ql_ref[...], qr_ref[...]
    per = pg // ns
    for st in range(ns):
        keys = jnp.concatenate([r[...].astype(BF16) for r in lat_refs[st * per:(st + 1) * per]], axis=0)
        krt = jnp.concatenate([r[...].astype(BF16) for r in krt_refs[st * per:(st + 1) * per]], axis=1)
        s = (_mm_nt(q_l, keys) + _mm(q_r, krt)) * MLA_SCALE
        _softmax_step(s, keys, m_s.at[st], l_s.at[st], acc_s.at[st])

    @pl.when(s_idx == n_steps - 1)
    def _():
        rows = MLA_HEADS * lq
        nl, nk = nl_ref[...], nk_ref[...]
        s = (_mm_nt(q_l, nl) + _mm_nt(q_r, nk)) * MLA_SCALE
        q_t = lax.broadcasted_iota(jnp.int32, (rows, lq), 0) % lq
        k_t = lax.broadcasted_iota(jnp.int32, (rows, lq), 1)
        s = jnp.where(k_t <= q_t, s, -jnp.inf)
        m_old = m_s[0]
        m_new = jnp.maximum(m_old, jnp.max(s, axis=-1, keepdims=True))
        for st in range(1, ns):
            m_new = jnp.maximum(m_new, m_s[st])
        p = jnp.exp(s - m_new[:, :1])
        l_tot = jnp.sum(p, axis=-1, keepdims=True)
        acc = _mm(p.astype(BF16), nl)
        for st in range(ns):
            a = jnp.exp(m_s[st] - m_new)
            l_tot = l_tot + a * l_s[st]
            acc = acc + jnp.tile(a, (1, MLA_KV_LORA // LANES)) * acc_s[st]
        ctx = acc / jnp.tile(l_tot, (1, MLA_KV_LORA // LANES))
        ys = [_mm(ctx[h * lq:(h + 1) * lq, :].astype(BF16), wuv_ref[h]) for h in range(MLA_HEADS)]
        y_ref[...] = jnp.concatenate(ys, axis=-1).astype(BF16)


def _attn_sample(layer, page_table, ql, qr, latb, krb, cache_lat, cache_krt, wuv, bz, lq):
    n_pages = page_table.shape[1]
    pg = min(16, n_pages)
    ns = 2 if pg % 2 == 0 else 1
    page = cache_lat.shape[2]
    rows = MLA_HEADS * lq
    qlh = ql.reshape(MLA_HEADS, bz, lq, MLA_KV_LORA).transpose(1, 0, 2, 3).reshape(bz, rows, MLA_KV_LORA)
    qrh = qr[:, :, :MLA_ROPE].reshape(MLA_HEADS, bz, lq, MLA_ROPE).transpose(1, 0, 2, 3).reshape(
        bz, rows, MLA_ROPE)

    def seq(n, w):
        return pl.BlockSpec((None, n, w), lambda b, s, pt: (b, 0, 0))

    in_specs = ([seq(rows, MLA_KV_LORA), seq(rows, MLA_ROPE), seq(lq, MLA_KV_LORA), seq(lq, MLA_ROPE),
                 pl.BlockSpec(wuv.shape, lambda b, s, pt: (0, 0, 0))]
                + [pl.BlockSpec((None, None, page, MLA_KV_LORA),
                                lambda b, s, pt, i=i: (layer, pt[b, s * pg + i], 0, 0)) for i in range(pg)]
                + [pl.BlockSpec((None, None, MLA_ROPE, page),
                                lambda b, s, pt, i=i: (layer, pt[b, s * pg + i], 0, 0)) for i in range(pg)])
    y = pl.pallas_call(
        functools.partial(_attn_sample_kernel, pg=pg, ns=ns, lq=lq),
        grid_spec=pltpu.PrefetchScalarGridSpec(
            num_scalar_prefetch=1,
            grid=(bz, n_pages // pg),
            in_specs=in_specs,
            out_specs=pl.BlockSpec((None, lq, MLA_HEADS * MLA_V), lambda b, s, pt: (b, 0, 0)),
            scratch_shapes=[pltpu.VMEM((ns, rows, LANES), F32), pltpu.VMEM((ns, rows, LANES), F32),
                            pltpu.VMEM((ns, rows, MLA_KV_LORA), F32)]),
        out_shape=jax.ShapeDtypeStruct((bz, lq, MLA_HEADS * MLA_V), BF16),
        compiler_params=_params(("arbitrary", "arbitrary")),
        name="attn_sample",
    )(page_table, qlh, qrh, latb.reshape(bz, lq, MLA_KV_LORA),
      krb[:, :MLA_ROPE].reshape(bz, lq, MLA_ROPE), wuv, *([cache_lat] * pg), *([cache_krt] * pg))
    return y.reshape(bz * lq, MLA_HEADS * MLA_V)


def _hgrn_kernel(hz_ref, st0_ref, lb_ref, ng_ref, y_ref, stt_ref, st_s, w_s, *, ch, tb):
    j = pl.program_id(1)

    @pl.when(j == 0)
    def _():
        st_s[...] = st0_ref[...]

    lb = lb_ref[...]
    log_lb, log1m_lb, one_m_lb = jnp.log(lb), jnp.log1p(-lb), 1.0 - lb
    tri = (lax.broadcasted_iota(jnp.int32, (ch, ch), 0)
           >= lax.broadcasted_iota(jnp.int32, (ch, ch), 1)).astype(F32)
    same_head = (lax.broadcasted_iota(jnp.int32, (HG_W, HG_W), 0) // HG_K
                 == lax.broadcasted_iota(jnp.int32, (HG_W, HG_W), 1) // HG_K)
    head_ones = same_head.astype(F32)
    head_ones_b = same_head.astype(BF16)
    ng = ng_ref[...]

    def chunk(c, carry):
        r0 = pl.multiple_of(c * ch, ch)
        q = hz_ref[pl.ds(r0, ch), 0:HG_W]
        fp = hz_ref[pl.ds(r0, ch), HG_W:2 * HG_W]
        v = hz_ref[pl.ds(r0, ch), 2 * HG_W:3 * HG_W]
        g = hz_ref[pl.ds(r0, ch), 3 * HG_W:4 * HG_W]
        logf = jnp.logaddexp(log_lb, log1m_lb + jax.nn.log_sigmoid(fp))
        k = one_m_lb * jax.nn.sigmoid(-fp)
        b = _mm_f32(tri, logf)
        st = st_s[...]
        o = _mm_nt((q * jnp.exp(b)).astype(BF16), st.astype(BF16))
        starts, r = [], 0
        for s in range(ch):
            t0 = (s // SUBLANES) * SUBLANES
            t_idx = t0 + lax.broadcasted_iota(jnp.int32, (ch - t0, HG_W), 0)
            e = jnp.exp(jnp.where(t_idx >= s, b[t0:] - b[s:s + 1, :], -jnp.inf))
            w_s[r:r + ch - t0, :] = q[t0:] * k[s:s + 1, :] * e
            starts.append(r)
            r += ch - t0
        att = _mm(w_s[...].astype(BF16), head_ones_b)
        tiles = [o[t0:t0 + SUBLANES] for t0 in range(0, ch, SUBLANES)]
        for s in range(ch):
            for kt in range(s // SUBLANES, ch // SUBLANES):
                a0 = starts[s] + (kt - s // SUBLANES) * SUBLANES
                tiles[kt] = tiles[kt] + att[a0:a0 + SUBLANES, :] * v[s:s + 1, :]
        o = jnp.concatenate(tiles, axis=0)
        bl = b[ch - 1:ch, :]
        kd = k * jnp.exp(bl - b)
        upd = _mm_tn(v.astype(BF16), kd.astype(BF16))
        st_s[...] = st * jnp.exp(bl) + jnp.where(same_head, upd, 0.0)
        ms = _mm_f32(o * o, head_ones) * (1.0 / HG_K)
        on = o * lax.rsqrt(ms + EPS) * ng
        y_ref[pl.ds(r0, ch), :] = on * (g * jax.nn.sigmoid(g))
        return carry

    lax.fori_loop(0, tb // ch, chunk, 0)

    @pl.when(j == pl.num_programs(1) - 1)
    def _():
        stt_ref[...] = st_s[...]


def _hgrn(bz, length, hz, st0, lb, ng, ch, tb):
    nb = length // tb
    st_spec = pl.BlockSpec((None, HG_W, HG_W), lambda b, j: (b, 0, 0))
    return pl.pallas_call(
        functools.partial(_hgrn_kernel, ch=ch, tb=tb),
        grid=(bz, nb),
        in_specs=[pl.BlockSpec((tb, 4 * HG_W), lambda b, j: (b * nb + j, 0)), st_spec,
                  _const_spec((1, HG_W)), _const_spec((1, HG_W))],
        out_specs=[pl.BlockSpec((tb, HG_W), lambda b, j: (b * nb + j, 0)), st_spec],
        out_shape=[jax.ShapeDtypeStruct((bz * length, HG_W), F32),
                   jax.ShapeDtypeStruct((bz, HG_W, HG_W), F32)],
        scratch_shapes=[pltpu.VMEM((HG_W, HG_W), F32),
                        pltpu.VMEM((sum(ch - s // SUBLANES * SUBLANES for s in range(ch)), HG_W), F32)],
        compiler_params=_params(("arbitrary", "arbitrary")),
        name="hgrn2",
    )(hz, st0, lb, ng)


def _mix_kernel(*refs, moe):
    if moe:
        (x_ref, ya_ref, yb_ref, yc_ref, gates_ref, g1_ref, sh2_ref, sc2_ref, n2_ref,
         pa_ref, pb_ref, pc_ref, wo_ref, rw_ref, rb_ref, xo_ref, h2_ref, gate_ref) = refs
    else:
        (x_ref, ya_ref, yb_ref, yc_ref, gates_ref, g1_ref, sh2_ref, sc2_ref, n2_ref,
         pa_ref, pb_ref, pc_ref, wo_ref, xo_ref, h2_ref) = refs
    d = D_MODEL
    mix = (jax.nn.sigmoid(gates_ref[:, 0:d]) * _mm(ya_ref[...].astype(BF16), pa_ref[...])
           + jax.nn.sigmoid(gates_ref[:, d:2 * d]) * _mm(yb_ref[...], pb_ref[...])
           + jax.nn.sigmoid(gates_ref[:, 2 * d:3 * d]) * _mm(yc_ref[...].astype(BF16), pc_ref[...]))
    xo = x_ref[...] + g1_ref[...] * _mm(mix.astype(BF16), wo_ref[...])
    xo_ref[...] = xo
    h2 = _rms(xo, n2_ref[...]) * (1.0 + sc2_ref[...]) + sh2_ref[...]
    h2_ref[...] = h2.astype(BF16)
    if moe:
        logits = _mm_split(h2, rw_ref[...]) + rb_ref[...]
        lane = lax.broadcasted_iota(jnp.int32, logits.shape, 1)
        lg = jnp.where(lane < N_EXPERTS, logits, -jnp.inf)
        m1 = jnp.max(lg, axis=-1, keepdims=True)
        i1 = jnp.min(jnp.where(lg == m1, lane, LANES), axis=-1, keepdims=True)
        lg2 = jnp.where(lane == i1, -jnp.inf, lg)
        m2 = jnp.max(lg2, axis=-1, keepdims=True)
        i2 = jnp.min(jnp.where(lg2 == m2, lane, LANES), axis=-1, keepdims=True)
        e2 = jnp.exp(m2 - m1)
        den = 1.0 + e2
        gate_ref[...] = jnp.where(lane == i1, 1.0 / den, 0.0) + jnp.where(lane == i2, e2 / den, 0.0)


def _mix(grp, x, ya, yb, yc, gates, mod, n2, pa, pb, pc, wo, router=None):
    m = grp.m
    moe = router is not None
    in_specs = [grp.rows(D_MODEL), grp.tmajor(SSM_WIDTH), grp.rows(MLA_HEADS * MLA_V), grp.rows(HG_W),
                grp.rows(3 * D_MODEL), grp.mod(2), grp.mod(3), grp.mod(4), _const_spec((1, D_MODEL)),
                _const_spec(pa.shape), _const_spec(pb.shape), _const_spec(pc.shape), _const_spec(wo.shape)]
    args = [x, ya, yb, yc, gates, mod, mod, mod, n2, pa, pb, pc, wo]
    out_specs = [grp.rows(D_MODEL), grp.rows(D_MODEL)]
    out_shape = [jax.ShapeDtypeStruct((m, D_MODEL), F32), jax.ShapeDtypeStruct((m, D_MODEL), BF16)]
    if moe:
        in_specs += [_const_spec(router[0].shape), _const_spec(router[1].shape)]
        args += list(router)
        out_specs.append(grp.rows(LANES))
        out_shape.append(jax.ShapeDtypeStruct((m, LANES), F32))
    return pl.pallas_call(
        functools.partial(_mix_kernel, moe=moe),
        grid=grp.grid, in_specs=in_specs, out_specs=out_specs, out_shape=out_shape,
        compiler_params=_params(("arbitrary", "arbitrary")),
        name="mix_moe" if moe else "mix",
    )(*args)


def _swiglu_tile(h2_ref, wg_ref, wu_ref, wd_ref):
    h = h2_ref[...]
    g = _mm(h, wg_ref[...].astype(BF16))
    u = _mm(h, wu_ref[...].astype(BF16))
    a = (g * jax.nn.sigmoid(g) * u).astype(BF16)
    return _mm(a, wd_ref[...].astype(BF16))


def _ffn_kernel(h2_ref, x_ref, g2_ref, wg_ref, wu_ref, wd_ref, o_ref, acc_s):
    f = pl.program_id(2)

    @pl.when(f == 0)
    def _():
        acc_s[...] = jnp.zeros(acc_s.shape, F32)

    acc_s[...] += _swiglu_tile(h2_ref, wg_ref, wu_ref, wd_ref)

    @pl.when(f == pl.num_programs(2) - 1)
    def _():
        o_ref[...] = x_ref[...] + g2_ref[...] * acc_s[...]


def _ffn(grp, h2, x, mod, wg, wu, wd, j, tf):
    nf = FF_DIM // tf
    return pl.pallas_call(
        _ffn_kernel,
        grid=grp.grid + (nf,),
        in_specs=[grp.rows(D_MODEL), grp.rows(D_MODEL), grp.mod(5),
                  pl.BlockSpec((None, D_MODEL, tf), lambda b, i, f: (j, 0, f)),
                  pl.BlockSpec((None, D_MODEL, tf), lambda b, i, f: (j, 0, f)),
                  pl.BlockSpec((None, tf, D_MODEL), lambda b, i, f: (j, f, 0))],
        out_specs=grp.rows(D_MODEL),
        out_shape=jax.ShapeDtypeStruct((grp.m, D_MODEL), F32),
        scratch_shapes=[pltpu.VMEM((grp.tm, D_MODEL), F32)],
        compiler_params=_params(("arbitrary", "arbitrary", "arbitrary")),
        name="ffn_dense",
    )(h2, x, mod, wg, wu, wd)


def _moe_kernel(h2_ref, x_ref, g2_ref, gate_ref, wg_ref, wu_ref, wd_ref, o_ref, acc_s):
    e, f = pl.program_id(2), pl.program_id(3)

    @pl.when((e == 0) & (f == 0))
    def _():
        acc_s[...] = jnp.zeros(acc_s.shape, F32)

    gate = gate_ref[...]
    lane = lax.broadcasted_iota(jnp.int32, gate.shape, 1)
    gcol = jnp.sum(jnp.where(lane == e, gate, 0.0), axis=-1, keepdims=True)
    acc_s[...] += gcol * _swiglu_tile(h2_ref, wg_ref, wu_ref, wd_ref)

    @pl.when((e == pl.num_programs(2) - 1) & (f == pl.num_programs(3) - 1))
    def _():
        o_ref[...] = x_ref[...] + g2_ref[...] * acc_s[...]


def _moe(grp, h2, x, mod, gate, wg, wu, wd, j, tf):
    nf = FF_DIM // tf
    return pl.pallas_call(
        _moe_kernel,
        grid=grp.grid + (N_EXPERTS, nf),
        in_specs=[grp.rows(D_MODEL), grp.rows(D_MODEL), grp.mod(5), grp.rows(LANES),
                  pl.BlockSpec((None, None, D_MODEL, tf), lambda b, i, e, f: (j, e, 0, f)),
                  pl.BlockSpec((None, None, D_MODEL, tf), lambda b, i, e, f: (j, e, 0, f)),
                  pl.BlockSpec((None, None, tf, D_MODEL), lambda b, i, e, f: (j, e, f, 0))],
        out_specs=grp.rows(D_MODEL),
        out_shape=jax.ShapeDtypeStruct((grp.m, D_MODEL), F32),
        scratch_shapes=[pltpu.VMEM((grp.tm, D_MODEL), F32)],
        compiler_params=_params(("arbitrary",) * 4),
        name="ffn_moe",
    )(h2, x, mod, gate, wg, wu, wd)


def _final_kernel(x_ref, g_ref, o_ref):
    o_ref[...] = _rms(x_ref[...], g_ref[...])


def _final_norm(grp, x, g):
    return pl.pallas_call(
        _final_kernel, grid=grp.grid,
        in_specs=[grp.rows(D_MODEL), _const_spec((1, D_MODEL))],
        out_specs=grp.rows(D_MODEL),
        out_shape=jax.ShapeDtypeStruct((grp.m, D_MODEL), F32),
        compiler_params=_params(("arbitrary", "arbitrary")),
        name="final_norm",
    )(x, g)


def _relayout_weights(w):
    bf = lambda a: a.astype(BF16)
    depth = w["w_in"].shape[0]
    wi = w["w_in"]
    o = [0, 256, 640, 896, 928, 1184, 1440, 1696, 1952, 5024]
    kr = wi[:, :, o[3]:o[4]]
    half = MLA_ROPE // 2
    kr_rot = jnp.concatenate([-kr[..., half:], kr[..., :half]], axis=-1)
    zeros = lambda n: jnp.zeros(wi.shape[:2] + (n,), wi.dtype)
    w_in = bf(jnp.concatenate([
        wi[:, :, o[0]:o[1]], wi[:, :, o[2]:o[3]], wi[:, :, o[1]:o[2]], zeros(C_HZ - C_CQ - MLA_Q_LORA),
        wi[:, :, o[4]:o[8]], wi[:, :, o[8]:o[9]], kr, zeros(ROPE_PAD - MLA_ROPE),
        kr_rot, zeros(ROPE_PAD - MLA_ROPE)], axis=-1))
    uq = w["mla_w_uq"].reshape(depth, MLA_Q_LORA, MLA_HEADS, MLA_NOPE + MLA_ROPE)
    nope = uq[..., :MLA_NOPE].reshape(depth, MLA_Q_LORA, MLA_HEADS * MLA_NOPE)
    r1 = uq[..., MLA_NOPE:MLA_NOPE + half]
    r2 = uq[..., MLA_NOPE + half:]
    pad = jnp.zeros(uq.shape[:3] + (ROPE_PAD - MLA_ROPE,), uq.dtype)
    rope = jnp.concatenate([r1, r2, pad], axis=-1).reshape(depth, MLA_Q_LORA, MLA_HEADS * ROPE_PAD)
    rope_rot = jnp.concatenate([-r2, r1, pad], axis=-1).reshape(depth, MLA_Q_LORA, MLA_HEADS * ROPE_PAD)
    wuq = bf(jnp.concatenate([nope, rope, rope_rot], axis=-1))
    eye_h = jnp.eye(MLA_HEADS, dtype=F32)
    uk = w["mla_w_uk"].reshape(depth, MLA_KV_LORA, MLA_HEADS, MLA_NOPE)
    wuk = bf(jnp.einsum("lrhd,hg->lhdgr", uk, eye_h).reshape(
        depth, MLA_HEADS * MLA_NOPE, MLA_HEADS * MLA_KV_LORA))
    wuv = bf(w["mla_w_uv"].reshape(depth, MLA_KV_LORA, MLA_HEADS, MLA_V).transpose(0, 2, 1, 3))
    eye_g = jnp.eye(SSM_GROUPS, dtype=F32)
    b_blk = lambda b: bf(jnp.einsum("lgpc,gh->lgchp", b, eye_g).reshape(depth, SSM_WIDTH, SSM_FLAT))
    c_blk = lambda c: bf(jnp.einsum("lgcp,gh->lgphc", c, eye_g).reshape(depth, SSM_FLAT, SSM_WIDTH))
    n_moe = w["moe_router_w"].shape[0]
    rw = jnp.concatenate([w["moe_router_w"],
                          jnp.zeros((n_moe, D_MODEL, LANES - N_EXPERTS), F32)], axis=-1)
    rb = jnp.concatenate([w["moe_router_b"], jnp.zeros((n_moe, LANES - N_EXPERTS), F32)],
                         axis=-1).reshape(n_moe, 1, LANES)
    return dict(
        w_in=w_in, wuq=wuq, wuk=wuk, wuv=wuv,
        bre=b_blk(w["ssm_b_re"]), bim=b_blk(w["ssm_b_im"]),
        cre=c_blk(w["ssm_c_re"]), cim=c_blk(w["ssm_c_im"]),
        glu_w=bf(w["ssm_glu_w"]), proj_a=bf(w["proj_a"]), proj_b=bf(w["proj_b"]),
        proj_c=bf(w["proj_c"]), w_out=bf(w["w_out"]), rw=rw, rb=rb,
        hg_norm=jnp.tile(w["hgrn_norm_g"], (1, HG_HEADS)))


def _rope_tables(pos):
    half = MLA_ROPE // 2
    freq = ROPE_THETA ** (-jnp.arange(half, dtype=F32) / half)
    ang = pos.astype(F32)[:, None] * freq[None, :]
    cos, sin = jnp.cos(ang), jnp.sin(ang)
    pad = jnp.zeros((pos.shape[0], ROPE_PAD - MLA_ROPE), F32)
    cosk = jnp.concatenate([cos, cos, pad], axis=-1)
    sink = jnp.concatenate([sin, sin, pad], axis=-1)
    return jnp.tile(cosk, (1, MLA_HEADS)), jnp.tile(sink, (1, MLA_HEADS)), cosk, sink


def _state_to_blocks(s):
    eye = jnp.eye(HG_HEADS, dtype=F32)
    return jnp.einsum("bhkv,hg->bhvgk", s, eye).reshape(s.shape[0], HG_W, HG_W)


def _blocks_to_state(st):
    b = st.shape[0]
    return jnp.einsum("bhvhk->bhkv", st.reshape(b, HG_HEADS, HG_K, HG_HEADS, HG_K))


def _trunk(x, mod_all, w, rw, prm, pos0, ssm_re0, ssm_im0, hgrn0, cache_lat, cache_kr, page_table):
    bz, length, _ = x.shape
    m = bz * length
    depth = w["w_in"].shape[0]
    prompt = cache_lat is None
    abr, abi, cr, ci, lbs = prm
    if prompt:
        grp = _Group(bz, length, min(256, length), per_row=False)
        grp_f = _Group(bz, length, min(1024, length), per_row=False)
        tl = min(128, length)
        ch = min(32, length)
        tb = min(256, length)
    else:
        grp = _Group(bz, length, m, per_row=True)
        grp_f = grp
        tl, ch, tb = length, length, length
    pos = pos0 + jnp.arange(length, dtype=jnp.int32)
    tables = _rope_tables(pos)
    if not prompt:
        tables = tuple(jnp.tile(t, (bz, 1)) for t in tables)
    cosq, sinq, cosk, sink = tables
    x = x.reshape(m, D_MODEL)
    lat_l, kr_l, sre_l, sim_l, hg_l = [], [], [], [], []
    row = lambda a: a.reshape(1, -1)
    for l in range(depth):
        mod = mod_all[l]
        mod = mod[:, None, :] if prompt else jnp.repeat(mod, length, axis=0)
        u, hz, gates, ql, qr, lat, latb, kr, krb = _in_proj(
            grp, x, mod, row(w["norm1_g"][l]), rw["w_in"][l], row(w["mla_q_norm_g"][l]), rw["wuq"][l],
            rw["wuk"][l], row(w["mla_kv_norm_g"][l]), cosq, sinq, cosk, sink)
        if prompt:
            h0r = jnp.zeros((bz, SSM_FLAT), F32)
            h0i = h0r
            u_tm = u.reshape(length, bz, SSM_WIDTH)
        else:
            h0r = ssm_re0[l].reshape(bz, SSM_FLAT)
            h0i = ssm_im0[l].reshape(bz, SSM_FLAT)
            u_tm = u.reshape(bz, length, SSM_WIDTH).transpose(1, 0, 2)
        ya, htr, hti = _s5(u_tm, h0r, h0i, row(abr[l]), row(abi[l]), row(cr[l]), row(ci[l]),
                           rw["bre"][l], rw["bim"][l], rw["cre"][l], rw["cim"][l], row(w["ssm_d"][l]),
                           rw["glu_w"][l], row(w["ssm_glu_b"][l]), tl)
        if prompt:
            ya = ya.reshape(length, bz * SSM_WIDTH)
        else:
            ya = ya.transpose(1, 0, 2).reshape(m, SSM_WIDTH)
        if prompt:
            yb = _attn_prompt(bz, length, ql, qr, latb, krb, rw["wuv"][l], grp.tm, min(512, length))
        else:
            yb = _attn_sample(l, page_table, ql, qr, latb, krb, cache_lat, cache_kr, rw["wuv"][l], bz, length)
        st0 = jnp.zeros((bz, HG_W, HG_W), F32) if prompt else _state_to_blocks(hgrn0[l])
        yc, st_t = _hgrn(bz, length, hz, st0, row(lbs[l]), row(rw["hg_norm"][l]), ch, tb)
        j = l // 2
        common = (grp, x, ya, yb, yc, gates, mod, row(w["norm2_g"][l]), rw["proj_a"][l], rw["proj_b"][l],
                  rw["proj_c"][l], rw["w_out"][l])
        if l % 2 == 0:
            x, h2 = _mix(*common)
            x = _ffn(grp_f, h2, x, mod, w["ffn_w_gate"], w["ffn_w_up"], w["ffn_w_down"], j, 256)
        else:
            x, h2, gate = _mix(*common, router=(rw["rw"][j], rw["rb"][j]))
            x = _moe(grp_f, h2, x, mod, gate, w["moe_w_gate"], w["moe_w_up"], w["moe_w_down"], j, 256)
        lat_l.append(lat.reshape(bz, length, MLA_KV_LORA))
        kr_l.append(kr.reshape(bz, length, MLA_ROPE))
        sre_l.append(htr.reshape(bz, SSM_GROUPS, SSM_STATE))
        sim_l.append(hti.reshape(bz, SSM_GROUPS, SSM_STATE))
        hg_l.append(_blocks_to_state(st_t))
    y = _final_norm(grp, x, row(w["final_norm_g"])).reshape(bz, length, D_MODEL)
    return y, jnp.stack(lat_l), jnp.stack(kr_l), jnp.stack(sre_l), jnp.stack(sim_l), jnp.stack(hg_l)


def kernel(x_prompt, x_sample, c_prompt, c_sample, cache_kv_latent, cache_k_rope, state_ssm_re, state_ssm_im,
           state_hgrn, page_table, ada_w, ada_b, norm1_g, norm2_g, w_in, ssm_a_re, ssm_a_im, ssm_log_dt,
           ssm_b_re, ssm_b_im, ssm_c_re, ssm_c_im, ssm_d, ssm_glu_w, ssm_glu_b, mla_q_norm_g, mla_w_uq,
           mla_kv_norm_g, mla_w_uk, mla_w_uv, hgrn_lb_logits, hgrn_norm_g, proj_a, proj_b, proj_c, w_out,
           ffn_w_gate, ffn_w_up, ffn_w_down, moe_router_w, moe_router_b, moe_w_gate, moe_w_up, moe_w_down,
           final_norm_g):
    w = dict(norm1_g=norm1_g, norm2_g=norm2_g, w_in=w_in, ssm_b_re=ssm_b_re, ssm_b_im=ssm_b_im,
             ssm_c_re=ssm_c_re, ssm_c_im=ssm_c_im, ssm_d=ssm_d, ssm_glu_w=ssm_glu_w, ssm_glu_b=ssm_glu_b,
             mla_q_norm_g=mla_q_norm_g, mla_w_uq=mla_w_uq, mla_kv_norm_g=mla_kv_norm_g, mla_w_uk=mla_w_uk,
             mla_w_uv=mla_w_uv, hgrn_norm_g=hgrn_norm_g, proj_a=proj_a, proj_b=proj_b, proj_c=proj_c,
             w_out=w_out, ffn_w_gate=ffn_w_gate, ffn_w_up=ffn_w_up, ffn_w_down=ffn_w_down,
             moe_router_w=moe_router_w, moe_router_b=moe_router_b, moe_w_gate=moe_w_gate,
             moe_w_up=moe_w_up, moe_w_down=moe_w_down, final_norm_g=final_norm_g)
    rw = _relayout_weights(w)
    prm = _prep_params(ssm_a_re, ssm_a_im, ssm_log_dt, hgrn_lb_logits)
    n_p = c_prompt.shape[0]
    mod_all = _ada_mod(jnp.concatenate([c_prompt, c_sample], axis=0), ada_w, ada_b)
    y_p, lat_p, kr_p, sre_p, sim_p, hg_p = _trunk(
        x_prompt, mod_all[:, :n_p], w, rw, prm, 0, None, None, None, None, None, None)
    past_len = page_table.shape[1] * cache_kv_latent.shape[2]
    y_s, lat_s, kr_s, sre_s, sim_s, hg_s = _trunk(
        x_sample, mod_all[:, n_p:], w, rw, prm, past_len, state_ssm_re, state_ssm_im, state_hgrn,
        cache_kv_latent, jnp.swapaxes(cache_k_rope, 2, 3), page_table)
    return (y_p, y_s, lat_p, kr_p, sre_p, sim_p, hg_p, lat_s, kr_s, sre_s, sim_s, hg_s)
```

```python
import functools

import jax
import jax.numpy as jnp
from jax import lax
from jax.experimental import pallas as pl
from jax.experimental.pallas import tpu as pltpu

F32 = jnp.float32
BF16 = jnp.bfloat16

D_MODEL = 1024
SSM_GROUPS = 16
SSM_GROUP_CH = 16
SSM_WIDTH = 256
SSM_STATE = 64
SSM_FLAT = SSM_GROUPS * SSM_STATE
MLA_HEADS = 8
MLA_NOPE = 64
MLA_ROPE = 32
MLA_V = 64
MLA_Q_LORA = 384
MLA_KV_LORA = 256
MLA_SCALE = (MLA_NOPE + MLA_ROPE) ** -0.5
ROPE_THETA = 10000.0
HG_HEADS = 4
HG_K = 64
HG_W = 256
FF_DIM = 2816
N_EXPERTS = 8
EPS = 1e-6

LANES = 128
SUBLANES = 8
VMEM_LIMIT = 52 * 1024 * 1024
MOE_TM = 256
MOE_T = 1024

C_U, C_CKV, C_CQ, C_HZ, C_GATES, C_KR, C_KR_ROT, C_END = 0, 256, 512, 1024, 2048, 5120, 5248, 5376
ROPE_PAD = LANES


def _mm(a, b):
    return jnp.dot(a, b, preferred_element_type=F32)


def _mm_nt(a, b):
    return lax.dot_general(a, b, (((1,), (1,)), ((), ())), preferred_element_type=F32)


def _mm_tn(a, b):
    return lax.dot_general(a, b, (((0,), (0,)), ((), ())), preferred_element_type=F32)


def _mm_f32(a, b):
    return jnp.dot(a, b, preferred_element_type=F32, precision=lax.Precision.HIGHEST)


def _mm_split(a, b):
    ah, bh = a.astype(BF16), b.astype(BF16)
    al = (a - ah.astype(F32)).astype(BF16)
    bl = (b - bh.astype(F32)).astype(BF16)
    return _mm(ah, bh) + (_mm(ah, bl) + _mm(al, bh))


def _rms(x, g):
    return x * lax.rsqrt(jnp.mean(x * x, axis=-1, keepdims=True) + EPS) * g


def _params(sem):
    return pltpu.CompilerParams(dimension_semantics=sem, vmem_limit_bytes=VMEM_LIMIT)


def _const_spec(shape):
    nd = len(shape)
    return pl.BlockSpec(shape, lambda *_: (0,) * nd)


class _Group:
    def __init__(self, bz, length, tm, per_row):
        self.bz, self.length, self.tm, self.per_row = bz, length, tm, per_row
        self.m = bz * length
        if per_row:
            self.nb = self.m // tm
            self.grid = (1, self.nb)
        else:
            self.nb = length // tm
            self.grid = (bz, self.nb)

    def rows(self, width):
        nb = self.nb
        return pl.BlockSpec((self.tm, width), lambda b, i, *_: (b * nb + i, 0))

    def mod(self, col):
        if self.per_row:
            return pl.BlockSpec((self.tm, D_MODEL), lambda b, i, *_: (i, col))
        return pl.BlockSpec((None, 1, D_MODEL), lambda b, i, *_: (b, 0, col))

    def pos(self, width):
        return pl.BlockSpec((self.tm, width), lambda b, i, *_: (i, 0))

    def tmajor(self, width):
        if self.per_row:
            return self.rows(width)
        return pl.BlockSpec((self.tm, width), lambda b, i, *_: (i, b))

    def tmajor_shape(self, width):
        return (self.m, width) if self.per_row else (self.length, self.bz * width)


def _param_kernel(ar_ref, ai_ref, ldt_ref, lbl_ref, abr_ref, abi_ref, cr_ref, ci_ref, lbs_ref):
    ar, ai = ar_ref[...], ai_ref[...]
    dt = jnp.exp(ldt_ref[...])
    mag = jnp.exp(dt * ar)
    abr, abi = mag * jnp.cos(dt * ai), mag * jnp.sin(dt * ai)
    den = ar * ar + ai * ai
    abr_ref[...] = abr
    abi_ref[...] = abi
    cr_ref[...] = ((abr - 1.0) * ar + abi * ai) / den
    ci_ref[...] = (abi * ar - (abr - 1.0) * ai) / den
    x = lbl_ref[...]
    e = jnp.exp(x - jnp.max(x, axis=0, keepdims=True))
    p = e / jnp.sum(e, axis=0, keepdims=True)
    rows, acc = [], p[0:1]
    for l in range(x.shape[0]):
        if l:
            acc = acc + p[l:l + 1]
        rows.append(acc - p[0:1])
    lbs_ref[...] = jnp.concatenate(rows, axis=0)


def _prep_params(a_re, a_im, log_dt, lb_logits):
    depth = a_re.shape[0]
    ar = a_re.reshape(depth, SSM_FLAT)
    ai = a_im.reshape(depth, SSM_FLAT)
    ldt = jnp.broadcast_to(log_dt[:, :, None], (depth, SSM_GROUPS, SSM_STATE)).reshape(depth, SSM_FLAT)
    flat = jax.ShapeDtypeStruct((depth, SSM_FLAT), F32)
    return pl.pallas_call(
        _param_kernel,
        out_shape=(flat, flat, flat, flat, jax.ShapeDtypeStruct((depth, HG_W), F32)),
        name="param_prep",
    )(ar, ai, ldt, lb_logits)


def _ada_kernel(c_ref, w_ref, b_ref, o_ref):
    c = c_ref[...]
    s = (c * jax.nn.sigmoid(c)).astype(BF16)
    o_ref[...] = _mm(s, w_ref[...].astype(BF16)) + b_ref[...]


def _ada_mod(c_all, ada_w, ada_b):
    depth, d, n = ada_w.shape
    rows = c_all.shape[0]
    tn = 1536
    return pl.pallas_call(
        _ada_kernel,
        grid=(depth, n // tn),
        in_specs=[pl.BlockSpec((rows, d), lambda l, j: (0, 0)),
                  pl.BlockSpec((None, d, tn), lambda l, j: (l, 0, j)),
                  pl.BlockSpec((None, 1, tn), lambda l, j: (l, 0, j))],
        out_specs=pl.BlockSpec((None, rows, tn), lambda l, j: (l, 0, j)),
        out_shape=jax.ShapeDtypeStruct((depth, rows, n), F32),
        compiler_params=_params(("arbitrary", "arbitrary")),
        name="ada_mod",
    )(c_all, ada_w, ada_b.reshape(depth, 1, n))


def _in_kernel(x_ref, sh_ref, sc_ref, g_ref, w_ref, qg_ref, wuq_ref, wuk_ref, kvg_ref,
               cq_ref, sq_ref, ck_ref, sk_ref,
               u_ref, hz_ref, gates_ref, ql_ref, qr_ref, lat_ref, latb_ref, kr_ref, krb_ref):
    x = x_ref[...]
    h = (_rms(x, g_ref[...]) * (1.0 + sc_ref[...]) + sh_ref[...]).astype(BF16)

    def seg(a, b):
        return _mm(h, w_ref[:, a:b])

    u_ref[...] = seg(C_U, C_CKV)
    hz_ref[...] = seg(C_HZ, C_GATES)
    gates_ref[...] = seg(C_GATES, C_KR)
    cqn = _rms(seg(C_CQ, C_CQ + MLA_Q_LORA), qg_ref[...]).astype(BF16)
    q = _mm(cqn, wuq_ref[...])
    n_nope = MLA_HEADS * MLA_NOPE
    n_rope = MLA_HEADS * ROPE_PAD
    ql = _mm(q[:, :n_nope].astype(BF16), wuk_ref[...]).astype(BF16)
    qr = (q[:, n_nope:n_nope + n_rope] * cq_ref[...] + q[:, n_nope + n_rope:] * sq_ref[...]).astype(BF16)
    for hd in range(MLA_HEADS):
        ql_ref[hd] = ql[:, hd * MLA_KV_LORA:(hd + 1) * MLA_KV_LORA]
        qr_ref[hd] = qr[:, hd * ROPE_PAD:(hd + 1) * ROPE_PAD]
    lat = _rms(seg(C_CKV, C_CQ), kvg_ref[...])
    lat_ref[...] = lat
    latb_ref[...] = lat.astype(BF16)
    kr = seg(C_KR, C_KR_ROT) * ck_ref[...] + seg(C_KR_ROT, C_END) * sk_ref[...]
    kr_ref[...] = kr[:, :MLA_ROPE]
    krb_ref[...] = kr.astype(BF16)


def _in_proj(grp, x, mod, g1, w_in, qg, wuq, wuk, kvg, cosq, sinq, cosk, sink):
    m, tm = grp.m, grp.tm
    n_rope = MLA_HEADS * ROPE_PAD
    nb = grp.nb

    def heads(width):
        return pl.BlockSpec((MLA_HEADS, tm, width), lambda b, i: (0, b * nb + i, 0))

    outs = [
        (grp.tmajor_shape(SSM_WIDTH), F32, grp.tmajor(SSM_WIDTH)),
        ((m, 4 * HG_W), F32, grp.rows(4 * HG_W)),
        ((m, 3 * D_MODEL), F32, grp.rows(3 * D_MODEL)),
        ((MLA_HEADS, m, MLA_KV_LORA), BF16, heads(MLA_KV_LORA)),
        ((MLA_HEADS, m, ROPE_PAD), BF16, heads(ROPE_PAD)),
        ((m, MLA_KV_LORA), F32, grp.rows(MLA_KV_LORA)),
        ((m, MLA_KV_LORA), BF16, grp.rows(MLA_KV_LORA)),
        ((m, MLA_ROPE), F32, grp.rows(MLA_ROPE)),
        ((m, ROPE_PAD), BF16, grp.rows(ROPE_PAD)),
    ]
    return pl.pallas_call(
        _in_kernel,
        grid=grp.grid,
        in_specs=[grp.rows(D_MODEL), grp.mod(0), grp.mod(1), _const_spec((1, D_MODEL)),
                  _const_spec(w_in.shape), _const_spec((1, MLA_Q_LORA)), _const_spec(wuq.shape),
                  _const_spec(wuk.shape), _const_spec((1, MLA_KV_LORA)),
                  grp.pos(n_rope), grp.pos(n_rope), grp.pos(ROPE_PAD), grp.pos(ROPE_PAD)],
        out_specs=[o[2] for o in outs],
        out_shape=[jax.ShapeDtypeStruct(o[0], o[1]) for o in outs],
        compiler_params=_params(("arbitrary", "arbitrary")),
        name="in_proj",
    )(x, mod, mod, g1, w_in, qg, wuq, wuk, kvg, cosq, sinq, cosk, sink)


def _s5_kernel(u_ref, h0r_ref, h0i_ref, abr_ref, abi_ref, cr_ref, ci_ref, bre_ref, bim_ref,
               cre_ref, cim_ref, d_ref, gw_ref, gb_ref,
               y_ref, htr_ref, hti_ref, hr_s, hi_s, xr_s, xi_s, *, tl, bz):
    i = pl.program_id(0)

    @pl.when(i == 0)
    def _():
        hr_s[...] = h0r_ref[...]
        hi_s[...] = h0i_ref[...]

    u = u_ref[...].reshape(tl * bz, SSM_WIDTH)
    ub = u.astype(BF16)
    bur, bui = _mm(ub, bre_ref[...]), _mm(ub, bim_ref[...])
    cr, ci = cr_ref[...], ci_ref[...]
    xr_s[...] = cr * bur - ci * bui
    xi_s[...] = cr * bui + ci * bur
    abr, abi = abr_ref[...], abi_ref[...]

    def step(t, carry):
        hr, hi = carry
        r0 = pl.multiple_of(t * bz, bz)
        nr = abr * hr - abi * hi + xr_s[pl.ds(r0, bz), :]
        ni = abr * hi + abi * hr + xi_s[pl.ds(r0, bz), :]
        xr_s[pl.ds(r0, bz), :] = nr
        xi_s[pl.ds(r0, bz), :] = ni
        return nr, ni

    hr, hi = lax.fori_loop(0, tl, step, (hr_s[...], hi_s[...]))
    hr_s[...] = hr
    hi_s[...] = hi
    htr_ref[...] = hr
    hti_ref[...] = hi
    y = (_mm(xr_s[...].astype(BF16), cre_ref[...]) - _mm(xi_s[...].astype(BF16), cim_ref[...])
         + d_ref[...] * u)
    y = jax.nn.gelu(y)
    y = y * jax.nn.sigmoid(_mm(y.astype(BF16), gw_ref[...]) + gb_ref[...])
    y_ref[...] = y.reshape(tl, bz, SSM_WIDTH)


def _s5(u_tm, h0r, h0i, abr, abi, cr, ci, bre, bim, cre, cim, d, gw, gb, tl):
    length, bz, _ = u_tm.shape
    row = _const_spec((1, SSM_FLAT))
    st = _const_spec((bz, SSM_FLAT))
    blk = pl.BlockSpec((tl, bz, SSM_WIDTH), lambda i: (i, 0, 0))
    return pl.pallas_call(
        functools.partial(_s5_kernel, tl=tl, bz=bz),
        grid=(length // tl,),
        in_specs=[blk, st, st, row, row, row, row, _const_spec(bre.shape), _const_spec(bim.shape),
                  _const_spec(cre.shape), _const_spec(cim.shape), _const_spec((1, SSM_WIDTH)),
                  _const_spec(gw.shape), _const_spec((1, SSM_WIDTH))],
        out_specs=[blk, st, st],
        out_shape=[jax.ShapeDtypeStruct(u_tm.shape, F32), jax.ShapeDtypeStruct((bz, SSM_FLAT), F32),
                   jax.ShapeDtypeStruct((bz, SSM_FLAT), F32)],
        scratch_shapes=[pltpu.VMEM((bz, SSM_FLAT), F32), pltpu.VMEM((bz, SSM_FLAT), F32),
                        pltpu.VMEM((tl * bz, SSM_FLAT), F32), pltpu.VMEM((tl * bz, SSM_FLAT), F32)],
        compiler_params=_params(("arbitrary",)),
        name="s5_scan",
    )(u_tm, h0r, h0i, abr, abi, cr, ci, bre, bim, cre, cim, d, gw, gb)


def _softmax_step(s, keys, m_ref, l_ref, acc_ref):
    m_old = m_ref[...]
    m_new = jnp.maximum(m_old, jnp.max(s, axis=-1, keepdims=True))
    alpha = jnp.exp(m_old - m_new)
    p = jnp.exp(s - jnp.tile(m_new, (1, s.shape[-1] // LANES)))
    l_ref[...] = alpha * l_ref[...] + jnp.sum(p, axis=-1, keepdims=True)
    acc_ref[...] = (acc_ref[...] * jnp.tile(alpha, (1, acc_ref.shape[-1] // LANES))
                    + _mm(p.astype(BF16), keys))
    m_ref[...] = m_new


def _attn_prompt_kernel(ql_ref, qr_ref, lat_ref, kr_ref, wuv_ref, y_ref, m_s, l_s, acc_s, *, tq, tk):
    i = pl.program_id(1)
    rows = MLA_HEADS * tq
    m_s[...] = jnp.full(m_s.shape, -jnp.inf, F32)
    l_s[...] = jnp.zeros(l_s.shape, F32)
    acc_s[...] = jnp.zeros(acc_s.shape, F32)

    def block(j, masked):
        k0 = pl.multiple_of(j * tk, tk)
        kl = lat_ref[pl.ds(k0, tk), :]
        kr = kr_ref[pl.ds(k0, tk), :]
        s = (_mm_nt(ql_ref[...].reshape(rows, MLA_KV_LORA), kl)
             + _mm_nt(qr_ref[...].reshape(rows, ROPE_PAD), kr)) * MLA_SCALE
        if masked:
            q_pos = i * tq + lax.broadcasted_iota(jnp.int32, (rows, tk), 0) % tq
            k_pos = k0 + lax.broadcasted_iota(jnp.int32, (rows, tk), 1)
            s = jnp.where(k_pos <= q_pos, s, -jnp.inf)
        _softmax_step(s, kl, m_s, l_s, acc_s)

    n_full = (i * tq) // tk

    def full_block(j, carry):
        block(j, False)
        return carry

    lax.fori_loop(0, n_full, full_block, 0)
    block(n_full, True)
    ctx = acc_s[...] / jnp.tile(l_s[...], (1, MLA_KV_LORA // LANES))
    ys = [_mm(ctx[h * tq:(h + 1) * tq, :].astype(BF16), wuv_ref[h]) for h in range(MLA_HEADS)]
    y_ref[...] = jnp.concatenate(ys, axis=-1).astype(BF16)


def _attn_prompt(grp_bz, length, ql, qr, latb, krb, wuv, tq, tk):
    nb = length // tq
    rows = MLA_HEADS * tq
    heads = lambda w: pl.BlockSpec((MLA_HEADS, tq, w), lambda b, i: (0, b * nb + i, 0))
    return pl.pallas_call(
        functools.partial(_attn_prompt_kernel, tq=tq, tk=tk),
        grid=(grp_bz, nb),
        in_specs=[heads(MLA_KV_LORA), heads(ROPE_PAD),
                  pl.BlockSpec((None, length, MLA_KV_LORA), lambda b, i: (b, 0, 0)),
                  pl.BlockSpec((None, length, ROPE_PAD), lambda b, i: (b, 0, 0)),
                  _const_spec(wuv.shape)],
        out_specs=pl.BlockSpec((tq, MLA_HEADS * MLA_V), lambda b, i: (b * nb + i, 0)),
        out_shape=jax.ShapeDtypeStruct((grp_bz * length, MLA_HEADS * MLA_V), BF16),
        scratch_shapes=[pltpu.VMEM((rows, LANES), F32), pltpu.VMEM((rows, LANES), F32),
                        pltpu.VMEM((rows, MLA_KV_LORA), F32)],
        compiler_params=_params(("arbitrary", "arbitrary")),
        name="attn_prompt",
    )(ql, qr, latb.reshape(grp_bz, length, MLA_KV_LORA), krb.reshape(grp_bz, length, ROPE_PAD), wuv)


def _attn_sample_kernel(pt_ref, ql_ref, qr_ref, nl_ref, nk_ref, wuv_ref, *rest, pg, ns, lq):
    lat_refs, krt_refs = rest[:pg], rest[pg:2 * pg]
    y_ref, m_s, l_s, acc_s = rest[2 * pg:]
    s_idx = pl.program_id(1)
    n_steps = pl.num_programs(1)

    @pl.when(s_idx == 0)
    def _():
        m_s[...] = jnp.full(m_s.shape, -jnp.inf, F32)
        l_s[...] = jnp.zeros(l_s.shape, F32)
        acc_s[...] = jnp.zeros(acc_s.shape, F32)

    q_l, q_r = ql_ref[...], qr_ref[...]
    per = pg // ns
    for st in range(ns):
        keys = jnp.concatenate([r[...].astype(BF16) for r in lat_refs[st * per:(st + 1) * per]], axis=0)
        krt = jnp.concatenate([r[...].astype(BF16) for r in krt_refs[st * per:(st + 1) * per]], axis=1)
        s = (_mm_nt(q_l, keys) + _mm(q_r, krt)) * MLA_SCALE
        _softmax_step(s, keys, m_s.at[st], l_s.at[st], acc_s.at[st])

    @pl.when(s_idx == n_steps - 1)
    def _():
        rows = MLA_HEADS * lq
        nl, nk = nl_ref[...], nk_ref[...]
        s = (_mm_nt(q_l, nl) + _mm_nt(q_r, nk)) * MLA_SCALE
        q_t = lax.broadcasted_iota(jnp.int32, (rows, lq), 0) % lq
        k_t = lax.broadcasted_iota(jnp.int32, (rows, lq), 1)
        s = jnp.where(k_t <= q_t, s, -jnp.inf)
        m_old = m_s[0]
        m_new = jnp.maximum(m_old, jnp.max(s, axis=-1, keepdims=True))
        for st in range(1, ns):
            m_new = jnp.maximum(m_new, m_s[st])
        p = jnp.exp(s - m_new[:, :1])
        l_tot = jnp.sum(p, axis=-1, keepdims=True)
        acc = _mm(p.astype(BF16), nl)
        for st in range(ns):
            a = jnp.exp(m_s[st] - m_new)
            l_tot = l_tot + a * l_s[st]
            acc = acc + jnp.tile(a, (1, MLA_KV_LORA // LANES)) * acc_s[st]
        ctx = acc / jnp.tile(l_tot, (1, MLA_KV_LORA // LANES))
        ys = [_mm(ctx[h * lq:(h + 1) * lq, :].astype(BF16), wuv_ref[h]) for h in range(MLA_HEADS)]
        y_ref[...] = jnp.concatenate(ys, axis=-1).astype(BF16)


def _attn_sample(layer, page_table, ql, qr, latb, krb, cache_lat, cache_krt, wuv, bz, lq):
    n_pages = page_table.shape[1]
    pg = min(16, n_pages)
    ns = 2 if pg % 2 == 0 else 1
    page = cache_lat.shape[2]
    rows = MLA_HEADS * lq
    qlh = ql.reshape(MLA_HEADS, bz, lq, MLA_KV_LORA).transpose(1, 0, 2, 3).reshape(bz, rows, MLA_KV_LORA)
    qrh = qr[:, :, :MLA_ROPE].reshape(MLA_HEADS, bz, lq, MLA_ROPE).transpose(1, 0, 2, 3).reshape(
        bz, rows, MLA_ROPE)

    def seq(n, w):
        return pl.BlockSpec((None, n, w), lambda b, s, pt: (b, 0, 0))

    in_specs = ([seq(rows, MLA_KV_LORA), seq(rows, MLA_ROPE), seq(lq, MLA_KV_LORA), seq(lq, MLA_ROPE),
                 pl.BlockSpec(wuv.shape, lambda b, s, pt: (0, 0, 0))]
                + [pl.BlockSpec((None, None, page, MLA_KV_LORA),
                                lambda b, s, pt, i=i: (layer, pt[b, s * pg + i], 0, 0)) for i in range(pg)]
                + [pl.BlockSpec((None, None, MLA_ROPE, page),
                                lambda b, s, pt, i=i: (layer, pt[b, s * pg + i], 0, 0)) for i in range(pg)])
    y = pl.pallas_call(
        functools.partial(_attn_sample_kernel, pg=pg, ns=ns, lq=lq),
        grid_spec=pltpu.PrefetchScalarGridSpec(
            num_scalar_prefetch=1,
            grid=(bz, n_pages // pg),
            in_specs=in_specs,
            out_specs=pl.BlockSpec((None, lq, MLA_HEADS * MLA_V), lambda b, s, pt: (b, 0, 0)),
            scratch_shapes=[pltpu.VMEM((ns, rows, LANES), F32), pltpu.VMEM((ns, rows, LANES), F32),
                            pltpu.VMEM((ns, rows, MLA_KV_LORA), F32)]),
        out_shape=jax.ShapeDtypeStruct((bz, lq, MLA_HEADS * MLA_V), BF16),
        compiler_params=_params(("arbitrary", "arbitrary")),
        name="attn_sample",
    )(page_table, qlh, qrh, latb.reshape(bz, lq, MLA_KV_LORA),
      krb[:, :MLA_ROPE].reshape(bz, lq, MLA_ROPE), wuv, *([cache_lat] * pg), *([cache_krt] * pg))
    return y.reshape(bz * lq, MLA_HEADS * MLA_V)


def _hgrn_kernel(hz_ref, st0_ref, lb_ref, ng_ref, y_ref, stt_ref, st_s, w_s, *, ch, tb):
    j = pl.program_id(1)

    @pl.when(j == 0)
    def _():
        st_s[...] = st0_ref[...]

    lb = lb_ref[...]
    log_lb, log1m_lb, one_m_lb = jnp.log(lb), jnp.log1p(-lb), 1.0 - lb
    tri = (lax.broadcasted_iota(jnp.int32, (ch, ch), 0)
           >= lax.broadcasted_iota(jnp.int32, (ch, ch), 1)).astype(F32)
    same_head = (lax.broadcasted_iota(jnp.int32, (HG_W, HG_W), 0) // HG_K
                 == lax.broadcasted_iota(jnp.int32, (HG_W, HG_W), 1) // HG_K)
    head_ones = same_head.astype(F32)
    head_ones_b = same_head.astype(BF16)
    ng = ng_ref[...]

    def chunk(c, carry):
        r0 = pl.multiple_of(c * ch, ch)
        q = hz_ref[pl.ds(r0, ch), 0:HG_W]
        fp = hz_ref[pl.ds(r0, ch), HG_W:2 * HG_W]
        v = hz_ref[pl.ds(r0, ch), 2 * HG_W:3 * HG_W]
        g = hz_ref[pl.ds(r0, ch), 3 * HG_W:4 * HG_W]
        logf = jnp.logaddexp(log_lb, log1m_lb + jax.nn.log_sigmoid(fp))
        k = one_m_lb * jax.nn.sigmoid(-fp)
        b = _mm_f32(tri, logf)
        st = st_s[...]
        o = _mm_nt((q * jnp.exp(b)).astype(BF16), st.astype(BF16))
        starts, r = [], 0
        for s in range(ch):
            t0 = (s // SUBLANES) * SUBLANES
            t_idx = t0 + lax.broadcasted_iota(jnp.int32, (ch - t0, HG_W), 0)
            e = jnp.exp(jnp.where(t_idx >= s, b[t0:] - b[s:s + 1, :], -jnp.inf))
            w_s[r:r + ch - t0, :] = q[t0:] * k[s:s + 1, :] * e
            starts.append(r)
            r += ch - t0
        att = _mm(w_s[...].astype(BF16), head_ones_b)
        tiles = [o[t0:t0 + SUBLANES] for t0 in range(0, ch, SUBLANES)]
        for s in range(ch):
            for kt in range(s // SUBLANES, ch // SUBLANES):
                a0 = starts[s] + (kt - s // SUBLANES) * SUBLANES
                tiles[kt] = tiles[kt] + att[a0:a0 + SUBLANES, :] * v[s:s + 1, :]
        o = jnp.concatenate(tiles, axis=0)
        bl = b[ch - 1:ch, :]
        kd = k * jnp.exp(bl - b)
        upd = _mm_tn(v.astype(BF16), kd.astype(BF16))
        st_s[...] = st * jnp.exp(bl) + jnp.where(same_head, upd, 0.0)
        ms = _mm_f32(o * o, head_ones) * (1.0 / HG_K)
        on = o * lax.rsqrt(ms + EPS) * ng
        y_ref[pl.ds(r0, ch), :] = on * (g * jax.nn.sigmoid(g))
        return carry

    lax.fori_loop(0, tb // ch, chunk, 0)

    @pl.when(j == pl.num_programs(1) - 1)
    def _():
        stt_ref[...] = st_s[...]


def _hgrn(bz, length, hz, st0, lb, ng, ch, tb):
    nb = length // tb
    st_spec = pl.BlockSpec((None, HG_W, HG_W), lambda b, j: (b, 0, 0))
    return pl.pallas_call(
        functools.partial(_hgrn_kernel, ch=ch, tb=tb),
        grid=(bz, nb),
        in_specs=[pl.BlockSpec((tb, 4 * HG_W), lambda b, j: (b * nb + j, 0)), st_spec,
                  _const_spec((1, HG_W)), _const_spec((1, HG_W))],
        out_specs=[pl.BlockSpec((tb, HG_W), lambda b, j: (b * nb + j, 0)), st_spec],
        out_shape=[jax.ShapeDtypeStruct((bz * length, HG_W), F32),
                   jax.ShapeDtypeStruct((bz, HG_W, HG_W), F32)],
        scratch_shapes=[pltpu.VMEM((HG_W, HG_W), F32),
                        pltpu.VMEM((sum(ch - s // SUBLANES * SUBLANES for s in range(ch)), HG_W), F32)],
        compiler_params=_params(("arbitrary", "arbitrary")),
        name="hgrn2",
    )(hz, st0, lb, ng)


def _mix_kernel(*refs, moe):
    if moe:
        (x_ref, ya_ref, yb_ref, yc_ref, gates_ref, g1_ref, sh2_ref, sc2_ref, n2_ref,
         pa_ref, pb_ref, pc_ref, wo_ref, rw_ref, rb_ref,
         xo_ref, h2_ref, gate_ref, route_ref, wts_ref, cb_ref, cnt_ref, run_s) = refs
    else:
        (x_ref, ya_ref, yb_ref, yc_ref, gates_ref, g1_ref, sh2_ref, sc2_ref, n2_ref,
         pa_ref, pb_ref, pc_ref, wo_ref, xo_ref, h2_ref) = refs
    d = D_MODEL
    mix = (jax.nn.sigmoid(gates_ref[:, 0:d]) * _mm(ya_ref[...].astype(BF16), pa_ref[...])
           + jax.nn.sigmoid(gates_ref[:, d:2 * d]) * _mm(yb_ref[...], pb_ref[...])
           + jax.nn.sigmoid(gates_ref[:, 2 * d:3 * d]) * _mm(yc_ref[...].astype(BF16), pc_ref[...]))
    xo = x_ref[...] + g1_ref[...] * _mm(mix.astype(BF16), wo_ref[...])
    xo_ref[...] = xo
    h2 = _rms(xo, n2_ref[...]) * (1.0 + sc2_ref[...]) + sh2_ref[...]
    h2_ref[...] = h2.astype(BF16)
    if moe:
        logits = _mm_split(h2, rw_ref[...]) + rb_ref[...]
        lane = lax.broadcasted_iota(jnp.int32, logits.shape, 1)
        lg = jnp.where(lane < N_EXPERTS, logits, -jnp.inf)
        m1 = jnp.max(lg, axis=-1, keepdims=True)
        i1 = jnp.min(jnp.where(lg == m1, lane, LANES), axis=-1, keepdims=True)
        lg2 = jnp.where(lane == i1, -jnp.inf, lg)
        m2 = jnp.max(lg2, axis=-1, keepdims=True)
        i2 = jnp.min(jnp.where(lg2 == m2, lane, LANES), axis=-1, keepdims=True)
        e2 = jnp.exp(m2 - m1)
        den = 1.0 + e2
        w1, w2 = 1.0 / den, e2 / den
        gate_ref[...] = jnp.where(lane == i1, w1, 0.0) + jnp.where(lane == i2, w2, 0.0)
        @pl.when((pl.program_id(0) == 0) & (pl.program_id(1) == 0))
        def _():
            run_s[...] = jnp.zeros(run_s.shape, F32)

        tm = logits.shape[0]
        sel = jnp.where(lane == i1, 1.0, 0.0) + jnp.where(lane == i2, 1.0, 0.0)
        before = (lax.broadcasted_iota(jnp.int32, (tm, tm), 0) > lax.broadcasted_iota(jnp.int32, (tm, tm), 1))
        rank = _mm(jnp.where(before, 1.0, 0.0).astype(BF16), sel.astype(BF16)) + run_s[...]
        r1 = jnp.sum(jnp.where(lane == i1, rank, 0.0), axis=-1, keepdims=True).astype(jnp.int32)
        r2 = jnp.sum(jnp.where(lane == i2, rank, 0.0), axis=-1, keepdims=True).astype(jnp.int32)
        route_ref[...] = jnp.where(lane == 0, i1, jnp.where(lane == 1, i2, jnp.where(
            lane == 2, r1, jnp.where(lane == 3, r2, 0))))
        wts_ref[...] = jnp.where(lane == 0, w1, jnp.where(lane == 1, w2, 0.0))
        cb_ref[...] = run_s[...]
        total = run_s[...] + jnp.sum(sel, axis=0, keepdims=True)
        run_s[...] = total
        cnt_ref[...] = total


def _mix(grp, x, ya, yb, yc, gates, mod, n2, pa, pb, pc, wo, router=None):
    m = grp.m
    moe = router is not None
    in_specs = [grp.rows(D_MODEL), grp.tmajor(SSM_WIDTH), grp.rows(MLA_HEADS * MLA_V), grp.rows(HG_W),
                grp.rows(3 * D_MODEL), grp.mod(2), grp.mod(3), grp.mod(4), _const_spec((1, D_MODEL)),
                _const_spec(pa.shape), _const_spec(pb.shape), _const_spec(pc.shape), _const_spec(wo.shape)]
    args = [x, ya, yb, yc, gates, mod, mod, mod, n2, pa, pb, pc, wo]
    out_specs = [grp.rows(D_MODEL), grp.rows(D_MODEL)]
    out_shape = [jax.ShapeDtypeStruct((m, D_MODEL), F32), jax.ShapeDtypeStruct((m, D_MODEL), BF16)]
    if moe:
        in_specs += [_const_spec(router[0].shape), _const_spec(router[1].shape)]
        args += list(router)
        nb, n_tiles = grp.nb, m // grp.tm
        out_specs += [grp.rows(LANES), grp.rows(LANES), grp.rows(LANES),
                      pl.BlockSpec((None, 1, LANES), lambda b, i: (b * nb + i, 0, 0)), _const_spec((1, LANES))]
        out_shape += [jax.ShapeDtypeStruct((m, LANES), F32), jax.ShapeDtypeStruct((m, LANES), jnp.int32),
                      jax.ShapeDtypeStruct((m, LANES), F32), jax.ShapeDtypeStruct((n_tiles, 1, LANES), F32),
                      jax.ShapeDtypeStruct((1, LANES), F32)]
    return pl.pallas_call(
        functools.partial(_mix_kernel, moe=moe),
        grid=grp.grid, in_specs=in_specs, out_specs=out_specs, out_shape=out_shape,
        scratch_shapes=[pltpu.VMEM((1, LANES), F32)] if moe else [],
        compiler_params=_params(("arbitrary", "arbitrary")),
        name="mix_moe" if moe else "mix",
    )(*args)


def _swiglu_tile(h2_ref, wg_ref, wu_ref, wd_ref):
    h = h2_ref[...]
    g = _mm(h, wg_ref[...].astype(BF16))
    u = _mm(h, wu_ref[...].astype(BF16))
    a = (g * jax.nn.sigmoid(g) * u).astype(BF16)
    return _mm(a, wd_ref[...].astype(BF16))


def _ffn_kernel(h2_ref, x_ref, g2_ref, wg_ref, wu_ref, wd_ref, o_ref, acc_s):
    f = pl.program_id(2)

    @pl.when(f == 0)
    def _():
        acc_s[...] = jnp.zeros(acc_s.shape, F32)

    acc_s[...] += _swiglu_tile(h2_ref, wg_ref, wu_ref, wd_ref)

    @pl.when(f == pl.num_programs(2) - 1)
    def _():
        o_ref[...] = x_ref[...] + g2_ref[...] * acc_s[...]


def _ffn(grp, h2, x, mod, wg, wu, wd, j, tf):
    nf = FF_DIM // tf
    return pl.pallas_call(
        _ffn_kernel,
        grid=grp.grid + (nf,),
        in_specs=[grp.rows(D_MODEL), grp.rows(D_MODEL), grp.mod(5),
                  pl.BlockSpec((None, D_MODEL, tf), lambda b, i, f: (j, 0, f)),
                  pl.BlockSpec((None, D_MODEL, tf), lambda b, i, f: (j, 0, f)),
                  pl.BlockSpec((None, tf, D_MODEL), lambda b, i, f: (j, f, 0))],
        out_specs=grp.rows(D_MODEL),
        out_shape=jax.ShapeDtypeStruct((grp.m, D_MODEL), F32),
        scratch_shapes=[pltpu.VMEM((grp.tm, D_MODEL), F32)],
        compiler_params=_params(("arbitrary", "arbitrary", "arbitrary")),
        name="ffn_dense",
    )(h2, x, mod, wg, wu, wd)


def _moe_kernel(h2_ref, x_ref, g2_ref, gate_ref, wg_ref, wu_ref, wd_ref, o_ref, acc_s):
    e, f = pl.program_id(2), pl.program_id(3)

    @pl.when((e == 0) & (f == 0))
    def _():
        acc_s[...] = jnp.zeros(acc_s.shape, F32)

    gate = gate_ref[...]
    lane = lax.broadcasted_iota(jnp.int32, gate.shape, 1)
    gcol = jnp.sum(jnp.where(lane == e, gate, 0.0), axis=-1, keepdims=True)
    acc_s[...] += gcol * _swiglu_tile(h2_ref, wg_ref, wu_ref, wd_ref)

    @pl.when((e == pl.num_programs(2) - 1) & (f == pl.num_programs(3) - 1))
    def _():
        o_ref[...] = x_ref[...] + g2_ref[...] * acc_s[...]


def _moe(grp, h2, x, mod, gate, wg, wu, wd, j, tf):
    nf = FF_DIM // tf
    return pl.pallas_call(
        _moe_kernel,
        grid=grp.grid + (N_EXPERTS, nf),
        in_specs=[grp.rows(D_MODEL), grp.rows(D_MODEL), grp.mod(5), grp.rows(LANES),
                  pl.BlockSpec((None, None, D_MODEL, tf), lambda b, i, e, f: (j, e, 0, f)),
                  pl.BlockSpec((None, None, D_MODEL, tf), lambda b, i, e, f: (j, e, 0, f)),
                  pl.BlockSpec((None, None, tf, D_MODEL), lambda b, i, e, f: (j, e, f, 0))],
        out_specs=grp.rows(D_MODEL),
        out_shape=jax.ShapeDtypeStruct((grp.m, D_MODEL), F32),
        scratch_shapes=[pltpu.VMEM((grp.tm, D_MODEL), F32)],
        compiler_params=_params(("arbitrary",) * 4),
        name="ffn_moe",
    )(h2, x, mod, gate, wg, wu, wd)


def _route_plan(route, cb, cnt, n_tok):
    cnt8 = cnt[0, :N_EXPERTS].astype(jnp.int32)
    cb8 = cb[:, 0, :N_EXPERTS].astype(jnp.int32)
    tile_of = jnp.arange(n_tok, dtype=jnp.int32) // MOE_TM
    p1 = route[:, 0] * MOE_TM + route[:, 2] - cb8[tile_of, route[:, 0]]
    p2 = route[:, 1] * MOE_TM + route[:, 3] - cb8[tile_of, route[:, 1]]
    gp = (cnt8 + MOE_T - 1) // MOE_T * MOE_T
    off_end = jnp.cumsum(gp)
    off = off_end - gp
    start = (off[None, :] + cb8).reshape(-1)
    nbe = (jnp.concatenate([cb8[1:], cnt8[None]], axis=0) - cb8).reshape(-1)
    n_tiles = (2 * n_tok + N_EXPERTS * (MOE_T - 1)) // MOE_T
    idx = jnp.arange(n_tiles, dtype=jnp.int32)
    valid = idx * MOE_T < off_end[-1]
    blk = jnp.where(valid, idx, off_end[-1] // MOE_T - 1)
    te = jnp.sum((blk * MOE_T)[:, None] >= off_end[None, :], axis=1).astype(jnp.int32)
    return p1, p2, start, nbe, te, blk, valid.astype(jnp.int32), n_tiles


def _segment_copies(n, src_of, dst_of, sem):
    out = []
    for k in range(MOE_TM.bit_length()):
        size = 1 << k
        pos = (n >> (k + 1)) << (k + 1)
        out.append((((n >> k) & 1) == 1, pltpu.make_async_copy(src_of(pos, size), dst_of(pos, size), sem)))
    return out


def _run_copies(copies):
    for cond, cp in copies:
        pl.when(cond)(cp.start)
    for cond, cp in copies:
        pl.when(cond)(cp.wait)


def _moe_gather_kernel(p1_ref, p2_ref, start_ref, nbe_ref, x_ref, xs_init_ref, xs_ref, buf, sem):
    del xs_init_ref
    b = pl.program_id(0)

    def tok(i, carry):
        row = x_ref[i]
        buf[p1_ref[b * MOE_TM + i]] = row
        buf[p2_ref[b * MOE_TM + i]] = row
        return carry

    lax.fori_loop(0, MOE_TM, tok, 0, unroll=8)
    copies = []
    for e in range(N_EXPERTS):
        s = start_ref[b * N_EXPERTS + e]
        copies += _segment_copies(
            nbe_ref[b * N_EXPERTS + e],
            lambda pos, size, e=e: buf.at[pl.ds(e * MOE_TM + pos, size)],
            lambda pos, size, s=s: xs_ref.at[pl.ds(s + pos, size)], sem)
    _run_copies(copies)


def _moe_gather(x3, p1, p2, start, nbe, n_rows):
    n_tok = x3.shape[0]
    tile = pl.BlockSpec((MOE_TM, SUBLANES, LANES), lambda b, *_: (b, 0, 0))
    return pl.pallas_call(
        _moe_gather_kernel,
        grid_spec=pltpu.PrefetchScalarGridSpec(
            num_scalar_prefetch=4, grid=(n_tok // MOE_TM,),
            in_specs=[tile, pl.BlockSpec(memory_space=pl.ANY)],
            out_specs=pl.BlockSpec(memory_space=pl.ANY),
            scratch_shapes=[pltpu.VMEM((N_EXPERTS * MOE_TM, SUBLANES, LANES), F32), pltpu.SemaphoreType.DMA]),
        out_shape=jax.ShapeDtypeStruct((n_rows, SUBLANES, LANES), F32),
        input_output_aliases={5: 0},
        compiler_params=_params(("arbitrary",)),
        name="moe_gather",
    )(p1, p2, start, nbe, x3, jnp.zeros((n_rows, SUBLANES, LANES), F32))


def _moe_ffn_kernel(te_ref, blk_ref, valid_ref, x_ref, wg_ref, wu_ref, wd_ref, y_ref, xb_s, acc_s):
    del te_ref, blk_ref
    j, f = pl.program_id(0), pl.program_id(1)

    @pl.when(valid_ref[j] == 1)
    def _():
        @pl.when(f == 0)
        def _():
            xb_s[...] = x_ref[...].astype(BF16)
            acc_s[...] = jnp.zeros(acc_s.shape, F32)

        acc_s[...] += _swiglu_tile(xb_s, wg_ref, wu_ref, wd_ref)

        @pl.when(f == pl.num_programs(1) - 1)
        def _():
            y_ref[...] = acc_s[...]

    @pl.when((valid_ref[j] == 0) & (f == pl.num_programs(1) - 1))
    def _():
        y_ref[...] = jnp.zeros(y_ref.shape, F32)


def _moe_ffn(xs, te, blk, valid, wg, wu, wd, layer, tf):
    n_rows = xs.shape[0]
    nf = FF_DIM // tf
    rows = pl.BlockSpec((MOE_T, D_MODEL), lambda j, f, te, blk, valid: (blk[j], 0))

    def fsel(j, f, valid):
        return jnp.where(valid[j] == 1, f, nf - 1)

    return pl.pallas_call(
        _moe_ffn_kernel,
        grid_spec=pltpu.PrefetchScalarGridSpec(
            num_scalar_prefetch=3, grid=(n_rows // MOE_T, nf),
            in_specs=[rows,
                      pl.BlockSpec((None, None, D_MODEL, tf),
                                   lambda j, f, te, blk, valid: (layer, te[j], 0, fsel(j, f, valid))),
                      pl.BlockSpec((None, None, D_MODEL, tf),
                                   lambda j, f, te, blk, valid: (layer, te[j], 0, fsel(j, f, valid))),
                      pl.BlockSpec((None, None, tf, D_MODEL),
                                   lambda j, f, te, blk, valid: (layer, te[j], fsel(j, f, valid), 0))],
            out_specs=pl.BlockSpec((MOE_T, D_MODEL), lambda j, f, te, blk, valid: (j, 0)),
            scratch_shapes=[pltpu.VMEM((MOE_T, D_MODEL), BF16), pltpu.VMEM((MOE_T, D_MODEL), F32)]),
        out_shape=jax.ShapeDtypeStruct((n_rows, D_MODEL), F32),
        compiler_params=_params(("arbitrary", "arbitrary")),
        name="moe_ffn",
    )(te, blk, valid, xs, wg, wu, wd)


def _moe_combine_kernel(p1_ref, p2_ref, start_ref, nbe_ref, w1_ref, w2_ref, ys_ref, x_ref, g2_ref, o_ref,
                        buf, sem):
    b = pl.program_id(0)
    copies = []
    for e in range(N_EXPERTS):
        s = start_ref[b * N_EXPERTS + e]
        copies += _segment_copies(
            nbe_ref[b * N_EXPERTS + e],
            lambda pos, size, s=s: ys_ref.at[pl.ds(s + pos, size)],
            lambda pos, size, e=e: buf.at[pl.ds(e * MOE_TM + pos, size)], sem)
    _run_copies(copies)
    g2 = g2_ref[...]

    def tok(i, carry):
        t = b * MOE_TM + i
        o_ref[i] = x_ref[i] + g2 * (w1_ref[t] * buf[p1_ref[t]] + w2_ref[t] * buf[p2_ref[t]])
        return carry

    lax.fori_loop(0, MOE_TM, tok, 0, unroll=8)


def _moe_combine(ys3, x3, g2_3, p1, p2, start, nbe, w1, w2, seq_len):
    n_tok = x3.shape[0]
    per_seq = seq_len // MOE_TM
    tile = pl.BlockSpec((MOE_TM, SUBLANES, LANES), lambda b, *_: (b, 0, 0))
    return pl.pallas_call(
        _moe_combine_kernel,
        grid_spec=pltpu.PrefetchScalarGridSpec(
            num_scalar_prefetch=6, grid=(n_tok // MOE_TM,),
            in_specs=[pl.BlockSpec(memory_space=pl.ANY), tile,
                      pl.BlockSpec((None, SUBLANES, LANES), lambda b, *_: (b // per_seq, 0, 0))],
            out_specs=tile,
            scratch_shapes=[pltpu.VMEM((N_EXPERTS * MOE_TM, SUBLANES, LANES), F32), pltpu.SemaphoreType.DMA]),
        out_shape=jax.ShapeDtypeStruct(x3.shape, F32),
        compiler_params=_params(("arbitrary",)),
        name="moe_combine",
    )(p1, p2, start, nbe, w1, w2, ys3, x3, g2_3)


def _moe_routed(bz, length, h2, x, mod, route, wts, cb, cnt, wg, wu, wd, layer, tf):
    m = bz * length
    as_tiles = lambda a: a.reshape(a.shape[0], SUBLANES, LANES)
    p1, p2, start, nbe, te, blk, valid, n_tiles = _route_plan(route, cb, cnt, m)
    xs3 = _moe_gather(as_tiles(h2.astype(F32)), p1, p2, start, nbe, n_tiles * MOE_T)
    ys = _moe_ffn(xs3.reshape(n_tiles * MOE_T, D_MODEL), te, blk, valid, wg, wu, wd, layer, tf)
    g2_3 = mod[:, 0, 5 * D_MODEL:6 * D_MODEL].reshape(bz, SUBLANES, LANES)
    out3 = _moe_combine(as_tiles(ys), as_tiles(x), g2_3, p1, p2, start, nbe, wts[:, 0], wts[:, 1], length)
    return out3.reshape(m, D_MODEL)


def _final_kernel(x_ref, g_ref, o_ref):
    o_ref[...] = _rms(x_ref[...], g_ref[...])


def _final_norm(grp, x, g):
    return pl.pallas_call(
        _final_kernel, grid=grp.grid,
        in_specs=[grp.rows(D_MODEL), _const_spec((1, D_MODEL))],
        out_specs=grp.rows(D_MODEL),
        out_shape=jax.ShapeDtypeStruct((grp.m, D_MODEL), F32),
        compiler_params=_params(("arbitrary", "arbitrary")),
        name="final_norm",
    )(x, g)


def _relayout_weights(w):
    bf = lambda a: a.astype(BF16)
    depth = w["w_in"].shape[0]
    wi = w["w_in"]
    o = [0, 256, 640, 896, 928, 1184, 1440, 1696, 1952, 5024]
    kr = wi[:, :, o[3]:o[4]]
    half = MLA_ROPE // 2
    kr_rot = jnp.concatenate([-kr[..., half:], kr[..., :half]], axis=-1)
    zeros = lambda n: jnp.zeros(wi.shape[:2] + (n,), wi.dtype)
    w_in = bf(jnp.concatenate([
        wi[:, :, o[0]:o[1]], wi[:, :, o[2]:o[3]], wi[:, :, o[1]:o[2]], zeros(C_HZ - C_CQ - MLA_Q_LORA),
        wi[:, :, o[4]:o[8]], wi[:, :, o[8]:o[9]], kr, zeros(ROPE_PAD - MLA_ROPE),
        kr_rot, zeros(ROPE_PAD - MLA_ROPE)], axis=-1))
    uq = w["mla_w_uq"].reshape(depth, MLA_Q_LORA, MLA_HEADS, MLA_NOPE + MLA_ROPE)
    nope = uq[..., :MLA_NOPE].reshape(depth, MLA_Q_LORA, MLA_HEADS * MLA_NOPE)
    r1 = uq[..., MLA_NOPE:MLA_NOPE + half]
    r2 = uq[..., MLA_NOPE + half:]
    pad = jnp.zeros(uq.shape[:3] + (ROPE_PAD - MLA_ROPE,), uq.dtype)
    rope = jnp.concatenate([r1, r2, pad], axis=-1).reshape(depth, MLA_Q_LORA, MLA_HEADS * ROPE_PAD)
    rope_rot = jnp.concatenate([-r2, r1, pad], axis=-1).reshape(depth, MLA_Q_LORA, MLA_HEADS * ROPE_PAD)
    wuq = bf(jnp.concatenate([nope, rope, rope_rot], axis=-1))
    eye_h = jnp.eye(MLA_HEADS, dtype=F32)
    uk = w["mla_w_uk"].reshape(depth, MLA_KV_LORA, MLA_HEADS, MLA_NOPE)
    wuk = bf(jnp.einsum("lrhd,hg->lhdgr", uk, eye_h).reshape(
        depth, MLA_HEADS * MLA_NOPE, MLA_HEADS * MLA_KV_LORA))
    wuv = bf(w["mla_w_uv"].reshape(depth, MLA_KV_LORA, MLA_HEADS, MLA_V).transpose(0, 2, 1, 3))
    eye_g = jnp.eye(SSM_GROUPS, dtype=F32)
    b_blk = lambda b: bf(jnp.einsum("lgpc,gh->lgchp", b, eye_g).reshape(depth, SSM_WIDTH, SSM_FLAT))
    c_blk = lambda c: bf(jnp.einsum("lgcp,gh->lgphc", c, eye_g).reshape(depth, SSM_FLAT, SSM_WIDTH))
    n_moe = w["moe_router_w"].shape[0]
    rw = jnp.concatenate([w["moe_router_w"],
                          jnp.zeros((n_moe, D_MODEL, LANES - N_EXPERTS), F32)], axis=-1)
    rb = jnp.concatenate([w["moe_router_b"], jnp.zeros((n_moe, LANES - N_EXPERTS), F32)],
                         axis=-1).reshape(n_moe, 1, LANES)
    return dict(
        w_in=w_in, wuq=wuq, wuk=wuk, wuv=wuv,
        bre=b_blk(w["ssm_b_re"]), bim=b_blk(w["ssm_b_im"]),
        cre=c_blk(w["ssm_c_re"]), cim=c_blk(w["ssm_c_im"]),
        glu_w=bf(w["ssm_glu_w"]), proj_a=bf(w["proj_a"]), proj_b=bf(w["proj_b"]),
        proj_c=bf(w["proj_c"]), w_out=bf(w["w_out"]), rw=rw, rb=rb,
        hg_norm=jnp.tile(w["hgrn_norm_g"], (1, HG_HEADS)))


def _rope_tables(pos):
    half = MLA_ROPE // 2
    freq = ROPE_THETA ** (-jnp.arange(half, dtype=F32) / half)
    ang = pos.astype(F32)[:, None] * freq[None, :]
    cos, sin = jnp.cos(ang), jnp.sin(ang)
    pad = jnp.zeros((pos.shape[0], ROPE_PAD - MLA_ROPE), F32)
    cosk = jnp.concatenate([cos, cos, pad], axis=-1)
    sink = jnp.concatenate([sin, sin, pad], axis=-1)
    return jnp.tile(cosk, (1, MLA_HEADS)), jnp.tile(sink, (1, MLA_HEADS)), cosk, sink


def _state_to_blocks(s):
    eye = jnp.eye(HG_HEADS, dtype=F32)
    return jnp.einsum("bhkv,hg->bhvgk", s, eye).reshape(s.shape[0], HG_W, HG_W)


def _blocks_to_state(st):
    b = st.shape[0]
    return jnp.einsum("bhvhk->bhkv", st.reshape(b, HG_HEADS, HG_K, HG_HEADS, HG_K))


def _trunk(x, mod_all, w, rw, prm, pos0, ssm_re0, ssm_im0, hgrn0, cache_lat, cache_kr, page_table):
    bz, length, _ = x.shape
    m = bz * length
    depth = w["w_in"].shape[0]
    prompt = cache_lat is None
    abr, abi, cr, ci, lbs = prm
    if prompt:
        grp = _Group(bz, length, min(256, length), per_row=False)
        grp_f = _Group(bz, length, min(1024, length), per_row=False)
        tl = min(128, length)
        ch = min(32, length)
        tb = min(256, length)
    else:
        grp = _Group(bz, length, m, per_row=True)
        grp_f = grp
        tl, ch, tb = length, length, length
    pos = pos0 + jnp.arange(length, dtype=jnp.int32)
    tables = _rope_tables(pos)
    if not prompt:
        tables = tuple(jnp.tile(t, (bz, 1)) for t in tables)
    cosq, sinq, cosk, sink = tables
    x = x.reshape(m, D_MODEL)
    lat_l, kr_l, sre_l, sim_l, hg_l = [], [], [], [], []
    row = lambda a: a.reshape(1, -1)
    for l in range(depth):
        mod = mod_all[l]
        mod = mod[:, None, :] if prompt else jnp.repeat(mod, length, axis=0)
        u, hz, gates, ql, qr, lat, latb, kr, krb = _in_proj(
            grp, x, mod, row(w["norm1_g"][l]), rw["w_in"][l], row(w["mla_q_norm_g"][l]), rw["wuq"][l],
            rw["wuk"][l], row(w["mla_kv_norm_g"][l]), cosq, sinq, cosk, sink)
        if prompt:
            h0r = jnp.zeros((bz, SSM_FLAT), F32)
            h0i = h0r
            u_tm = u.reshape(length, bz, SSM_WIDTH)
        else:
            h0r = ssm_re0[l].reshape(bz, SSM_FLAT)
            h0i = ssm_im0[l].reshape(bz, SSM_FLAT)
            u_tm = u.reshape(bz, length, SSM_WIDTH).transpose(1, 0, 2)
        ya, htr, hti = _s5(u_tm, h0r, h0i, row(abr[l]), row(abi[l]), row(cr[l]), row(ci[l]),
                           rw["bre"][l], rw["bim"][l], rw["cre"][l], rw["cim"][l], row(w["ssm_d"][l]),
                           rw["glu_w"][l], row(w["ssm_glu_b"][l]), tl)
        if prompt:
            ya = ya.reshape(length, bz * SSM_WIDTH)
        else:
            ya = ya.transpose(1, 0, 2).reshape(m, SSM_WIDTH)
        if prompt:
            yb = _attn_prompt(bz, length, ql, qr, latb, krb, rw["wuv"][l], grp.tm, min(512, length))
        else:
            yb = _attn_sample(l, page_table, ql, qr, latb, krb, cache_lat, cache_kr, rw["wuv"][l], bz, length)
        st0 = jnp.zeros((bz, HG_W, HG_W), F32) if prompt else _state_to_blocks(hgrn0[l])
        yc, st_t = _hgrn(bz, length, hz, st0, row(lbs[l]), row(rw["hg_norm"][l]), ch, tb)
        j = l // 2
        common = (grp, x, ya, yb, yc, gates, mod, row(w["norm2_g"][l]), rw["proj_a"][l], rw["proj_b"][l],
                  rw["proj_c"][l], rw["w_out"][l])
        if l % 2 == 0:
            x, h2 = _mix(*common)
            x = _ffn(grp_f, h2, x, mod, w["ffn_w_gate"], w["ffn_w_up"], w["ffn_w_down"], j, 256)
        else:
            x, h2, gate, route, wts, cb, cnt = _mix(*common, router=(rw["rw"][j], rw["rb"][j]))
            experts = (w["moe_w_gate"], w["moe_w_up"], w["moe_w_down"], j, 256)
            if prompt and grp.tm == MOE_TM:
                x = _moe_routed(bz, length, h2, x, mod, route, wts, cb, cnt, *experts)
            else:
                x = _moe(grp_f, h2, x, mod, gate, *experts)
        lat_l.append(lat.reshape(bz, length, MLA_KV_LORA))
        kr_l.append(kr.reshape(bz, length, MLA_ROPE))
        sre_l.append(htr.reshape(bz, SSM_GROUPS, SSM_STATE))
        sim_l.append(hti.reshape(bz, SSM_GROUPS, SSM_STATE))
        hg_l.append(_blocks_to_state(st_t))
    y = _final_norm(grp, x, row(w["final_norm_g"])).reshape(bz, length, D_MODEL)
    return y, jnp.stack(lat_l), jnp.stack(kr_l), jnp.stack(sre_l), jnp.stack(sim_l), jnp.stack(hg_l)


def kernel(x_prompt, x_sample, c_prompt, c_sample, cache_kv_latent, cache_k_rope, state_ssm_re, state_ssm_im,
           state_hgrn, page_table, ada_w, ada_b, norm1_g, norm2_g, w_in, ssm_a_re, ssm_a_im, ssm_log_dt,
           ssm_b_re, ssm_b_im, ssm_c_re, ssm_c_im, ssm_d, ssm_glu_w, ssm_glu_b, mla_q_norm_g, mla_w_uq,
           mla_kv_norm_g, mla_w_uk, mla_w_uv, hgrn_lb_logits, hgrn_norm_g, proj_a, proj_b, proj_c, w_out,
           ffn_w_gate, ffn_w_up, ffn_w_down, moe_router_w, moe_router_b, moe_w_gate, moe_w_up, moe_w_down,
           final_norm_g):
    w = dict(norm1_g=norm1_g, norm2_g=norm2_g, w_in=w_in, ssm_b_re=ssm_b_re, ssm_b_im=ssm_b_im,
             ssm_c_re=ssm_c_re, ssm_c_im=ssm_c_im, ssm_d=ssm_d, ssm_glu_w=ssm_glu_w, ssm_glu_b=ssm_glu_b,
             mla_q_norm_g=mla_q_norm_g, mla_w_uq=mla_w_uq, mla_kv_norm_g=mla_kv_norm_g, mla_w_uk=mla_w_uk,
             mla_w_uv=mla_w_uv, hgrn_norm_g=hgrn_norm_g, proj_a=proj_a, proj_b=proj_b, proj_c=proj_c,
             w_out=w_out, ffn_w_gate=ffn_w_gate, ffn_w_up=ffn_w_up, ffn_w_down=ffn_w_down,
             moe_router_w=moe_router_w, moe_router_b=moe_router_b, moe_w_gate=moe_w_gate,
             moe_w_up=moe_w_up, moe_w_down=moe_w_down, final_norm_g=final_norm_g)
    rw = _relayout_weights(w)
    prm = _prep_params(ssm_a_re, ssm_a_im, ssm_log_dt, hgrn_lb_logits)
    n_p = c_prompt.shape[0]
    mod_all = _ada_mod(jnp.concatenate([c_prompt, c_sample], axis=0), ada_w, ada_b)
    y_p, lat_p, kr_p, sre_p, sim_p, hg_p = _trunk(
        x_prompt, mod_all[:, :n_p], w, rw, prm, 0, None, None, None, None, None, None)
    past_len = page_table.shape[1] * cache_kv_latent.shape[2]
    y_s, lat_s, kr_s, sre_s, sim_s, hg_s = _trunk(
        x_sample, mod_all[:, n_p:], w, rw, prm, past_len, state_ssm_re, state_ssm_im, state_hgrn,
        cache_kv_latent, jnp.swapaxes(cache_k_rope, 2, 3), page_table)
    return (y_p, y_s, lat_p, kr_p, sre_p, sim_p, hg_p, lat_s, kr_s, sre_s, sim_s, hg_s)
```

```python
import functools

import jax
import jax.numpy as jnp
from jax import lax
from jax.experimental import pallas as pl
from jax.experimental.pallas import tpu as pltpu

F32 = jnp.float32
BF16 = jnp.bfloat16

D_MODEL = 1024
SSM_GROUPS = 16
SSM_GROUP_CH = 16
SSM_WIDTH = 256
SSM_STATE = 64
SSM_FLAT = SSM_GROUPS * SSM_STATE
MLA_HEADS = 8
MLA_NOPE = 64
MLA_ROPE = 32
MLA_V = 64
MLA_Q_LORA = 384
MLA_KV_LORA = 256
MLA_SCALE = (MLA_NOPE + MLA_ROPE) ** -0.5
ROPE_THETA = 10000.0
HG_HEADS = 4
HG_K = 64
HG_W = 256
FF_DIM = 2816
N_EXPERTS = 8
EPS = 1e-6

LANES = 128
SUBLANES = 8
VMEM_LIMIT = 52 * 1024 * 1024
VMEM_LIMIT_IN_PROJ = 58 * 1024 * 1024
MOE_TM = 256
MOE_T = 1024

C_U, C_CKV, C_CQ, C_HZ, C_GATES, C_KR, C_KR_ROT, C_END = 0, 256, 512, 1024, 2048, 5120, 5248, 5376
ROPE_PAD = LANES


def _mm(a, b):
    return jnp.dot(a, b, preferred_element_type=F32)


def _mm_nt(a, b):
    return lax.dot_general(a, b, (((1,), (1,)), ((), ())), preferred_element_type=F32)


def _mm_tn(a, b):
    return lax.dot_general(a, b, (((0,), (0,)), ((), ())), preferred_element_type=F32)


def _mm_f32(a, b):
    return jnp.dot(a, b, preferred_element_type=F32, precision=lax.Precision.HIGHEST)


def _mm_split(a, b):
    ah, bh = a.astype(BF16), b.astype(BF16)
    al = (a - ah.astype(F32)).astype(BF16)
    bl = (b - bh.astype(F32)).astype(BF16)
    return _mm(ah, bh) + (_mm(ah, bl) + _mm(al, bh))


def _rms(x, g):
    return x * lax.rsqrt(jnp.mean(x * x, axis=-1, keepdims=True) + EPS) * g


def _params(sem, vmem_limit=VMEM_LIMIT):
    return pltpu.CompilerParams(dimension_semantics=sem, vmem_limit_bytes=vmem_limit)


def _const_spec(shape, single=False):
    nd = len(shape)
    if single:
        return pl.BlockSpec(shape, lambda *_: (0,) * nd, pipeline_mode=pl.Buffered(1))
    return pl.BlockSpec(shape, lambda *_: (0,) * nd)


class _Group:
    def __init__(self, bz, length, tm, per_row):
        self.bz, self.length, self.tm, self.per_row = bz, length, tm, per_row
        self.m = bz * length
        if per_row:
            self.nb = self.m // tm
            self.grid = (1, self.nb)
        else:
            self.nb = length // tm
            self.grid = (bz, self.nb)

    def rows(self, width):
        nb = self.nb
        return pl.BlockSpec((self.tm, width), lambda b, i, *_: (b * nb + i, 0))

    def mod(self, col):
        if self.per_row:
            return pl.BlockSpec((self.tm, D_MODEL), lambda b, i, *_: (i, col))
        return pl.BlockSpec((None, 1, D_MODEL), lambda b, i, *_: (b, 0, col))

    def pos(self, width):
        return pl.BlockSpec((self.tm, width), lambda b, i, *_: (i, 0))

    def tmajor(self, width):
        if self.per_row:
            return self.rows(width)
        return pl.BlockSpec((self.tm, width), lambda b, i, *_: (i, b))

    def tmajor_shape(self, width):
        return (self.m, width) if self.per_row else (self.length, self.bz * width)


def _param_kernel(ar_ref, ai_ref, ldt_ref, lbl_ref, abr_ref, abi_ref, cr_ref, ci_ref, lbs_ref):
    ar, ai = ar_ref[...], ai_ref[...]
    dt = jnp.exp(ldt_ref[...])
    mag = jnp.exp(dt * ar)
    abr, abi = mag * jnp.cos(dt * ai), mag * jnp.sin(dt * ai)
    den = ar * ar + ai * ai
    abr_ref[...] = abr
    abi_ref[...] = abi
    cr_ref[...] = ((abr - 1.0) * ar + abi * ai) / den
    ci_ref[...] = (abi * ar - (abr - 1.0) * ai) / den
    x = lbl_ref[...]
    e = jnp.exp(x - jnp.max(x, axis=0, keepdims=True))
    p = e / jnp.sum(e, axis=0, keepdims=True)
    rows, acc = [], p[0:1]
    for l in range(x.shape[0]):
        if l:
            acc = acc + p[l:l + 1]
        rows.append(acc - p[0:1])
    lbs_ref[...] = jnp.concatenate(rows, axis=0)


def _prep_params(a_re, a_im, log_dt, lb_logits):
    depth = a_re.shape[0]
    ar = a_re.reshape(depth, SSM_FLAT)
    ai = a_im.reshape(depth, SSM_FLAT)
    ldt = jnp.broadcast_to(log_dt[:, :, None], (depth, SSM_GROUPS, SSM_STATE)).reshape(depth, SSM_FLAT)
    flat = jax.ShapeDtypeStruct((depth, SSM_FLAT), F32)
    return pl.pallas_call(
        _param_kernel,
        out_shape=(flat, flat, flat, flat, jax.ShapeDtypeStruct((depth, HG_W), F32)),
        name="param_prep",
    )(ar, ai, ldt, lb_logits)


def _ada_kernel(c_ref, w_ref, b_ref, o_ref):
    c = c_ref[...]
    s = (c * jax.nn.sigmoid(c)).astype(BF16)
    o_ref[...] = _mm(s, w_ref[...].astype(BF16)) + b_ref[...]


def _ada_mod(c_all, ada_w, ada_b):
    depth, d, n = ada_w.shape
    rows = c_all.shape[0]
    tn = 1536
    return pl.pallas_call(
        _ada_kernel,
        grid=(depth, n // tn),
        in_specs=[pl.BlockSpec((rows, d), lambda l, j: (0, 0)),
                  pl.BlockSpec((None, d, tn), lambda l, j: (l, 0, j)),
                  pl.BlockSpec((None, 1, tn), lambda l, j: (l, 0, j))],
        out_specs=pl.BlockSpec((None, rows, tn), lambda l, j: (l, 0, j)),
        out_shape=jax.ShapeDtypeStruct((depth, rows, n), F32),
        compiler_params=_params(("arbitrary", "arbitrary")),
        name="ada_mod",
    )(c_all, ada_w, ada_b.reshape(depth, 1, n))


def _in_kernel(x_ref, sh_ref, sc_ref, g_ref, w_ref, qg_ref, wuq_ref, wuk_ref, kvg_ref,
               cq_ref, sq_ref, ck_ref, sk_ref,
               u_ref, hz_ref, gates_ref, ql_ref, qr_ref, lat_ref, latb_ref, kr_ref, krb_ref):
    x = x_ref[...]
    h = (_rms(x, g_ref[...]) * (1.0 + sc_ref[...]) + sh_ref[...]).astype(BF16)

    def seg(a, b):
        return _mm(h, w_ref[:, a:b])

    u_ref[...] = seg(C_U, C_CKV)
    hz_ref[...] = seg(C_HZ, C_GATES)
    gates_ref[...] = seg(C_GATES, C_KR)
    cqn = _rms(seg(C_CQ, C_CQ + MLA_Q_LORA), qg_ref[...]).astype(BF16)
    q = _mm(cqn, wuq_ref[...])
    n_nope = MLA_HEADS * MLA_NOPE
    n_rope = MLA_HEADS * ROPE_PAD
    ql = _mm(q[:, :n_nope].astype(BF16), wuk_ref[...]).astype(BF16)
    qr = (q[:, n_nope:n_nope + n_rope] * cq_ref[...] + q[:, n_nope + n_rope:] * sq_ref[...]).astype(BF16)
    for hd in range(MLA_HEADS):
        ql_ref[hd] = ql[:, hd * MLA_KV_LORA:(hd + 1) * MLA_KV_LORA]
        qr_ref[hd] = qr[:, hd * ROPE_PAD:(hd + 1) * ROPE_PAD]
    lat = _rms(seg(C_CKV, C_CQ), kvg_ref[...])
    lat_ref[...] = lat
    latb_ref[...] = lat.astype(BF16)
    kr = seg(C_KR, C_KR_ROT) * ck_ref[...] + seg(C_KR_ROT, C_END) * sk_ref[...]
    kr_ref[...] = kr[:, :MLA_ROPE]
    krb_ref[...] = kr.astype(BF16)


def _in_proj(grp, x, mod, g1, w_in, qg, wuq, wuk, kvg, cosq, sinq, cosk, sink):
    m, tm = grp.m, grp.tm
    n_rope = MLA_HEADS * ROPE_PAD
    nb = grp.nb

    def heads(width):
        return pl.BlockSpec((MLA_HEADS, tm, width), lambda b, i: (0, b * nb + i, 0))

    outs = [
        (grp.tmajor_shape(SSM_WIDTH), F32, grp.tmajor(SSM_WIDTH)),
        ((m, 4 * HG_W), F32, grp.rows(4 * HG_W)),
        ((m, 3 * D_MODEL), F32, grp.rows(3 * D_MODEL)),
        ((MLA_HEADS, m, MLA_KV_LORA), BF16, heads(MLA_KV_LORA)),
        ((MLA_HEADS, m, ROPE_PAD), BF16, heads(ROPE_PAD)),
        ((m, MLA_KV_LORA), F32, grp.rows(MLA_KV_LORA)),
        ((m, MLA_KV_LORA), BF16, grp.rows(MLA_KV_LORA)),
        ((m, MLA_ROPE), F32, grp.rows(MLA_ROPE)),
        ((m, ROPE_PAD), BF16, grp.rows(ROPE_PAD)),
    ]
    return pl.pallas_call(
        _in_kernel,
        grid=grp.grid,
        in_specs=[grp.rows(D_MODEL), grp.mod(0), grp.mod(1), _const_spec((1, D_MODEL)),
                  _const_spec(w_in.shape, True), _const_spec((1, MLA_Q_LORA)), _const_spec(wuq.shape, True),
                  _const_spec(wuk.shape, True), _const_spec((1, MLA_KV_LORA)),
                  grp.pos(n_rope), grp.pos(n_rope), grp.pos(ROPE_PAD), grp.pos(ROPE_PAD)],
        out_specs=[o[2] for o in outs],
        out_shape=[jax.ShapeDtypeStruct(o[0], o[1]) for o in outs],
        compiler_params=_params(("arbitrary", "arbitrary"), VMEM_LIMIT_IN_PROJ),
        name="in_proj",
    )(x, mod, mod, g1, w_in, qg, wuq, wuk, kvg, cosq, sinq, cosk, sink)


def _s5_kernel(u_ref, h0r_ref, h0i_ref, abr_ref, abi_ref, cr_ref, ci_ref, bre_ref, bim_ref,
               cre_ref, cim_ref, d_ref, gw_ref, gb_ref,
               y_ref, htr_ref, hti_ref, hr_s, hi_s, xr_s, xi_s, *, tl, bz):
    i = pl.program_id(0)

    @pl.when(i == 0)
    def _():
        hr_s[...] = h0r_ref[...]
        hi_s[...] = h0i_ref[...]

    u = u_ref[...].reshape(tl * bz, SSM_WIDTH)
    ub = u.astype(BF16)
    bur, bui = _mm(ub, bre_ref[...]), _mm(ub, bim_ref[...])
    cr, ci = cr_ref[...], ci_ref[...]
    xr_s[...] = cr * bur - ci * bui
    xi_s[...] = cr * bui + ci * bur
    abr, abi = abr_ref[...], abi_ref[...]

    def step(t, carry):
        hr, hi = carry
        r0 = pl.multiple_of(t * bz, bz)
        nr = abr * hr - abi * hi + xr_s[pl.ds(r0, bz), :]
        ni = abr * hi + abi * hr + xi_s[pl.ds(r0, bz), :]
        xr_s[pl.ds(r0, bz), :] = nr
        xi_s[pl.ds(r0, bz), :] = ni
        return nr, ni

    hr, hi = lax.fori_loop(0, tl, step, (hr_s[...], hi_s[...]))
    hr_s[...] = hr
    hi_s[...] = hi
    htr_ref[...] = hr
    hti_ref[...] = hi
    y = (_mm(xr_s[...].astype(BF16), cre_ref[...]) - _mm(xi_s[...].astype(BF16), cim_ref[...])
         + d_ref[...] * u)
    y = jax.nn.gelu(y)
    y = y * jax.nn.sigmoid(_mm(y.astype(BF16), gw_ref[...]) + gb_ref[...])
    y_ref[...] = y.reshape(tl, bz, SSM_WIDTH)


def _s5(u_tm, h0r, h0i, abr, abi, cr, ci, bre, bim, cre, cim, d, gw, gb, tl):
    length, bz, _ = u_tm.shape
    row = _const_spec((1, SSM_FLAT))
    st = _const_spec((bz, SSM_FLAT))
    blk = pl.BlockSpec((tl, bz, SSM_WIDTH), lambda i: (i, 0, 0))
    return pl.pallas_call(
        functools.partial(_s5_kernel, tl=tl, bz=bz),
        grid=(length // tl,),
        in_specs=[blk, st, st, row, row, row, row, _const_spec(bre.shape), _const_spec(bim.shape),
                  _const_spec(cre.shape), _const_spec(cim.shape), _const_spec((1, SSM_WIDTH)),
                  _const_spec(gw.shape), _const_spec((1, SSM_WIDTH))],
        out_specs=[blk, st, st],
        out_shape=[jax.ShapeDtypeStruct(u_tm.shape, F32), jax.ShapeDtypeStruct((bz, SSM_FLAT), F32),
                   jax.ShapeDtypeStruct((bz, SSM_FLAT), F32)],
        scratch_shapes=[pltpu.VMEM((bz, SSM_FLAT), F32), pltpu.VMEM((bz, SSM_FLAT), F32),
                        pltpu.VMEM((tl * bz, SSM_FLAT), F32), pltpu.VMEM((tl * bz, SSM_FLAT), F32)],
        compiler_params=_params(("arbitrary",)),
        name="s5_scan",
    )(u_tm, h0r, h0i, abr, abi, cr, ci, bre, bim, cre, cim, d, gw, gb)


def _softmax_step(s, keys, m_ref, l_ref, acc_ref):
    m_old = m_ref[...]
    m_new = jnp.maximum(m_old, jnp.max(s, axis=-1, keepdims=True))
    alpha = jnp.exp(m_old - m_new)
    p = jnp.exp(s - jnp.tile(m_new, (1, s.shape[-1] // LANES)))
    l_ref[...] = alpha * l_ref[...] + jnp.sum(p, axis=-1, keepdims=True)
    acc_ref[...] = (acc_ref[...] * jnp.tile(alpha, (1, acc_ref.shape[-1] // LANES))
                    + _mm(p.astype(BF16), keys))
    m_ref[...] = m_new


def _attn_prompt_kernel(ql_ref, qr_ref, lat_ref, kr_ref, wuv_ref, y_ref, m_s, l_s, acc_s, *, tq, tk):
    i = pl.program_id(1)
    rows = MLA_HEADS * tq
    m_s[...] = jnp.full(m_s.shape, -jnp.inf, F32)
    l_s[...] = jnp.zeros(l_s.shape, F32)
    acc_s[...] = jnp.zeros(acc_s.shape, F32)

    def block(j, masked):
        k0 = pl.multiple_of(j * tk, tk)
        kl = lat_ref[pl.ds(k0, tk), :]
        kr = kr_ref[pl.ds(k0, tk), :]
        s = (_mm_nt(ql_ref[...].reshape(rows, MLA_KV_LORA), kl)
             + _mm_nt(qr_ref[...].reshape(rows, ROPE_PAD), kr)) * MLA_SCALE
        if masked:
            q_pos = i * tq + lax.broadcasted_iota(jnp.int32, (rows, tk), 0) % tq
            k_pos = k0 + lax.broadcasted_iota(jnp.int32, (rows, tk), 1)
            s = jnp.where(k_pos <= q_pos, s, -jnp.inf)
        _softmax_step(s, kl, m_s, l_s, acc_s)

    n_full = (i * tq) // tk

    def full_block(j, carry):
        block(j, False)
        return carry

    lax.fori_loop(0, n_full, full_block, 0)
    block(n_full, True)
    ctx = acc_s[...] / jnp.tile(l_s[...], (1, MLA_KV_LORA // LANES))
    ys = [_mm(ctx[h * tq:(h + 1) * tq, :].astype(BF16), wuv_ref[h]) for h in range(MLA_HEADS)]
    y_ref[...] = jnp.concatenate(ys, axis=-1).astype(BF16)


def _attn_prompt(grp_bz, length, ql, qr, latb, krb, wuv, tq, tk):
    nb = length // tq
    rows = MLA_HEADS * tq
    heads = lambda w: pl.BlockSpec((MLA_HEADS, tq, w), lambda b, i: (0, b * nb + i, 0))
    return pl.pallas_call(
        functools.partial(_attn_prompt_kernel, tq=tq, tk=tk),
        grid=(grp_bz, nb),
        in_specs=[heads(MLA_KV_LORA), heads(ROPE_PAD),
                  pl.BlockSpec((None, length, MLA_KV_LORA), lambda b, i: (b, 0, 0)),
                  pl.BlockSpec((None, length, ROPE_PAD), lambda b, i: (b, 0, 0)),
                  _const_spec(wuv.shape)],
        out_specs=pl.BlockSpec((tq, MLA_HEADS * MLA_V), lambda b, i: (b * nb + i, 0)),
        out_shape=jax.ShapeDtypeStruct((grp_bz * length, MLA_HEADS * MLA_V), BF16),
        scratch_shapes=[pltpu.VMEM((rows, LANES), F32), pltpu.VMEM((rows, LANES), F32),
                        pltpu.VMEM((rows, MLA_KV_LORA), F32)],
        compiler_params=_params(("arbitrary", "arbitrary")),
        name="attn_prompt",
    )(ql, qr, latb.reshape(grp_bz, length, MLA_KV_LORA), krb.reshape(grp_bz, length, ROPE_PAD), wuv)


def _attn_sample_kernel(pt_ref, ql_ref, qr_ref, nl_ref, nk_ref, wuv_ref, lat_hbm, krt_hbm, y_ref,
                        lat_buf, krt_buf, sem, m_s, l_s, acc_s, *, layer, pg, ns, lq):
    b, s_idx = pl.program_id(0), pl.program_id(1)
    n_steps = pl.num_programs(1)
    total = pl.num_programs(0) * n_steps
    step = b * n_steps + s_idx
    slot = step % 2

    def page_copies(seq, st, sl):
        cps = []
        for i in range(pg):
            page = pt_ref[seq, st * pg + i]
            cps.append(pltpu.make_async_copy(lat_hbm.at[layer, page], lat_buf.at[sl, i], sem.at[sl]))
            cps.append(pltpu.make_async_copy(krt_hbm.at[layer, page], krt_buf.at[sl, i], sem.at[sl]))
        return cps

    @pl.when(step == 0)
    def _():
        for cp in page_copies(0, 0, 0):
            cp.start()

    for cp in page_copies(b, s_idx, slot):
        cp.wait()

    @pl.when(step + 1 < total)
    def _():
        nxt = step + 1
        for cp in page_copies(nxt // n_steps, nxt % n_steps, 1 - slot):
            cp.start()

    @pl.when(s_idx == 0)
    def _():
        m_s[...] = jnp.full(m_s.shape, -jnp.inf, F32)
        l_s[...] = jnp.zeros(l_s.shape, F32)
        acc_s[...] = jnp.zeros(acc_s.shape, F32)

    q_l, q_r = ql_ref[...], qr_ref[...]
    per = pg // ns
    for st in range(ns):
        keys = jnp.concatenate([lat_buf[slot, i].astype(BF16) for i in range(st * per, (st + 1) * per)], axis=0)
        krt = jnp.concatenate([krt_buf[slot, i].astype(BF16) for i in range(st * per, (st + 1) * per)], axis=1)
        s = (_mm_nt(q_l, keys) + _mm(q_r, krt)) * MLA_SCALE
        _softmax_step(s, keys, m_s.at[st], l_s.at[st], acc_s.at[st])

    @pl.when(s_idx == n_steps - 1)
    def _():
        rows = MLA_HEADS * lq
        nl, nk = nl_ref[...], nk_ref[...]
        s = (_mm_nt(q_l, nl) + _mm_nt(q_r, nk)) * MLA_SCALE
        q_t = lax.broadcasted_iota(jnp.int32, (rows, lq), 0) % lq
        k_t = lax.broadcasted_iota(jnp.int32, (rows, lq), 1)
        s = jnp.where(k_t <= q_t, s, -jnp.inf)
        m_old = m_s[0]
        m_new = jnp.maximum(m_old, jnp.max(s, axis=-1, keepdims=True))
        for st in range(1, ns):
            m_new = jnp.maximum(m_new, m_s[st])
        p = jnp.exp(s - m_new[:, :1])
        l_tot = jnp.sum(p, axis=-1, keepdims=True)
        acc = _mm(p.astype(BF16), nl)
        for st in range(ns):
            a = jnp.exp(m_s[st] - m_new)
            l_tot = l_tot + a * l_s[st]
            acc = acc + jnp.tile(a, (1, MLA_KV_LORA // LANES)) * acc_s[st]
        ctx = acc / jnp.tile(l_tot, (1, MLA_KV_LORA // LANES))
        ys = [_mm(ctx[h * lq:(h + 1) * lq, :].astype(BF16), wuv_ref[h]) for h in range(MLA_HEADS)]
        y_ref[...] = jnp.concatenate(ys, axis=-1).astype(BF16)


def _attn_sample(layer, page_table, ql, qr, latb, krb, cache_lat, cache_krt, wuv, bz, lq):
    n_pages = page_table.shape[1]
    pg = min(16, n_pages)
    ns = 2 if pg % 2 == 0 else 1
    page = cache_lat.shape[2]
    rows = MLA_HEADS * lq
    qlh = ql.reshape(MLA_HEADS, bz, lq, MLA_KV_LORA).transpose(1, 0, 2, 3).reshape(bz, rows, MLA_KV_LORA)
    qrh = qr[:, :, :MLA_ROPE].reshape(MLA_HEADS, bz, lq, MLA_ROPE).transpose(1, 0, 2, 3).reshape(
        bz, rows, MLA_ROPE)

    def seq(n, w):
        return pl.BlockSpec((None, n, w), lambda b, s, pt: (b, 0, 0))

    in_specs = [seq(rows, MLA_KV_LORA), seq(rows, MLA_ROPE), seq(lq, MLA_KV_LORA), seq(lq, MLA_ROPE),
                pl.BlockSpec(wuv.shape, lambda b, s, pt: (0, 0, 0)),
                pl.BlockSpec(memory_space=pl.ANY), pl.BlockSpec(memory_space=pl.ANY)]
    y = pl.pallas_call(
        functools.partial(_attn_sample_kernel, layer=layer, pg=pg, ns=ns, lq=lq),
        grid_spec=pltpu.PrefetchScalarGridSpec(
            num_scalar_prefetch=1,
            grid=(bz, n_pages // pg),
            in_specs=in_specs,
            out_specs=pl.BlockSpec((None, lq, MLA_HEADS * MLA_V), lambda b, s, pt: (b, 0, 0)),
            scratch_shapes=[pltpu.VMEM((2, pg, page, MLA_KV_LORA), F32), pltpu.VMEM((2, pg, MLA_ROPE, page), F32),
                            pltpu.SemaphoreType.DMA((2,)),
                            pltpu.VMEM((ns, rows, LANES), F32), pltpu.VMEM((ns, rows, LANES), F32),
                            pltpu.VMEM((ns, rows, MLA_KV_LORA), F32)]),
        out_shape=jax.ShapeDtypeStruct((bz, lq, MLA_HEADS * MLA_V), BF16),
        compiler_params=_params(("arbitrary", "arbitrary")),
        name="attn_sample",
    )(page_table, qlh, qrh, latb.reshape(bz, lq, MLA_KV_LORA),
      krb[:, :MLA_ROPE].reshape(bz, lq, MLA_ROPE), wuv, cache_lat, cache_krt)
    return y.reshape(bz * lq, MLA_HEADS * MLA_V)


def _hgrn_kernel(hz_ref, st0_ref, lb_ref, ng_ref, y_ref, stt_ref, st_s, w_s, *, ch, tb, nseq):
    j = pl.program_id(1)

    @pl.when(j == 0)
    def _():
        st_s[...] = st0_ref[...]

    lb = lb_ref[...]
    log_lb, log1m_lb, one_m_lb = jnp.log(lb), jnp.log1p(-lb), 1.0 - lb
    tri = (lax.broadcasted_iota(jnp.int32, (ch, ch), 0)
           >= lax.broadcasted_iota(jnp.int32, (ch, ch), 1)).astype(F32)
    same_head = (lax.broadcasted_iota(jnp.int32, (HG_W, HG_W), 0) // HG_K
                 == lax.broadcasted_iota(jnp.int32, (HG_W, HG_W), 1) // HG_K)
    head_ones = same_head.astype(F32)
    head_ones_b = same_head.astype(BF16)
    ng = ng_ref[...]

    def chunk(c, carry):
        r0 = pl.multiple_of(c * ch, ch)
        for sq in range(nseq):
            chunk_of(sq, r0)
        return carry

    def chunk_of(sq, r0):
        q = hz_ref[sq, pl.ds(r0, ch), 0:HG_W]
        fp = hz_ref[sq, pl.ds(r0, ch), HG_W:2 * HG_W]
        v = hz_ref[sq, pl.ds(r0, ch), 2 * HG_W:3 * HG_W]
        g = hz_ref[sq, pl.ds(r0, ch), 3 * HG_W:4 * HG_W]
        logf = jnp.logaddexp(log_lb, log1m_lb + jax.nn.log_sigmoid(fp))
        k = one_m_lb * jax.nn.sigmoid(-fp)
        b = _mm_f32(tri, logf)
        st = st_s[sq]
        o = _mm_nt((q * jnp.exp(b)).astype(BF16), st.astype(BF16))
        starts, r = [], 0
        for s in range(ch):
            t0 = (s // SUBLANES) * SUBLANES
            t_idx = t0 + lax.broadcasted_iota(jnp.int32, (ch - t0, HG_W), 0)
            e = jnp.exp(jnp.where(t_idx >= s, b[t0:] - b[s:s + 1, :], -jnp.inf))
            w_s[sq, r:r + ch - t0, :] = q[t0:] * k[s:s + 1, :] * e
            starts.append(r)
            r += ch - t0
        att = _mm(w_s[sq].astype(BF16), head_ones_b)
        tiles = [o[t0:t0 + SUBLANES] for t0 in range(0, ch, SUBLANES)]
        for s in range(ch):
            for kt in range(s // SUBLANES, ch // SUBLANES):
                a0 = starts[s] + (kt - s // SUBLANES) * SUBLANES
                tiles[kt] = tiles[kt] + att[a0:a0 + SUBLANES, :] * v[s:s + 1, :]
        o = jnp.concatenate(tiles, axis=0)
        bl = b[ch - 1:ch, :]
        kd = k * jnp.exp(bl - b)
        upd = _mm_tn(v.astype(BF16), kd.astype(BF16))
        st_s[sq] = st * jnp.exp(bl) + jnp.where(same_head, upd, 0.0)
        ms = _mm_f32(o * o, head_ones) * (1.0 / HG_K)
        on = o * lax.rsqrt(ms + EPS) * ng
        y_ref[sq, pl.ds(r0, ch), :] = on * (g * jax.nn.sigmoid(g))

    lax.fori_loop(0, tb // ch, chunk, 0)

    @pl.when(j == pl.num_programs(1) - 1)
    def _():
        stt_ref[...] = st_s[...]


def _hgrn(bz, length, hz, st0, lb, ng, ch, tb):
    nb = length // tb
    nseq = 2 if bz % 2 == 0 else 1
    st_spec = pl.BlockSpec((nseq, HG_W, HG_W), lambda b, j: (b, 0, 0))
    pair_rows = sum(ch - s // SUBLANES * SUBLANES for s in range(ch))
    y, st_t = pl.pallas_call(
        functools.partial(_hgrn_kernel, ch=ch, tb=tb, nseq=nseq),
        grid=(bz // nseq, nb),
        in_specs=[pl.BlockSpec((nseq, tb, 4 * HG_W), lambda b, j: (b, j, 0)), st_spec,
                  _const_spec((1, HG_W)), _const_spec((1, HG_W))],
        out_specs=[pl.BlockSpec((nseq, tb, HG_W), lambda b, j: (b, j, 0)), st_spec],
        out_shape=[jax.ShapeDtypeStruct((bz, length, HG_W), F32),
                   jax.ShapeDtypeStruct((bz, HG_W, HG_W), F32)],
        scratch_shapes=[pltpu.VMEM((nseq, HG_W, HG_W), F32), pltpu.VMEM((nseq, pair_rows, HG_W), F32)],
        compiler_params=_params(("arbitrary", "arbitrary")),
        name="hgrn2",
    )(hz.reshape(bz, length, 4 * HG_W), st0, lb, ng)
    return y.reshape(bz * length, HG_W), st_t


def _mix_kernel(*refs, moe):
    if moe:
        (x_ref, ya_ref, yb_ref, yc_ref, gates_ref, g1_ref, sh2_ref, sc2_ref, n2_ref,
         pa_ref, pb_ref, pc_ref, wo_ref, rw_ref, rb_ref,
         xo_ref, h2_ref, gate_ref, route_ref, wts_ref, cb_ref, cnt_ref, run_s) = refs
    else:
        (x_ref, ya_ref, yb_ref, yc_ref, gates_ref, g1_ref, sh2_ref, sc2_ref, n2_ref,
         pa_ref, pb_ref, pc_ref, wo_ref, xo_ref, h2_ref) = refs
    d = D_MODEL
    mix = (jax.nn.sigmoid(gates_ref[:, 0:d]) * _mm(ya_ref[...].astype(BF16), pa_ref[...])
           + jax.nn.sigmoid(gates_ref[:, d:2 * d]) * _mm(yb_ref[...], pb_ref[...])
           + jax.nn.sigmoid(gates_ref[:, 2 * d:3 * d]) * _mm(yc_ref[...].astype(BF16), pc_ref[...]))
    xo = x_ref[...] + g1_ref[...] * _mm(mix.astype(BF16), wo_ref[...])
    xo_ref[...] = xo
    h2 = _rms(xo, n2_ref[...]) * (1.0 + sc2_ref[...]) + sh2_ref[...]
    h2_ref[...] = h2.astype(BF16)
    if moe:
        logits = _mm_split(h2, rw_ref[...]) + rb_ref[...]
        lane = lax.broadcasted_iota(jnp.int32, logits.shape, 1)
        lg = jnp.where(lane < N_EXPERTS, logits, -jnp.inf)
        m1 = jnp.max(lg, axis=-1, keepdims=True)
        i1 = jnp.min(jnp.where(lg == m1, lane, LANES), axis=-1, keepdims=True)
        lg2 = jnp.where(lane == i1, -jnp.inf, lg)
        m2 = jnp.max(lg2, axis=-1, keepdims=True)
        i2 = jnp.min(jnp.where(lg2 == m2, lane, LANES), axis=-1, keepdims=True)
        e2 = jnp.exp(m2 - m1)
        den = 1.0 + e2
        w1, w2 = 1.0 / den, e2 / den
        gate_ref[...] = jnp.where(lane == i1, w1, 0.0) + jnp.where(lane == i2, w2, 0.0)
        @pl.when((pl.program_id(0) == 0) & (pl.program_id(1) == 0))
        def _():
            run_s[...] = jnp.zeros(run_s.shape, F32)

        tm = logits.shape[0]
        sel = jnp.where(lane == i1, 1.0, 0.0) + jnp.where(lane == i2, 1.0, 0.0)
        before = (lax.broadcasted_iota(jnp.int32, (tm, tm), 0) > lax.broadcasted_iota(jnp.int32, (tm, tm), 1))
        rank = _mm(jnp.where(before, 1.0, 0.0).astype(BF16), sel.astype(BF16))
        r1 = jnp.sum(jnp.where(lane == i1, rank, 0.0), axis=-1, keepdims=True).astype(jnp.int32)
        r2 = jnp.sum(jnp.where(lane == i2, rank, 0.0), axis=-1, keepdims=True).astype(jnp.int32)
        route_ref[...] = jnp.where(lane == 0, i1 * tm + r1, jnp.where(lane == 1, i2 * tm + r2, 0))
        wts_ref[...] = jnp.where(lane == 0, w1, jnp.where(lane == 1, w2, 0.0))
        cb_ref[...] = run_s[...]
        total = run_s[...] + jnp.sum(sel, axis=0, keepdims=True)
        run_s[...] = total
        cnt_ref[...] = total


def _mix(grp, x, ya, yb, yc, gates, mod, n2, pa, pb, pc, wo, router=None):
    m = grp.m
    moe = router is not None
    in_specs = [grp.rows(D_MODEL), grp.tmajor(SSM_WIDTH), grp.rows(MLA_HEADS * MLA_V), grp.rows(HG_W),
                grp.rows(3 * D_MODEL), grp.mod(2), grp.mod(3), grp.mod(4), _const_spec((1, D_MODEL)),
                _const_spec(pa.shape), _const_spec(pb.shape), _const_spec(pc.shape), _const_spec(wo.shape)]
    args = [x, ya, yb, yc, gates, mod, mod, mod, n2, pa, pb, pc, wo]
    out_specs = [grp.rows(D_MODEL), grp.rows(D_MODEL)]
    out_shape = [jax.ShapeDtypeStruct((m, D_MODEL), F32), jax.ShapeDtypeStruct((m, D_MODEL), BF16)]
    if moe:
        in_specs += [_const_spec(router[0].shape), _const_spec(router[1].shape)]
        args += list(router)
        nb, n_tiles = grp.nb, m // grp.tm
        out_specs += [grp.rows(LANES), grp.rows(LANES), grp.rows(LANES),
                      pl.BlockSpec((None, 1, LANES), lambda b, i: (b * nb + i, 0, 0)), _const_spec((1, LANES))]
        out_shape += [jax.ShapeDtypeStruct((m, LANES), F32), jax.ShapeDtypeStruct((m, LANES), jnp.int32),
                      jax.ShapeDtypeStruct((m, LANES), F32), jax.ShapeDtypeStruct((n_tiles, 1, LANES), F32),
                      jax.ShapeDtypeStruct((1, LANES), F32)]
    return pl.pallas_call(
        functools.partial(_mix_kernel, moe=moe),
        grid=grp.grid, in_specs=in_specs, out_specs=out_specs, out_shape=out_shape,
        scratch_shapes=[pltpu.VMEM((1, LANES), F32)] if moe else [],
        compiler_params=_params(("arbitrary", "arbitrary")),
        name="mix_moe" if moe else "mix",
    )(*args)


def _swiglu_tile(h2_ref, wg_ref, wu_ref, wd_ref):
    h = h2_ref[...]
    g = _mm(h, wg_ref[...].astype(BF16))
    u = _mm(h, wu_ref[...].astype(BF16))
    a = (g * jax.nn.sigmoid(g) * u).astype(BF16)
    return _mm(a, wd_ref[...].astype(BF16))


def _ffn_kernel(h2_ref, x_ref, g2_ref, wg_ref, wu_ref, wd_ref, o_ref, acc_s):
    f = pl.program_id(2)

    @pl.when(f == 0)
    def _():
        acc_s[...] = jnp.zeros(acc_s.shape, F32)

    acc_s[...] += _swiglu_tile(h2_ref, wg_ref, wu_ref, wd_ref)

    @pl.when(f == pl.num_programs(2) - 1)
    def _():
        o_ref[...] = x_ref[...] + g2_ref[...] * acc_s[...]


def _ffn(grp, h2, x, mod, wg, wu, wd, j, tf):
    nf = FF_DIM // tf
    return pl.pallas_call(
        _ffn_kernel,
        grid=grp.grid + (nf,),
        in_specs=[grp.rows(D_MODEL), grp.rows(D_MODEL), grp.mod(5),
                  pl.BlockSpec((None, D_MODEL, tf), lambda b, i, f: (j, 0, f)),
                  pl.BlockSpec((None, D_MODEL, tf), lambda b, i, f: (j, 0, f)),
                  pl.BlockSpec((None, tf, D_MODEL), lambda b, i, f: (j, f, 0))],
        out_specs=grp.rows(D_MODEL),
        out_shape=jax.ShapeDtypeStruct((grp.m, D_MODEL), F32),
        scratch_shapes=[pltpu.VMEM((grp.tm, D_MODEL), F32)],
        compiler_params=_params(("arbitrary", "arbitrary", "arbitrary")),
        name="ffn_dense",
    )(h2, x, mod, wg, wu, wd)


def _moe_kernel(h2_ref, x_ref, g2_ref, gate_ref, wg_ref, wu_ref, wd_ref, o_ref, acc_s):
    e, f = pl.program_id(2), pl.program_id(3)

    @pl.when((e == 0) & (f == 0))
    def _():
        acc_s[...] = jnp.zeros(acc_s.shape, F32)

    gate = gate_ref[...]
    lane = lax.broadcasted_iota(jnp.int32, gate.shape, 1)
    gcol = jnp.sum(jnp.where(lane == e, gate, 0.0), axis=-1, keepdims=True)
    acc_s[...] += gcol * _swiglu_tile(h2_ref, wg_ref, wu_ref, wd_ref)

    @pl.when((e == pl.num_programs(2) - 1) & (f == pl.num_programs(3) - 1))
    def _():
        o_ref[...] = x_ref[...] + g2_ref[...] * acc_s[...]


def _moe(grp, h2, x, mod, gate, wg, wu, wd, j, tf):
    nf = FF_DIM // tf
    return pl.pallas_call(
        _moe_kernel,
        grid=grp.grid + (N_EXPERTS, nf),
        in_specs=[grp.rows(D_MODEL), grp.rows(D_MODEL), grp.mod(5), grp.rows(LANES),
                  pl.BlockSpec((None, None, D_MODEL, tf), lambda b, i, e, f: (j, e, 0, f)),
                  pl.BlockSpec((None, None, D_MODEL, tf), lambda b, i, e, f: (j, e, 0, f)),
                  pl.BlockSpec((None, None, tf, D_MODEL), lambda b, i, e, f: (j, e, f, 0))],
        out_specs=grp.rows(D_MODEL),
        out_shape=jax.ShapeDtypeStruct((grp.m, D_MODEL), F32),
        scratch_shapes=[pltpu.VMEM((grp.tm, D_MODEL), F32)],
        compiler_params=_params(("arbitrary",) * 4),
        name="ffn_moe",
    )(h2, x, mod, gate, wg, wu, wd)


def _route_plan(route, cb, cnt, n_tok):
    cnt8 = cnt[0, :N_EXPERTS].astype(jnp.int32)
    cb8 = cb[:, 0, :N_EXPERTS].astype(jnp.int32)
    p1, p2 = route[:, 0], route[:, 1]
    gp = (cnt8 + MOE_T - 1) // MOE_T * MOE_T
    off_end = jnp.cumsum(gp)
    off = off_end - gp
    start = (off[None, :] + cb8).reshape(-1)
    nbe = (jnp.concatenate([cb8[1:], cnt8[None]], axis=0) - cb8).reshape(-1)
    n_tiles = (2 * n_tok + N_EXPERTS * (MOE_T - 1)) // MOE_T
    idx = jnp.arange(n_tiles, dtype=jnp.int32)
    valid = idx * MOE_T < off_end[-1]
    blk = jnp.where(valid, idx, off_end[-1] // MOE_T - 1)
    te = jnp.sum((blk * MOE_T)[:, None] >= off_end[None, :], axis=1).astype(jnp.int32)
    return p1, p2, start, nbe, te, blk, valid.astype(jnp.int32), n_tiles


def _segment_copies(n, src_of, dst_of, sem):
    out = []
    for k in range(MOE_TM.bit_length()):
        size = 1 << k
        pos = (n >> (k + 1)) << (k + 1)
        out.append((((n >> k) & 1) == 1, pltpu.make_async_copy(src_of(pos, size), dst_of(pos, size), sem)))
    return out


def _run_copies(copies):
    for cond, cp in copies:
        pl.when(cond)(cp.start)
    for cond, cp in copies:
        pl.when(cond)(cp.wait)


def _moe_gather_kernel(p1_ref, p2_ref, start_ref, nbe_ref, x_ref, xs_init_ref, xs_ref, buf, sem):
    del xs_init_ref
    b = pl.program_id(0)

    def tok(i, carry):
        row = x_ref[i]
        buf[p1_ref[b * MOE_TM + i]] = row
        buf[p2_ref[b * MOE_TM + i]] = row
        return carry

    lax.fori_loop(0, MOE_TM, tok, 0, unroll=8)
    copies = []
    for e in range(N_EXPERTS):
        s = start_ref[b * N_EXPERTS + e]
        copies += _segment_copies(
            nbe_ref[b * N_EXPERTS + e],
            lambda pos, size, e=e: buf.at[pl.ds(e * MOE_TM + pos, size)],
            lambda pos, size, s=s: xs_ref.at[pl.ds(s + pos, size)], sem)
    _run_copies(copies)


def _moe_gather(x3, p1, p2, start, nbe, n_rows):
    n_tok = x3.shape[0]
    tile = pl.BlockSpec((MOE_TM, SUBLANES, LANES), lambda b, *_: (b, 0, 0))
    return pl.pallas_call(
        _moe_gather_kernel,
        grid_spec=pltpu.PrefetchScalarGridSpec(
            num_scalar_prefetch=4, grid=(n_tok // MOE_TM,),
            in_specs=[tile, pl.BlockSpec(memory_space=pl.ANY)],
            out_specs=pl.BlockSpec(memory_space=pl.ANY),
            scratch_shapes=[pltpu.VMEM((N_EXPERTS * MOE_TM, SUBLANES, LANES), F32), pltpu.SemaphoreType.DMA]),
        out_shape=jax.ShapeDtypeStruct((n_rows, SUBLANES, LANES), F32),
        input_output_aliases={5: 0},
        compiler_params=_params(("arbitrary",)),
        name="moe_gather",
    )(p1, p2, start, nbe, x3, jnp.zeros((n_rows, SUBLANES, LANES), F32))


def _moe_ffn_kernel(te_ref, blk_ref, valid_ref, x_ref, wg_ref, wu_ref, wd_ref, y_ref, xb_s, acc_s):
    del te_ref, blk_ref
    j, f = pl.program_id(0), pl.program_id(1)

    @pl.when(valid_ref[j] == 1)
    def _():
        @pl.when(f == 0)
        def _():
            xb_s[...] = x_ref[...].astype(BF16)
            acc_s[...] = jnp.zeros(acc_s.shape, F32)

        acc_s[...] += _swiglu_tile(xb_s, wg_ref, wu_ref, wd_ref)

        @pl.when(f == pl.num_programs(1) - 1)
        def _():
            y_ref[...] = acc_s[...]

    @pl.when((valid_ref[j] == 0) & (f == pl.num_programs(1) - 1))
    def _():
        y_ref[...] = jnp.zeros(y_ref.shape, F32)


def _moe_ffn(xs, te, blk, valid, wg, wu, wd, layer, tf):
    n_rows = xs.shape[0]
    nf = FF_DIM // tf
    rows = pl.BlockSpec((MOE_T, D_MODEL), lambda j, f, te, blk, valid: (blk[j], 0))

    def fsel(j, f, valid):
        return jnp.where(valid[j] == 1, f, nf - 1)

    return pl.pallas_call(
        _moe_ffn_kernel,
        grid_spec=pltpu.PrefetchScalarGridSpec(
            num_scalar_prefetch=3, grid=(n_rows // MOE_T, nf),
            in_specs=[rows,
                      pl.BlockSpec((None, None, D_MODEL, tf),
                                   lambda j, f, te, blk, valid: (layer, te[j], 0, fsel(j, f, valid))),
                      pl.BlockSpec((None, None, D_MODEL, tf),
                                   lambda j, f, te, blk, valid: (layer, te[j], 0, fsel(j, f, valid))),
                      pl.BlockSpec((None, None, tf, D_MODEL),
                                   lambda j, f, te, blk, valid: (layer, te[j], fsel(j, f, valid), 0))],
            out_specs=pl.BlockSpec((MOE_T, D_MODEL), lambda j, f, te, blk, valid: (j, 0)),
            scratch_shapes=[pltpu.VMEM((MOE_T, D_MODEL), BF16), pltpu.VMEM((MOE_T, D_MODEL), F32)]),
        out_shape=jax.ShapeDtypeStruct((n_rows, D_MODEL), F32),
        compiler_params=_params(("arbitrary", "arbitrary")),
        name="moe_ffn",
    )(te, blk, valid, xs, wg, wu, wd)


def _moe_combine_kernel(p1_ref, p2_ref, start_ref, nbe_ref, w1_ref, w2_ref, ys_ref, x_ref, g2_ref, o_ref,
                        buf, sem):
    b = pl.program_id(0)
    copies = []
    for e in range(N_EXPERTS):
        s = start_ref[b * N_EXPERTS + e]
        copies += _segment_copies(
            nbe_ref[b * N_EXPERTS + e],
            lambda pos, size, s=s: ys_ref.at[pl.ds(s + pos, size)],
            lambda pos, size, e=e: buf.at[pl.ds(e * MOE_TM + pos, size)], sem)
    _run_copies(copies)
    g2 = g2_ref[...]

    def tok(i, carry):
        t = b * MOE_TM + i
        o_ref[i] = x_ref[i] + g2 * (w1_ref[t] * buf[p1_ref[t]] + w2_ref[t] * buf[p2_ref[t]])
        return carry

    lax.fori_loop(0, MOE_TM, tok, 0, unroll=8)


def _moe_combine(ys3, x3, g2_3, p1, p2, start, nbe, w1, w2, seq_len):
    n_tok = x3.shape[0]
    per_seq = seq_len // MOE_TM
    tile = pl.BlockSpec((MOE_TM, SUBLANES, LANES), lambda b, *_: (b, 0, 0))
    return pl.pallas_call(
        _moe_combine_kernel,
        grid_spec=pltpu.PrefetchScalarGridSpec(
            num_scalar_prefetch=6, grid=(n_tok // MOE_TM,),
            in_specs=[pl.BlockSpec(memory_space=pl.ANY), tile,
                      pl.BlockSpec((None, SUBLANES, LANES), lambda b, *_: (b // per_seq, 0, 0))],
            out_specs=tile,
            scratch_shapes=[pltpu.VMEM((N_EXPERTS * MOE_TM, SUBLANES, LANES), F32), pltpu.SemaphoreType.DMA]),
        out_shape=jax.ShapeDtypeStruct(x3.shape, F32),
        compiler_params=_params(("arbitrary",)),
        name="moe_combine",
    )(p1, p2, start, nbe, w1, w2, ys3, x3, g2_3)


def _moe_routed(bz, length, h2, x, mod, route, wts, cb, cnt, wg, wu, wd, layer, tf):
    m = bz * length
    as_tiles = lambda a: a.reshape(a.shape[0], SUBLANES, LANES)
    p1, p2, start, nbe, te, blk, valid, n_tiles = _route_plan(route, cb, cnt, m)
    xs3 = _moe_gather(as_tiles(h2.astype(F32)), p1, p2, start, nbe, n_tiles * MOE_T)
    ys = _moe_ffn(xs3.reshape(n_tiles * MOE_T, D_MODEL), te, blk, valid, wg, wu, wd, layer, tf)
    g2_3 = mod[:, 0, 5 * D_MODEL:6 * D_MODEL].reshape(bz, SUBLANES, LANES)
    out3 = _moe_combine(as_tiles(ys), as_tiles(x), g2_3, p1, p2, start, nbe, wts[:, 0], wts[:, 1], length)
    return out3.reshape(m, D_MODEL)


def _final_kernel(x_ref, g_ref, o_ref):
    o_ref[...] = _rms(x_ref[...], g_ref[...])


def _final_norm(grp, x, g):
    return pl.pallas_call(
        _final_kernel, grid=grp.grid,
        in_specs=[grp.rows(D_MODEL), _const_spec((1, D_MODEL))],
        out_specs=grp.rows(D_MODEL),
        out_shape=jax.ShapeDtypeStruct((grp.m, D_MODEL), F32),
        compiler_params=_params(("arbitrary", "arbitrary")),
        name="final_norm",
    )(x, g)


def _relayout_weights(w):
    bf = lambda a: a.astype(BF16)
    depth = w["w_in"].shape[0]
    wi = w["w_in"]
    o = [0, 256, 640, 896, 928, 1184, 1440, 1696, 1952, 5024]
    kr = wi[:, :, o[3]:o[4]]
    half = MLA_ROPE // 2
    kr_rot = jnp.concatenate([-kr[..., half:], kr[..., :half]], axis=-1)
    zeros = lambda n: jnp.zeros(wi.shape[:2] + (n,), wi.dtype)
    w_in = bf(jnp.concatenate([
        wi[:, :, o[0]:o[1]], wi[:, :, o[2]:o[3]], wi[:, :, o[1]:o[2]], zeros(C_HZ - C_CQ - MLA_Q_LORA),
        wi[:, :, o[4]:o[8]], wi[:, :, o[8]:o[9]], kr, zeros(ROPE_PAD - MLA_ROPE),
        kr_rot, zeros(ROPE_PAD - MLA_ROPE)], axis=-1))
    uq = w["mla_w_uq"].reshape(depth, MLA_Q_LORA, MLA_HEADS, MLA_NOPE + MLA_ROPE)
    nope = uq[..., :MLA_NOPE].reshape(depth, MLA_Q_LORA, MLA_HEADS * MLA_NOPE)
    r1 = uq[..., MLA_NOPE:MLA_NOPE + half]
    r2 = uq[..., MLA_NOPE + half:]
    pad = jnp.zeros(uq.shape[:3] + (ROPE_PAD - MLA_ROPE,), uq.dtype)
    rope = jnp.concatenate([r1, r2, pad], axis=-1).reshape(depth, MLA_Q_LORA, MLA_HEADS * ROPE_PAD)
    rope_rot = jnp.concatenate([-r2, r1, pad], axis=-1).reshape(depth, MLA_Q_LORA, MLA_HEADS * ROPE_PAD)
    wuq = bf(jnp.concatenate([nope, rope, rope_rot], axis=-1))
    eye_h = jnp.eye(MLA_HEADS, dtype=F32)
    uk = w["mla_w_uk"].reshape(depth, MLA_KV_LORA, MLA_HEADS, MLA_NOPE)
    wuk = bf(jnp.einsum("lrhd,hg->lhdgr", uk, eye_h).reshape(
        depth, MLA_HEADS * MLA_NOPE, MLA_HEADS * MLA_KV_LORA))
    wuv = bf(w["mla_w_uv"].reshape(depth, MLA_KV_LORA, MLA_HEADS, MLA_V).transpose(0, 2, 1, 3))
    eye_g = jnp.eye(SSM_GROUPS, dtype=F32)
    b_blk = lambda b: bf(jnp.einsum("lgpc,gh->lgchp", b, eye_g).reshape(depth, SSM_WIDTH, SSM_FLAT))
    c_blk = lambda c: bf(jnp.einsum("lgcp,gh->lgphc", c, eye_g).reshape(depth, SSM_FLAT, SSM_WIDTH))
    n_moe = w["moe_router_w"].shape[0]
    rw = jnp.concatenate([w["moe_router_w"],
                          jnp.zeros((n_moe, D_MODEL, LANES - N_EXPERTS), F32)], axis=-1)
    rb = jnp.concatenate([w["moe_router_b"], jnp.zeros((n_moe, LANES - N_EXPERTS), F32)],
                         axis=-1).reshape(n_moe, 1, LANES)
    return dict(
        w_in=w_in, wuq=wuq, wuk=wuk, wuv=wuv,
        bre=b_blk(w["ssm_b_re"]), bim=b_blk(w["ssm_b_im"]),
        cre=c_blk(w["ssm_c_re"]), cim=c_blk(w["ssm_c_im"]),
        glu_w=bf(w["ssm_glu_w"]), proj_a=bf(w["proj_a"]), proj_b=bf(w["proj_b"]),
        proj_c=bf(w["proj_c"]), w_out=bf(w["w_out"]), rw=rw, rb=rb,
        hg_norm=jnp.tile(w["hgrn_norm_g"], (1, HG_HEADS)))


def _rope_tables(pos):
    half = MLA_ROPE // 2
    freq = ROPE_THETA ** (-jnp.arange(half, dtype=F32) / half)
    ang = pos.astype(F32)[:, None] * freq[None, :]
    cos, sin = jnp.cos(ang), jnp.sin(ang)
    pad = jnp.zeros((pos.shape[0], ROPE_PAD - MLA_ROPE), F32)
    cosk = jnp.concatenate([cos, cos, pad], axis=-1)
    sink = jnp.concatenate([sin, sin, pad], axis=-1)
    return jnp.tile(cosk, (1, MLA_HEADS)), jnp.tile(sink, (1, MLA_HEADS)), cosk, sink


def _state_to_blocks(s):
    eye = jnp.eye(HG_HEADS, dtype=F32)
    return jnp.einsum("bhkv,hg->bhvgk", s, eye).reshape(s.shape[0], HG_W, HG_W)


def _blocks_to_state(st):
    b = st.shape[0]
    return jnp.einsum("bhvhk->bhkv", st.reshape(b, HG_HEADS, HG_K, HG_HEADS, HG_K))


def _trunk(x, mod_all, w, rw, prm, pos0, ssm_re0, ssm_im0, hgrn0, cache_lat, cache_kr, page_table):
    bz, length, _ = x.shape
    m = bz * length
    depth = w["w_in"].shape[0]
    prompt = cache_lat is None
    abr, abi, cr, ci, lbs = prm
    if prompt:
        grp = _Group(bz, length, min(256, length), per_row=False)
        grp_in = _Group(bz, length, min(512, length), per_row=False)
        grp_f = _Group(bz, length, min(1024, length), per_row=False)
        tl = min(128, length)
        ch = min(32, length)
        tb = min(256, length)
    else:
        grp = _Group(bz, length, m, per_row=True)
        grp_in = grp_f = grp
        tl, ch, tb = length, length, length
    pos = pos0 + jnp.arange(length, dtype=jnp.int32)
    tables = _rope_tables(pos)
    if not prompt:
        tables = tuple(jnp.tile(t, (bz, 1)) for t in tables)
    cosq, sinq, cosk, sink = tables
    x = x.reshape(m, D_MODEL)
    lat_l, kr_l, sre_l, sim_l, hg_l = [], [], [], [], []
    row = lambda a: a.reshape(1, -1)
    for l in range(depth):
        mod = mod_all[l]
        mod = mod[:, None, :] if prompt else jnp.repeat(mod, length, axis=0)
        u, hz, gates, ql, qr, lat, latb, kr, krb = _in_proj(
            grp_in, x, mod, row(w["norm1_g"][l]), rw["w_in"][l], row(w["mla_q_norm_g"][l]), rw["wuq"][l],
            rw["wuk"][l], row(w["mla_kv_norm_g"][l]), cosq, sinq, cosk, sink)
        if prompt:
            h0r = jnp.zeros((bz, SSM_FLAT), F32)
            h0i = h0r
            u_tm = u.reshape(length, bz, SSM_WIDTH)
        else:
            h0r = ssm_re0[l].reshape(bz, SSM_FLAT)
            h0i = ssm_im0[l].reshape(bz, SSM_FLAT)
            u_tm = u.reshape(bz, length, SSM_WIDTH).transpose(1, 0, 2)
        ya, htr, hti = _s5(u_tm, h0r, h0i, row(abr[l]), row(abi[l]), row(cr[l]), row(ci[l]),
                           rw["bre"][l], rw["bim"][l], rw["cre"][l], rw["cim"][l], row(w["ssm_d"][l]),
                           rw["glu_w"][l], row(w["ssm_glu_b"][l]), tl)
        if prompt:
            ya = ya.reshape(length, bz * SSM_WIDTH)
        else:
            ya = ya.transpose(1, 0, 2).reshape(m, SSM_WIDTH)
        if prompt:
            yb = _attn_prompt(bz, length, ql, qr, latb, krb, rw["wuv"][l], grp.tm, min(512, length))
        else:
            yb = _attn_sample(l, page_table, ql, qr, latb, krb, cache_lat, cache_kr, rw["wuv"][l], bz, length)
        st0 = jnp.zeros((bz, HG_W, HG_W), F32) if prompt else _state_to_blocks(hgrn0[l])
        yc, st_t = _hgrn(bz, length, hz, st0, row(lbs[l]), row(rw["hg_norm"][l]), ch, tb)
        j = l // 2
        common = (grp, x, ya, yb, yc, gates, mod, row(w["norm2_g"][l]), rw["proj_a"][l], rw["proj_b"][l],
                  rw["proj_c"][l], rw["w_out"][l])
        if l % 2 == 0:
            x, h2 = _mix(*common)
            x = _ffn(grp_f, h2, x, mod, w["ffn_w_gate"], w["ffn_w_up"], w["ffn_w_down"], j, 256)
        else:
            x, h2, gate, route, wts, cb, cnt = _mix(*common, router=(rw["rw"][j], rw["rb"][j]))
            experts = (w["moe_w_gate"], w["moe_w_up"], w["moe_w_down"], j, 256)
            if prompt and grp.tm == MOE_TM:
                x = _moe_routed(bz, length, h2, x, mod, route, wts, cb, cnt, *experts)
            else:
                x = _moe(grp_f, h2, x, mod, gate, *experts)
        lat_l.append(lat.reshape(bz, length, MLA_KV_LORA))
        kr_l.append(kr.reshape(bz, length, MLA_ROPE))
        sre_l.append(htr.reshape(bz, SSM_GROUPS, SSM_STATE))
        sim_l.append(hti.reshape(bz, SSM_GROUPS, SSM_STATE))
        hg_l.append(_blocks_to_state(st_t))
    y = _final_norm(grp, x, row(w["final_norm_g"])).reshape(bz, length, D_MODEL)
    return y, jnp.stack(lat_l), jnp.stack(kr_l), jnp.stack(sre_l), jnp.stack(sim_l), jnp.stack(hg_l)


def kernel(x_prompt, x_sample, c_prompt, c_sample, cache_kv_latent, cache_k_rope, state_ssm_re, state_ssm_im,
           state_hgrn, page_table, ada_w, ada_b, norm1_g, norm2_g, w_in, ssm_a_re, ssm_a_im, ssm_log_dt,
           ssm_b_re, ssm_b_im, ssm_c_re, ssm_c_im, ssm_d, ssm_glu_w, ssm_glu_b, mla_q_norm_g, mla_w_uq,
           mla_kv_norm_g, mla_w_uk, mla_w_uv, hgrn_lb_logits, hgrn_norm_g, proj_a, proj_b, proj_c, w_out,
           ffn_w_gate, ffn_w_up, ffn_w_down, moe_router_w, moe_router_b, moe_w_gate, moe_w_up, moe_w_down,
           final_norm_g):
    w = dict(norm1_g=norm1_g, norm2_g=norm2_g, w_in=w_in, ssm_b_re=ssm_b_re, ssm_b_im=ssm_b_im,
             ssm_c_re=ssm_c_re, ssm_c_im=ssm_c_im, ssm_d=ssm_d, ssm_glu_w=ssm_glu_w, ssm_glu_b=ssm_glu_b,
             mla_q_norm_g=mla_q_norm_g, mla_w_uq=mla_w_uq, mla_kv_norm_g=mla_kv_norm_g, mla_w_uk=mla_w_uk,
             mla_w_uv=mla_w_uv, hgrn_norm_g=hgrn_norm_g, proj_a=proj_a, proj_b=proj_b, proj_c=proj_c,
             w_out=w_out, ffn_w_gate=ffn_w_gate, ffn_w_up=ffn_w_up, ffn_w_down=ffn_w_down,
             moe_router_w=moe_router_w, moe_router_b=moe_router_b, moe_w_gate=moe_w_gate,
             moe_w_up=moe_w_up, moe_w_down=moe_w_down, final_norm_g=final_norm_g)
    rw = _relayout_weights(w)
    prm = _prep_params(ssm_a_re, ssm_a_im, ssm_log_dt, hgrn_lb_logits)
    n_p = c_prompt.shape[0]
    mod_all = _ada_mod(jnp.concatenate([c_prompt, c_sample], axis=0), ada_w, ada_b)
    y_p, lat_p, kr_p, sre_p, sim_p, hg_p = _trunk(
        x_prompt, mod_all[:, :n_p], w, rw, prm, 0, None, None, None, None, None, None)
    past_len = page_table.shape[1] * cache_kv_latent.shape[2]
    y_s, lat_s, kr_s, sre_s, sim_s, hg_s = _trunk(
        x_sample, mod_all[:, n_p:], w, rw, prm, past_len, state_ssm_re, state_ssm_im, state_hgrn,
        cache_kv_latent, jnp.swapaxes(cache_k_rope, 2, 3), page_table)
    return (y_p, y_s, lat_p, kr_p, sre_p, sim_p, hg_p, lat_s, kr_s, sre_s, sim_s, hg_s)
```

```python
import functools

import jax
import jax.numpy as jnp
from jax import lax
from jax.experimental import pallas as pl
from jax.experimental.pallas import tpu as pltpu

F32 = jnp.float32
BF16 = jnp.bfloat16

D_MODEL = 1024
SSM_GROUPS = 16
SSM_GROUP_CH = 16
SSM_WIDTH = 256
SSM_STATE = 64
SSM_FLAT = SSM_GROUPS * SSM_STATE
MLA_HEADS = 8
MLA_NOPE = 64
MLA_ROPE = 32
MLA_V = 64
MLA_Q_LORA = 384
MLA_KV_LORA = 256
MLA_SCALE = (MLA_NOPE + MLA_ROPE) ** -0.5
ROPE_THETA = 10000.0
HG_HEADS = 4
HG_K = 64
HG_W = 256
FF_DIM = 2816
N_EXPERTS = 8
EPS = 1e-6

LANES = 128
SUBLANES = 8
VMEM_LIMIT = 52 * 1024 * 1024
VMEM_LIMIT_IN_PROJ = 58 * 1024 * 1024
PREFETCH_AHEAD = 2
DMA_SLOTS = PREFETCH_AHEAD + 1
MOE_TM = 256
MOE_T = 1024

C_U, C_CKV, C_CQ, C_HZ, C_GATES, C_KR, C_KR_ROT, C_END = 0, 256, 512, 1024, 2048, 5120, 5248, 5376
ROPE_PAD = LANES


def _mm(a, b):
    return jnp.dot(a, b, preferred_element_type=F32)


def _mm_nt(a, b):
    return lax.dot_general(a, b, (((1,), (1,)), ((), ())), preferred_element_type=F32)


def _mm_tn(a, b):
    return lax.dot_general(a, b, (((0,), (0,)), ((), ())), preferred_element_type=F32)


def _mm_f32(a, b):
    return jnp.dot(a, b, preferred_element_type=F32, precision=lax.Precision.HIGHEST)


def _mm_split(a, b):
    ah, bh = a.astype(BF16), b.astype(BF16)
    al = (a - ah.astype(F32)).astype(BF16)
    bl = (b - bh.astype(F32)).astype(BF16)
    return _mm(ah, bh) + (_mm(ah, bl) + _mm(al, bh))


def _rms(x, g):
    return x * lax.rsqrt(jnp.mean(x * x, axis=-1, keepdims=True) + EPS) * g


def _params(sem, vmem_limit=VMEM_LIMIT):
    return pltpu.CompilerParams(dimension_semantics=sem, vmem_limit_bytes=vmem_limit)


def _const_spec(shape, single=False):
    nd = len(shape)
    if single:
        return pl.BlockSpec(shape, lambda *_: (0,) * nd, pipeline_mode=pl.Buffered(1))
    return pl.BlockSpec(shape, lambda *_: (0,) * nd)


class _Group:
    def __init__(self, bz, length, tm, per_row):
        self.bz, self.length, self.tm, self.per_row = bz, length, tm, per_row
        self.m = bz * length
        if per_row:
            self.nb = self.m // tm
            self.grid = (1, self.nb)
        else:
            self.nb = length // tm
            self.grid = (bz, self.nb)

    def rows(self, width):
        nb = self.nb
        return pl.BlockSpec((self.tm, width), lambda b, i, *_: (b * nb + i, 0))

    def mod(self, col):
        if self.per_row:
            return pl.BlockSpec((self.tm, D_MODEL), lambda b, i, *_: (i, col))
        return pl.BlockSpec((None, 1, D_MODEL), lambda b, i, *_: (b, 0, col))

    def pos(self, width):
        return pl.BlockSpec((self.tm, width), lambda b, i, *_: (i, 0))

    def tmajor(self, width):
        if self.per_row:
            return self.rows(width)
        return pl.BlockSpec((self.tm, width), lambda b, i, *_: (i, b))

    def tmajor_shape(self, width):
        return (self.m, width) if self.per_row else (self.length, self.bz * width)


def _param_kernel(ar_ref, ai_ref, ldt_ref, lbl_ref, abr_ref, abi_ref, cr_ref, ci_ref, lbs_ref):
    ar, ai = ar_ref[...], ai_ref[...]
    dt = jnp.exp(ldt_ref[...])
    mag = jnp.exp(dt * ar)
    abr, abi = mag * jnp.cos(dt * ai), mag * jnp.sin(dt * ai)
    den = ar * ar + ai * ai
    abr_ref[...] = abr
    abi_ref[...] = abi
    cr_ref[...] = ((abr - 1.0) * ar + abi * ai) / den
    ci_ref[...] = (abi * ar - (abr - 1.0) * ai) / den
    x = lbl_ref[...]
    e = jnp.exp(x - jnp.max(x, axis=0, keepdims=True))
    p = e / jnp.sum(e, axis=0, keepdims=True)
    rows, acc = [], p[0:1]
    for l in range(x.shape[0]):
        if l:
            acc = acc + p[l:l + 1]
        rows.append(acc - p[0:1])
    lbs_ref[...] = jnp.concatenate(rows, axis=0)


def _prep_params(a_re, a_im, log_dt, lb_logits):
    depth = a_re.shape[0]
    ar = a_re.reshape(depth, SSM_FLAT)
    ai = a_im.reshape(depth, SSM_FLAT)
    ldt = jnp.broadcast_to(log_dt[:, :, None], (depth, SSM_GROUPS, SSM_STATE)).reshape(depth, SSM_FLAT)
    flat = jax.ShapeDtypeStruct((depth, SSM_FLAT), F32)
    return pl.pallas_call(
        _param_kernel,
        out_shape=(flat, flat, flat, flat, jax.ShapeDtypeStruct((depth, HG_W), F32)),
        name="param_prep",
    )(ar, ai, ldt, lb_logits)


def _ada_kernel(c_ref, w_ref, b_ref, o_ref):
    c = c_ref[...]
    s = (c * jax.nn.sigmoid(c)).astype(BF16)
    o_ref[...] = _mm(s, w_ref[...].astype(BF16)) + b_ref[...]


def _ada_mod(c_all, ada_w, ada_b):
    depth, d, n = ada_w.shape
    rows = c_all.shape[0]
    tn = 1536
    return pl.pallas_call(
        _ada_kernel,
        grid=(depth, n // tn),
        in_specs=[pl.BlockSpec((rows, d), lambda l, j: (0, 0)),
                  pl.BlockSpec((None, d, tn), lambda l, j: (l, 0, j)),
                  pl.BlockSpec((None, 1, tn), lambda l, j: (l, 0, j))],
        out_specs=pl.BlockSpec((None, rows, tn), lambda l, j: (l, 0, j)),
        out_shape=jax.ShapeDtypeStruct((depth, rows, n), F32),
        compiler_params=_params(("arbitrary", "arbitrary")),
        name="ada_mod",
    )(c_all, ada_w, ada_b.reshape(depth, 1, n))


def _in_kernel(x_ref, sh_ref, sc_ref, g_ref, w_ref, qg_ref, wuq_ref, wuk_ref, kvg_ref,
               cq_ref, sq_ref, ck_ref, sk_ref,
               u_ref, hz_ref, gates_ref, ql_ref, qr_ref, lat_ref, latb_ref, kr_ref, krb_ref):
    x = x_ref[...]
    h = (_rms(x, g_ref[...]) * (1.0 + sc_ref[...]) + sh_ref[...]).astype(BF16)

    def seg(a, b):
        return _mm(h, w_ref[:, a:b])

    u_ref[...] = seg(C_U, C_CKV)
    hz_ref[...] = seg(C_HZ, C_GATES)
    gates_ref[...] = seg(C_GATES, C_KR)
    cqn = _rms(seg(C_CQ, C_CQ + MLA_Q_LORA), qg_ref[...]).astype(BF16)
    q = _mm(cqn, wuq_ref[...])
    n_nope = MLA_HEADS * MLA_NOPE
    n_rope = MLA_HEADS * ROPE_PAD
    ql = _mm(q[:, :n_nope].astype(BF16), wuk_ref[...]).astype(BF16)
    qr = (q[:, n_nope:n_nope + n_rope] * cq_ref[...] + q[:, n_nope + n_rope:] * sq_ref[...]).astype(BF16)
    for hd in range(MLA_HEADS):
        ql_ref[hd] = ql[:, hd * MLA_KV_LORA:(hd + 1) * MLA_KV_LORA]
        qr_ref[hd] = qr[:, hd * ROPE_PAD:(hd + 1) * ROPE_PAD]
    lat = _rms(seg(C_CKV, C_CQ), kvg_ref[...])
    lat_ref[...] = lat
    latb_ref[...] = lat.astype(BF16)
    kr = seg(C_KR, C_KR_ROT) * ck_ref[...] + seg(C_KR_ROT, C_END) * sk_ref[...]
    kr_ref[...] = kr[:, :MLA_ROPE]
    krb_ref[...] = kr.astype(BF16)


def _in_proj(grp, x, mod, g1, w_in, qg, wuq, wuk, kvg, cosq, sinq, cosk, sink):
    m, tm = grp.m, grp.tm
    n_rope = MLA_HEADS * ROPE_PAD
    nb = grp.nb

    def heads(width):
        return pl.BlockSpec((MLA_HEADS, tm, width), lambda b, i: (0, b * nb + i, 0))

    outs = [
        (grp.tmajor_shape(SSM_WIDTH), F32, grp.tmajor(SSM_WIDTH)),
        ((m, 4 * HG_W), F32, grp.rows(4 * HG_W)),
        ((m, 3 * D_MODEL), F32, grp.rows(3 * D_MODEL)),
        ((MLA_HEADS, m, MLA_KV_LORA), BF16, heads(MLA_KV_LORA)),
        ((MLA_HEADS, m, ROPE_PAD), BF16, heads(ROPE_PAD)),
        ((m, MLA_KV_LORA), F32, grp.rows(MLA_KV_LORA)),
        ((m, MLA_KV_LORA), BF16, grp.rows(MLA_KV_LORA)),
        ((m, MLA_ROPE), F32, grp.rows(MLA_ROPE)),
        ((m, ROPE_PAD), BF16, grp.rows(ROPE_PAD)),
    ]
    return pl.pallas_call(
        _in_kernel,
        grid=grp.grid,
        in_specs=[grp.rows(D_MODEL), grp.mod(0), grp.mod(1), _const_spec((1, D_MODEL)),
                  _const_spec(w_in.shape, True), _const_spec((1, MLA_Q_LORA)), _const_spec(wuq.shape, True),
                  _const_spec(wuk.shape, True), _const_spec((1, MLA_KV_LORA)),
                  grp.pos(n_rope), grp.pos(n_rope), grp.pos(ROPE_PAD), grp.pos(ROPE_PAD)],
        out_specs=[o[2] for o in outs],
        out_shape=[jax.ShapeDtypeStruct(o[0], o[1]) for o in outs],
        compiler_params=_params(("arbitrary", "arbitrary"), VMEM_LIMIT_IN_PROJ),
        name="in_proj",
    )(x, mod, mod, g1, w_in, qg, wuq, wuk, kvg, cosq, sinq, cosk, sink)


def _s5_kernel(u_ref, h0r_ref, h0i_ref, abr_ref, abi_ref, cr_ref, ci_ref, bre_ref, bim_ref,
               cre_ref, cim_ref, d_ref, gw_ref, gb_ref,
               y_ref, htr_ref, hti_ref, hr_s, hi_s, xr_s, xi_s, *, tl, bz):
    i = pl.program_id(0)

    @pl.when(i == 0)
    def _():
        hr_s[...] = h0r_ref[...]
        hi_s[...] = h0i_ref[...]

    u = u_ref[...].reshape(tl * bz, SSM_WIDTH)
    ub = u.astype(BF16)
    bur, bui = _mm(ub, bre_ref[...]), _mm(ub, bim_ref[...])
    cr, ci = cr_ref[...], ci_ref[...]
    xr_s[...] = cr * bur - ci * bui
    xi_s[...] = cr * bui + ci * bur
    abr, abi = abr_ref[...], abi_ref[...]

    def step(t, carry):
        hr, hi = carry
        r0 = pl.multiple_of(t * bz, bz)
        nr = abr * hr - abi * hi + xr_s[pl.ds(r0, bz), :]
        ni = abr * hi + abi * hr + xi_s[pl.ds(r0, bz), :]
        xr_s[pl.ds(r0, bz), :] = nr
        xi_s[pl.ds(r0, bz), :] = ni
        return nr, ni

    hr, hi = lax.fori_loop(0, tl, step, (hr_s[...], hi_s[...]))
    hr_s[...] = hr
    hi_s[...] = hi
    htr_ref[...] = hr
    hti_ref[...] = hi
    y = (_mm(xr_s[...].astype(BF16), cre_ref[...]) - _mm(xi_s[...].astype(BF16), cim_ref[...])
         + d_ref[...] * u)
    y = jax.nn.gelu(y)
    y = y * jax.nn.sigmoid(_mm(y.astype(BF16), gw_ref[...]) + gb_ref[...])
    y_ref[...] = y.reshape(tl, bz, SSM_WIDTH)


def _s5(u_tm, h0r, h0i, abr, abi, cr, ci, bre, bim, cre, cim, d, gw, gb, tl):
    length, bz, _ = u_tm.shape
    row = _const_spec((1, SSM_FLAT))
    st = _const_spec((bz, SSM_FLAT))
    blk = pl.BlockSpec((tl, bz, SSM_WIDTH), lambda i: (i, 0, 0))
    return pl.pallas_call(
        functools.partial(_s5_kernel, tl=tl, bz=bz),
        grid=(length // tl,),
        in_specs=[blk, st, st, row, row, row, row, _const_spec(bre.shape), _const_spec(bim.shape),
                  _const_spec(cre.shape), _const_spec(cim.shape), _const_spec((1, SSM_WIDTH)),
                  _const_spec(gw.shape), _const_spec((1, SSM_WIDTH))],
        out_specs=[blk, st, st],
        out_shape=[jax.ShapeDtypeStruct(u_tm.shape, F32), jax.ShapeDtypeStruct((bz, SSM_FLAT), F32),
                   jax.ShapeDtypeStruct((bz, SSM_FLAT), F32)],
        scratch_shapes=[pltpu.VMEM((bz, SSM_FLAT), F32), pltpu.VMEM((bz, SSM_FLAT), F32),
                        pltpu.VMEM((tl * bz, SSM_FLAT), F32), pltpu.VMEM((tl * bz, SSM_FLAT), F32)],
        compiler_params=_params(("arbitrary",)),
        name="s5_scan",
    )(u_tm, h0r, h0i, abr, abi, cr, ci, bre, bim, cre, cim, d, gw, gb)


def _softmax_step(s, keys, m_ref, l_ref, acc_ref):
    m_old = m_ref[...]
    m_new = jnp.maximum(m_old, jnp.max(s, axis=-1, keepdims=True))
    alpha = jnp.exp(m_old - m_new)
    p = jnp.exp(s - jnp.tile(m_new, (1, s.shape[-1] // LANES)))
    l_ref[...] = alpha * l_ref[...] + jnp.sum(p, axis=-1, keepdims=True)
    acc_ref[...] = (acc_ref[...] * jnp.tile(alpha, (1, acc_ref.shape[-1] // LANES))
                    + _mm(p.astype(BF16), keys))
    m_ref[...] = m_new


def _attn_prompt_kernel(ql_ref, qr_ref, lat_ref, kr_ref, wuv_ref, y_ref, m_s, l_s, acc_s, *, tq, tk):
    i = pl.program_id(1)
    rows = MLA_HEADS * tq
    m_s[...] = jnp.full(m_s.shape, -jnp.inf, F32)
    l_s[...] = jnp.zeros(l_s.shape, F32)
    acc_s[...] = jnp.zeros(acc_s.shape, F32)

    def block(j, masked):
        k0 = pl.multiple_of(j * tk, tk)
        kl = lat_ref[pl.ds(k0, tk), :]
        kr = kr_ref[pl.ds(k0, tk), :]
        s = (_mm_nt(ql_ref[...].reshape(rows, MLA_KV_LORA), kl)
             + _mm_nt(qr_ref[...].reshape(rows, ROPE_PAD), kr)) * MLA_SCALE
        if masked:
            q_pos = i * tq + lax.broadcasted_iota(jnp.int32, (rows, tk), 0) % tq
            k_pos = k0 + lax.broadcasted_iota(jnp.int32, (rows, tk), 1)
            s = jnp.where(k_pos <= q_pos, s, -jnp.inf)
        _softmax_step(s, kl, m_s, l_s, acc_s)

    n_full = (i * tq) // tk

    def full_block(j, carry):
        block(j, False)
        return carry

    lax.fori_loop(0, n_full, full_block, 0)
    block(n_full, True)
    ctx = acc_s[...] / jnp.tile(l_s[...], (1, MLA_KV_LORA // LANES))
    ys = [_mm(ctx[h * tq:(h + 1) * tq, :].astype(BF16), wuv_ref[h]) for h in range(MLA_HEADS)]
    y_ref[...] = jnp.concatenate(ys, axis=-1).astype(BF16)


def _attn_prompt(grp_bz, length, ql, qr, latb, krb, wuv, tq, tk):
    nb = length // tq
    rows = MLA_HEADS * tq
    heads = lambda w: pl.BlockSpec((MLA_HEADS, tq, w), lambda b, i: (0, b * nb + i, 0))
    return pl.pallas_call(
        functools.partial(_attn_prompt_kernel, tq=tq, tk=tk),
        grid=(grp_bz, nb),
        in_specs=[heads(MLA_KV_LORA), heads(ROPE_PAD),
                  pl.BlockSpec((None, length, MLA_KV_LORA), lambda b, i: (b, 0, 0)),
                  pl.BlockSpec((None, length, ROPE_PAD), lambda b, i: (b, 0, 0)),
                  _const_spec(wuv.shape)],
        out_specs=pl.BlockSpec((tq, MLA_HEADS * MLA_V), lambda b, i: (b * nb + i, 0)),
        out_shape=jax.ShapeDtypeStruct((grp_bz * length, MLA_HEADS * MLA_V), BF16),
        scratch_shapes=[pltpu.VMEM((rows, LANES), F32), pltpu.VMEM((rows, LANES), F32),
                        pltpu.VMEM((rows, MLA_KV_LORA), F32)],
        compiler_params=_params(("arbitrary", "arbitrary")),
        name="attn_prompt",
    )(ql, qr, latb.reshape(grp_bz, length, MLA_KV_LORA), krb.reshape(grp_bz, length, ROPE_PAD), wuv)


def _attn_sample_kernel(pt_ref, ql_ref, qr_ref, nl_ref, nk_ref, wuv_ref, lat_hbm, krt_hbm, y_ref,
                        lat_buf, krt_buf, sem, m_s, l_s, acc_s, *, layer, pg, ns, lq):
    b, s_idx = pl.program_id(0), pl.program_id(1)
    n_steps = pl.num_programs(1)
    total = pl.num_programs(0) * n_steps
    step = b * n_steps + s_idx
    slot = step % DMA_SLOTS

    def page_copies(at_step):
        seq, st, sl = at_step // n_steps, at_step % n_steps, at_step % DMA_SLOTS
        cps = []
        for i in range(pg):
            page = pt_ref[seq, st * pg + i]
            cps.append(pltpu.make_async_copy(lat_hbm.at[layer, page], lat_buf.at[sl, i], sem.at[sl]))
            cps.append(pltpu.make_async_copy(krt_hbm.at[layer, page], krt_buf.at[sl, i], sem.at[sl]))
        return cps

    for first in range(PREFETCH_AHEAD):
        @pl.when((step == 0) & (first < total))
        def _(first=first):
            for cp in page_copies(first):
                cp.start()

    for cp in page_copies(step):
        cp.wait()

    @pl.when(step + PREFETCH_AHEAD < total)
    def _():
        for cp in page_copies(step + PREFETCH_AHEAD):
            cp.start()

    @pl.when(s_idx == 0)
    def _():
        m_s[...] = jnp.full(m_s.shape, -jnp.inf, F32)
        l_s[...] = jnp.zeros(l_s.shape, F32)
        acc_s[...] = jnp.zeros(acc_s.shape, F32)

    q_l, q_r = ql_ref[...], qr_ref[...]
    per = pg // ns
    for st in range(ns):
        keys = jnp.concatenate([lat_buf[slot, i].astype(BF16) for i in range(st * per, (st + 1) * per)], axis=0)
        krt = jnp.concatenate([krt_buf[slot, i].astype(BF16) for i in range(st * per, (st + 1) * per)], axis=1)
        s = (_mm_nt(q_l, keys) + _mm(q_r, krt)) * MLA_SCALE
        _softmax_step(s, keys, m_s.at[st], l_s.at[st], acc_s.at[st])

    @pl.when(s_idx == n_steps - 1)
    def _():
        rows = MLA_HEADS * lq
        nl, nk = nl_ref[...], nk_ref[...]
        s = (_mm_nt(q_l, nl) + _mm_nt(q_r, nk)) * MLA_SCALE
        q_t = lax.broadcasted_iota(jnp.int32, (rows, lq), 0) % lq
        k_t = lax.broadcasted_iota(jnp.int32, (rows, lq), 1)
        s = jnp.where(k_t <= q_t, s, -jnp.inf)
        m_old = m_s[0]
        m_new = jnp.maximum(m_old, jnp.max(s, axis=-1, keepdims=True))
        for st in range(1, ns):
            m_new = jnp.maximum(m_new, m_s[st])
        p = jnp.exp(s - m_new[:, :1])
        l_tot = jnp.sum(p, axis=-1, keepdims=True)
        acc = _mm(p.astype(BF16), nl)
        for st in range(ns):
            a = jnp.exp(m_s[st] - m_new)
            l_tot = l_tot + a * l_s[st]
            acc = acc + jnp.tile(a, (1, MLA_KV_LORA // LANES)) * acc_s[st]
        ctx = acc / jnp.tile(l_tot, (1, MLA_KV_LORA // LANES))
        ys = [_mm(ctx[h * lq:(h + 1) * lq, :].astype(BF16), wuv_ref[h]) for h in range(MLA_HEADS)]
        y_ref[...] = jnp.concatenate(ys, axis=-1).astype(BF16)


def _attn_sample(layer, page_table, ql, qr, latb, krb, cache_lat, cache_krt, wuv, bz, lq):
    n_pages = page_table.shape[1]
    pg = min(16, n_pages)
    ns = 2 if pg % 2 == 0 else 1
    page = cache_lat.shape[2]
    rows = MLA_HEADS * lq
    qlh = ql.reshape(MLA_HEADS, bz, lq, MLA_KV_LORA).transpose(1, 0, 2, 3).reshape(bz, rows, MLA_KV_LORA)
    qrh = qr[:, :, :MLA_ROPE].reshape(MLA_HEADS, bz, lq, MLA_ROPE).transpose(1, 0, 2, 3).reshape(
        bz, rows, MLA_ROPE)

    def seq(n, w):
        return pl.BlockSpec((None, n, w), lambda b, s, pt: (b, 0, 0))

    in_specs = [seq(rows, MLA_KV_LORA), seq(rows, MLA_ROPE), seq(lq, MLA_KV_LORA), seq(lq, MLA_ROPE),
                pl.BlockSpec(wuv.shape, lambda b, s, pt: (0, 0, 0)),
                pl.BlockSpec(memory_space=pl.ANY), pl.BlockSpec(memory_space=pl.ANY)]
    y = pl.pallas_call(
        functools.partial(_attn_sample_kernel, layer=layer, pg=pg, ns=ns, lq=lq),
        grid_spec=pltpu.PrefetchScalarGridSpec(
            num_scalar_prefetch=1,
            grid=(bz, n_pages // pg),
            in_specs=in_specs,
            out_specs=pl.BlockSpec((None, lq, MLA_HEADS * MLA_V), lambda b, s, pt: (b, 0, 0)),
            scratch_shapes=[pltpu.VMEM((DMA_SLOTS, pg, page, MLA_KV_LORA), F32),
                            pltpu.VMEM((DMA_SLOTS, pg, MLA_ROPE, page), F32),
                            pltpu.SemaphoreType.DMA((DMA_SLOTS,)),
                            pltpu.VMEM((ns, rows, LANES), F32), pltpu.VMEM((ns, rows, LANES), F32),
                            pltpu.VMEM((ns, rows, MLA_KV_LORA), F32)]),
        out_shape=jax.ShapeDtypeStruct((bz, lq, MLA_HEADS * MLA_V), BF16),
        compiler_params=_params(("arbitrary", "arbitrary")),
        name="attn_sample",
    )(page_table, qlh, qrh, latb.reshape(bz, lq, MLA_KV_LORA),
      krb[:, :MLA_ROPE].reshape(bz, lq, MLA_ROPE), wuv, cache_lat, cache_krt)
    return y.reshape(bz * lq, MLA_HEADS * MLA_V)


def _hgrn_kernel(hz_ref, st0_ref, lb_ref, ng_ref, y_ref, stt_ref, st_s, w_s, *, ch, tb, nseq):
    j = pl.program_id(1)

    @pl.when(j == 0)
    def _():
        st_s[...] = st0_ref[...]

    lb = lb_ref[...]
    log_lb, log1m_lb, one_m_lb = jnp.log(lb), jnp.log1p(-lb), 1.0 - lb
    tri = (lax.broadcasted_iota(jnp.int32, (ch, ch), 0)
           >= lax.broadcasted_iota(jnp.int32, (ch, ch), 1)).astype(F32)
    same_head = (lax.broadcasted_iota(jnp.int32, (HG_W, HG_W), 0) // HG_K
                 == lax.broadcasted_iota(jnp.int32, (HG_W, HG_W), 1) // HG_K)
    head_ones = same_head.astype(F32)
    head_ones_b = same_head.astype(BF16)
    ng = ng_ref[...]

    def chunk(c, carry):
        r0 = pl.multiple_of(c * ch, ch)
        for sq in range(nseq):
            chunk_of(sq, r0)
        return carry

    def chunk_of(sq, r0):
        q = hz_ref[sq, pl.ds(r0, ch), 0:HG_W]
        fp = hz_ref[sq, pl.ds(r0, ch), HG_W:2 * HG_W]
        v = hz_ref[sq, pl.ds(r0, ch), 2 * HG_W:3 * HG_W]
        g = hz_ref[sq, pl.ds(r0, ch), 3 * HG_W:4 * HG_W]
        logf = jnp.logaddexp(log_lb, log1m_lb + jax.nn.log_sigmoid(fp))
        k = one_m_lb * jax.nn.sigmoid(-fp)
        b = _mm_f32(tri, logf)
        st = st_s[sq]
        o = _mm_nt((q * jnp.exp(b)).astype(BF16), st.astype(BF16))
        starts, r = [], 0
        for s in range(ch):
            t0 = (s // SUBLANES) * SUBLANES
            t_idx = t0 + lax.broadcasted_iota(jnp.int32, (ch - t0, HG_W), 0)
            e = jnp.exp(jnp.where(t_idx >= s, b[t0:] - b[s:s + 1, :], -jnp.inf))
            w_s[sq, r:r + ch - t0, :] = q[t0:] * k[s:s + 1, :] * e
            starts.append(r)
            r += ch - t0
        att = _mm(w_s[sq].astype(BF16), head_ones_b)
        tiles = [o[t0:t0 + SUBLANES] for t0 in range(0, ch, SUBLANES)]
        for s in range(ch):
            for kt in range(s // SUBLANES, ch // SUBLANES):
                a0 = starts[s] + (kt - s // SUBLANES) * SUBLANES
                tiles[kt] = tiles[kt] + att[a0:a0 + SUBLANES, :] * v[s:s + 1, :]
        o = jnp.concatenate(tiles, axis=0)
        bl = b[ch - 1:ch, :]
        kd = k * jnp.exp(bl - b)
        upd = _mm_tn(v.astype(BF16), kd.astype(BF16))
        st_s[sq] = st * jnp.exp(bl) + jnp.where(same_head, upd, 0.0)
        ms = _mm_f32(o * o, head_ones) * (1.0 / HG_K)
        on = o * lax.rsqrt(ms + EPS) * ng
        y_ref[sq, pl.ds(r0, ch), :] = on * (g * jax.nn.sigmoid(g))

    lax.fori_loop(0, tb // ch, chunk, 0)

    @pl.when(j == pl.num_programs(1) - 1)
    def _():
        stt_ref[...] = st_s[...]


def _hgrn(bz, length, hz, st0, lb, ng, ch, tb):
    nb = length // tb
    nseq = 2 if bz % 2 == 0 else 1
    st_spec = pl.BlockSpec((nseq, HG_W, HG_W), lambda b, j: (b, 0, 0))
    pair_rows = sum(ch - s // SUBLANES * SUBLANES for s in range(ch))
    y, st_t = pl.pallas_call(
        functools.partial(_hgrn_kernel, ch=ch, tb=tb, nseq=nseq),
        grid=(bz // nseq, nb),
        in_specs=[pl.BlockSpec((nseq, tb, 4 * HG_W), lambda b, j: (b, j, 0)), st_spec,
                  _const_spec((1, HG_W)), _const_spec((1, HG_W))],
        out_specs=[pl.BlockSpec((nseq, tb, HG_W), lambda b, j: (b, j, 0)), st_spec],
        out_shape=[jax.ShapeDtypeStruct((bz, length, HG_W), F32),
                   jax.ShapeDtypeStruct((bz, HG_W, HG_W), F32)],
        scratch_shapes=[pltpu.VMEM((nseq, HG_W, HG_W), F32), pltpu.VMEM((nseq, pair_rows, HG_W), F32)],
        compiler_params=_params(("arbitrary", "arbitrary")),
        name="hgrn2",
    )(hz.reshape(bz, length, 4 * HG_W), st0, lb, ng)
    return y.reshape(bz * length, HG_W), st_t


def _mix_kernel(*refs, moe):
    if moe:
        (x_ref, ya_ref, yb_ref, yc_ref, gates_ref, g1_ref, sh2_ref, sc2_ref, n2_ref,
         pa_ref, pb_ref, pc_ref, wo_ref, rw_ref, rb_ref,
         xo_ref, h2_ref, gate_ref, route_ref, wts_ref, cb_ref, cnt_ref, run_s) = refs
    else:
        (x_ref, ya_ref, yb_ref, yc_ref, gates_ref, g1_ref, sh2_ref, sc2_ref, n2_ref,
         pa_ref, pb_ref, pc_ref, wo_ref, xo_ref, h2_ref) = refs
    d = D_MODEL
    mix = (jax.nn.sigmoid(gates_ref[:, 0:d]) * _mm(ya_ref[...].astype(BF16), pa_ref[...])
           + jax.nn.sigmoid(gates_ref[:, d:2 * d]) * _mm(yb_ref[...], pb_ref[...])
           + jax.nn.sigmoid(gates_ref[:, 2 * d:3 * d]) * _mm(yc_ref[...].astype(BF16), pc_ref[...]))
    xo = x_ref[...] + g1_ref[...] * _mm(mix.astype(BF16), wo_ref[...])
    xo_ref[...] = xo
    h2 = _rms(xo, n2_ref[...]) * (1.0 + sc2_ref[...]) + sh2_ref[...]
    h2_ref[...] = h2.astype(BF16)
    if moe:
        logits = _mm_split(h2, rw_ref[...]) + rb_ref[...]
        lane = lax.broadcasted_iota(jnp.int32, logits.shape, 1)
        lg = jnp.where(lane < N_EXPERTS, logits, -jnp.inf)
        m1 = jnp.max(lg, axis=-1, keepdims=True)
        i1 = jnp.min(jnp.where(lg == m1, lane, LANES), axis=-1, keepdims=True)
        lg2 = jnp.where(lane == i1, -jnp.inf, lg)
        m2 = jnp.max(lg2, axis=-1, keepdims=True)
        i2 = jnp.min(jnp.where(lg2 == m2, lane, LANES), axis=-1, keepdims=True)
        e2 = jnp.exp(m2 - m1)
        den = 1.0 + e2
        w1, w2 = 1.0 / den, e2 / den
        gate_ref[...] = jnp.where(lane == i1, w1, 0.0) + jnp.where(lane == i2, w2, 0.0)
        @pl.when((pl.program_id(0) == 0) & (pl.program_id(1) == 0))
        def _():
            run_s[...] = jnp.zeros(run_s.shape, F32)

        tm = logits.shape[0]
        sel = jnp.where(lane == i1, 1.0, 0.0) + jnp.where(lane == i2, 1.0, 0.0)
        before = (lax.broadcasted_iota(jnp.int32, (tm, tm), 0) > lax.broadcasted_iota(jnp.int32, (tm, tm), 1))
        rank = _mm(jnp.where(before, 1.0, 0.0).astype(BF16), sel.astype(BF16))
        r1 = jnp.sum(jnp.where(lane == i1, rank, 0.0), axis=-1, keepdims=True).astype(jnp.int32)
        r2 = jnp.sum(jnp.where(lane == i2, rank, 0.0), axis=-1, keepdims=True).astype(jnp.int32)
        route_ref[...] = jnp.where(lane == 0, i1 * tm + r1, jnp.where(lane == 1, i2 * tm + r2, 0))
        wts_ref[...] = jnp.where(lane == 0, w1, jnp.where(lane == 1, w2, 0.0))
        cb_ref[...] = run_s[...]
        total = run_s[...] + jnp.sum(sel, axis=0, keepdims=True)
        run_s[...] = total
        cnt_ref[...] = total


def _mix(grp, x, ya, yb, yc, gates, mod, n2, pa, pb, pc, wo, router=None):
    m = grp.m
    moe = router is not None
    in_specs = [grp.rows(D_MODEL), grp.tmajor(SSM_WIDTH), grp.rows(MLA_HEADS * MLA_V), grp.rows(HG_W),
                grp.rows(3 * D_MODEL), grp.mod(2), grp.mod(3), grp.mod(4), _const_spec((1, D_MODEL)),
                _const_spec(pa.shape), _const_spec(pb.shape), _const_spec(pc.shape), _const_spec(wo.shape)]
    args = [x, ya, yb, yc, gates, mod, mod, mod, n2, pa, pb, pc, wo]
    out_specs = [grp.rows(D_MODEL), grp.rows(D_MODEL)]
    out_shape = [jax.ShapeDtypeStruct((m, D_MODEL), F32), jax.ShapeDtypeStruct((m, D_MODEL), BF16)]
    if moe:
        in_specs += [_const_spec(router[0].shape), _const_spec(router[1].shape)]
        args += list(router)
        nb, n_tiles = grp.nb, m // grp.tm
        out_specs += [grp.rows(LANES), grp.rows(LANES), grp.rows(LANES),
                      pl.BlockSpec((None, 1, LANES), lambda b, i: (b * nb + i, 0, 0)), _const_spec((1, LANES))]
        out_shape += [jax.ShapeDtypeStruct((m, LANES), F32), jax.ShapeDtypeStruct((m, LANES), jnp.int32),
                      jax.ShapeDtypeStruct((m, LANES), F32), jax.ShapeDtypeStruct((n_tiles, 1, LANES), F32),
                      jax.ShapeDtypeStruct((1, LANES), F32)]
    return pl.pallas_call(
        functools.partial(_mix_kernel, moe=moe),
        grid=grp.grid, in_specs=in_specs, out_specs=out_specs, out_shape=out_shape,
        scratch_shapes=[pltpu.VMEM((1, LANES), F32)] if moe else [],
        compiler_params=_params(("arbitrary", "arbitrary")),
        name="mix_moe" if moe else "mix",
    )(*args)


def _swiglu_tile(h2_ref, wg_ref, wu_ref, wd_ref):
    h = h2_ref[...]
    g = _mm(h, wg_ref[...].astype(BF16))
    u = _mm(h, wu_ref[...].astype(BF16))
    a = (g * jax.nn.sigmoid(g) * u).astype(BF16)
    return _mm(a, wd_ref[...].astype(BF16))


def _ffn_kernel(h2_ref, x_ref, g2_ref, wg_ref, wu_ref, wd_ref, o_ref, acc_s):
    f = pl.program_id(2)

    @pl.when(f == 0)
    def _():
        acc_s[...] = jnp.zeros(acc_s.shape, F32)

    acc_s[...] += _swiglu_tile(h2_ref, wg_ref, wu_ref, wd_ref)

    @pl.when(f == pl.num_programs(2) - 1)
    def _():
        o_ref[...] = x_ref[...] + g2_ref[...] * acc_s[...]


def _ffn(grp, h2, x, mod, wg, wu, wd, j, tf):
    nf = FF_DIM // tf
    return pl.pallas_call(
        _ffn_kernel,
        grid=grp.grid + (nf,),
        in_specs=[grp.rows(D_MODEL), grp.rows(D_MODEL), grp.mod(5),
                  pl.BlockSpec((None, D_MODEL, tf), lambda b, i, f: (j, 0, f)),
                  pl.BlockSpec((None, D_MODEL, tf), lambda b, i, f: (j, 0, f)),
                  pl.BlockSpec((None, tf, D_MODEL), lambda b, i, f: (j, f, 0))],
        out_specs=grp.rows(D_MODEL),
        out_shape=jax.ShapeDtypeStruct((grp.m, D_MODEL), F32),
        scratch_shapes=[pltpu.VMEM((grp.tm, D_MODEL), F32)],
        compiler_params=_params(("arbitrary", "arbitrary", "arbitrary")),
        name="ffn_dense",
    )(h2, x, mod, wg, wu, wd)


def _moe_kernel(h2_ref, x_ref, g2_ref, gate_ref, wg_ref, wu_ref, wd_ref, o_ref, acc_s):
    e, f = pl.program_id(2), pl.program_id(3)

    @pl.when((e == 0) & (f == 0))
    def _():
        acc_s[...] = jnp.zeros(acc_s.shape, F32)

    gate = gate_ref[...]
    lane = lax.broadcasted_iota(jnp.int32, gate.shape, 1)
    gcol = jnp.sum(jnp.where(lane == e, gate, 0.0), axis=-1, keepdims=True)
    acc_s[...] += gcol * _swiglu_tile(h2_ref, wg_ref, wu_ref, wd_ref)

    @pl.when((e == pl.num_programs(2) - 1) & (f == pl.num_programs(3) - 1))
    def _():
        o_ref[...] = x_ref[...] + g2_ref[...] * acc_s[...]


def _moe(grp, h2, x, mod, gate, wg, wu, wd, j, tf):
    nf = FF_DIM // tf
    return pl.pallas_call(
        _moe_kernel,
        grid=grp.grid + (N_EXPERTS, nf),
        in_specs=[grp.rows(D_MODEL), grp.rows(D_MODEL), grp.mod(5), grp.rows(LANES),
                  pl.BlockSpec((None, None, D_MODEL, tf), lambda b, i, e, f: (j, e, 0, f)),
                  pl.BlockSpec((None, None, D_MODEL, tf), lambda b, i, e, f: (j, e, 0, f)),
                  pl.BlockSpec((None, None, tf, D_MODEL), lambda b, i, e, f: (j, e, f, 0))],
        out_specs=grp.rows(D_MODEL),
        out_shape=jax.ShapeDtypeStruct((grp.m, D_MODEL), F32),
        scratch_shapes=[pltpu.VMEM((grp.tm, D_MODEL), F32)],
        compiler_params=_params(("arbitrary",) * 4),
        name="ffn_moe",
    )(h2, x, mod, gate, wg, wu, wd)


def _route_plan(route, cb, cnt, n_tok):
    cnt8 = cnt[0, :N_EXPERTS].astype(jnp.int32)
    cb8 = cb[:, 0, :N_EXPERTS].astype(jnp.int32)
    p1, p2 = route[:, 0], route[:, 1]
    gp = (cnt8 + MOE_T - 1) // MOE_T * MOE_T
    off_end = jnp.cumsum(gp)
    off = off_end - gp
    start = (off[None, :] + cb8).reshape(-1)
    nbe = (jnp.concatenate([cb8[1:], cnt8[None]], axis=0) - cb8).reshape(-1)
    n_tiles = (2 * n_tok + N_EXPERTS * (MOE_T - 1)) // MOE_T
    idx = jnp.arange(n_tiles, dtype=jnp.int32)
    valid = idx * MOE_T < off_end[-1]
    blk = jnp.where(valid, idx, off_end[-1] // MOE_T - 1)
    te = jnp.sum((blk * MOE_T)[:, None] >= off_end[None, :], axis=1).astype(jnp.int32)
    return p1, p2, start, nbe, te, blk, valid.astype(jnp.int32), n_tiles


def _segment_copies(n, src_of, dst_of, sem):
    out = []
    for k in range(MOE_TM.bit_length()):
        size = 1 << k
        pos = (n >> (k + 1)) << (k + 1)
        out.append((((n >> k) & 1) == 1, pltpu.make_async_copy(src_of(pos, size), dst_of(pos, size), sem)))
    return out


def _moe_gather_kernel(p1_ref, p2_ref, start_ref, nbe_ref, x_ref, xs_init_ref, xs_ref, buf, sem):
    del xs_init_ref
    b = pl.program_id(0)
    half = b % 2

    def tok(i, carry):
        row = x_ref[i]
        buf[half, p1_ref[b * MOE_TM + i]] = row
        buf[half, p2_ref[b * MOE_TM + i]] = row
        return carry

    lax.fori_loop(0, MOE_TM, tok, 0, unroll=8)

    def tile_copies(tile):
        copies = []
        for e in range(N_EXPERTS):
            s = start_ref[tile * N_EXPERTS + e]
            copies += _segment_copies(
                nbe_ref[tile * N_EXPERTS + e],
                lambda pos, size, e=e: buf.at[tile % 2, pl.ds(e * MOE_TM + pos, size)],
                lambda pos, size, s=s: xs_ref.at[pl.ds(s + pos, size)], sem.at[tile % 2])
        return copies

    @pl.when(b > 0)
    def _():
        for cond, cp in tile_copies(b - 1):
            pl.when(cond)(cp.wait)

    mine = tile_copies(b)
    for cond, cp in mine:
        pl.when(cond)(cp.start)

    @pl.when(b == pl.num_programs(0) - 1)
    def _():
        for cond, cp in mine:
            pl.when(cond)(cp.wait)


def _moe_gather(x3, p1, p2, start, nbe, n_rows):
    n_tok = x3.shape[0]
    tile = pl.BlockSpec((MOE_TM, SUBLANES, LANES), lambda b, *_: (b, 0, 0))
    return pl.pallas_call(
        _moe_gather_kernel,
        grid_spec=pltpu.PrefetchScalarGridSpec(
            num_scalar_prefetch=4, grid=(n_tok // MOE_TM,),
            in_specs=[tile, pl.BlockSpec(memory_space=pl.ANY)],
            out_specs=pl.BlockSpec(memory_space=pl.ANY),
            scratch_shapes=[pltpu.VMEM((2, N_EXPERTS * MOE_TM, SUBLANES, LANES), F32),
                            pltpu.SemaphoreType.DMA((2,))]),
        out_shape=jax.ShapeDtypeStruct((n_rows, SUBLANES, LANES), F32),
        input_output_aliases={5: 0},
        compiler_params=_params(("arbitrary",)),
        name="moe_gather",
    )(p1, p2, start, nbe, x3, jnp.zeros((n_rows, SUBLANES, LANES), F32))


def _moe_ffn_kernel(te_ref, blk_ref, valid_ref, x_ref, wg_ref, wu_ref, wd_ref, y_ref, xb_s, acc_s):
    del te_ref, blk_ref
    j, f = pl.program_id(0), pl.program_id(1)

    @pl.when(valid_ref[j] == 1)
    def _():
        @pl.when(f == 0)
        def _():
            xb_s[...] = x_ref[...].astype(BF16)
            acc_s[...] = jnp.zeros(acc_s.shape, F32)

        acc_s[...] += _swiglu_tile(xb_s, wg_ref, wu_ref, wd_ref)

        @pl.when(f == pl.num_programs(1) - 1)
        def _():
            y_ref[...] = acc_s[...]

    @pl.when((valid_ref[j] == 0) & (f == pl.num_programs(1) - 1))
    def _():
        y_ref[...] = jnp.zeros(y_ref.shape, F32)


def _moe_ffn(xs, te, blk, valid, wg, wu, wd, layer, tf):
    n_rows = xs.shape[0]
    nf = FF_DIM // tf
    rows = pl.BlockSpec((MOE_T, D_MODEL), lambda j, f, te, blk, valid: (blk[j], 0))

    def fsel(j, f, valid):
        return jnp.where(valid[j] == 1, f, nf - 1)

    return pl.pallas_call(
        _moe_ffn_kernel,
        grid_spec=pltpu.PrefetchScalarGridSpec(
            num_scalar_prefetch=3, grid=(n_rows // MOE_T, nf),
            in_specs=[rows,
                      pl.BlockSpec((None, None, D_MODEL, tf),
                                   lambda j, f, te, blk, valid: (layer, te[j], 0, fsel(j, f, valid))),
                      pl.BlockSpec((None, None, D_MODEL, tf),
                                   lambda j, f, te, blk, valid: (layer, te[j], 0, fsel(j, f, valid))),
                      pl.BlockSpec((None, None, tf, D_MODEL),
                                   lambda j, f, te, blk, valid: (layer, te[j], fsel(j, f, valid), 0))],
            out_specs=pl.BlockSpec((MOE_T, D_MODEL), lambda j, f, te, blk, valid: (j, 0)),
            scratch_shapes=[pltpu.VMEM((MOE_T, D_MODEL), BF16), pltpu.VMEM((MOE_T, D_MODEL), F32)]),
        out_shape=jax.ShapeDtypeStruct((n_rows, D_MODEL), F32),
        compiler_params=_params(("arbitrary", "arbitrary")),
        name="moe_ffn",
    )(te, blk, valid, xs, wg, wu, wd)


def _moe_combine_kernel(p1_ref, p2_ref, start_ref, nbe_ref, w1_ref, w2_ref, ys_ref, x_ref, g2_ref, o_ref,
                        buf, sem):
    b = pl.program_id(0)
    n_tiles = pl.num_programs(0)
    slot = b % DMA_SLOTS

    def tile_copies(tile):
        copies = []
        for e in range(N_EXPERTS):
            s = start_ref[tile * N_EXPERTS + e]
            copies += _segment_copies(
                nbe_ref[tile * N_EXPERTS + e],
                lambda pos, size, s=s: ys_ref.at[pl.ds(s + pos, size)],
                lambda pos, size, e=e: buf.at[tile % DMA_SLOTS, pl.ds(e * MOE_TM + pos, size)],
                sem.at[tile % DMA_SLOTS])
        return copies

    for first in range(PREFETCH_AHEAD):
        @pl.when((b == 0) & (first < n_tiles))
        def _(first=first):
            for cond, cp in tile_copies(first):
                pl.when(cond)(cp.start)

    for cond, cp in tile_copies(b):
        pl.when(cond)(cp.wait)

    @pl.when(b + PREFETCH_AHEAD < n_tiles)
    def _():
        for cond, cp in tile_copies(b + PREFETCH_AHEAD):
            pl.when(cond)(cp.start)

    g2 = g2_ref[...]

    def tok(i, carry):
        t = b * MOE_TM + i
        o_ref[i] = x_ref[i] + g2 * (w1_ref[t] * buf[slot, p1_ref[t]] + w2_ref[t] * buf[slot, p2_ref[t]])
        return carry

    lax.fori_loop(0, MOE_TM, tok, 0, unroll=8)


def _moe_combine(ys3, x3, g2_3, p1, p2, start, nbe, w1, w2, seq_len):
    n_tok = x3.shape[0]
    per_seq = seq_len // MOE_TM
    tile = pl.BlockSpec((MOE_TM, SUBLANES, LANES), lambda b, *_: (b, 0, 0))
    return pl.pallas_call(
        _moe_combine_kernel,
        grid_spec=pltpu.PrefetchScalarGridSpec(
            num_scalar_prefetch=6, grid=(n_tok // MOE_TM,),
            in_specs=[pl.BlockSpec(memory_space=pl.ANY), tile,
                      pl.BlockSpec((None, SUBLANES, LANES), lambda b, *_: (b // per_seq, 0, 0))],
            out_specs=tile,
            scratch_shapes=[pltpu.VMEM((DMA_SLOTS, N_EXPERTS * MOE_TM, SUBLANES, LANES), F32),
                            pltpu.SemaphoreType.DMA((DMA_SLOTS,))]),
        out_shape=jax.ShapeDtypeStruct(x3.shape, F32),
        compiler_params=_params(("arbitrary",)),
        name="moe_combine",
    )(p1, p2, start, nbe, w1, w2, ys3, x3, g2_3)


def _moe_routed(bz, length, h2, x, mod, route, wts, cb, cnt, wg, wu, wd, layer, tf):
    m = bz * length
    as_tiles = lambda a: a.reshape(a.shape[0], SUBLANES, LANES)
    p1, p2, start, nbe, te, blk, valid, n_tiles = _route_plan(route, cb, cnt, m)
    xs3 = _moe_gather(as_tiles(h2.astype(F32)), p1, p2, start, nbe, n_tiles * MOE_T)
    ys = _moe_ffn(xs3.reshape(n_tiles * MOE_T, D_MODEL), te, blk, valid, wg, wu, wd, layer, tf)
    g2_3 = mod[:, 0, 5 * D_MODEL:6 * D_MODEL].reshape(bz, SUBLANES, LANES)
    out3 = _moe_combine(as_tiles(ys), as_tiles(x), g2_3, p1, p2, start, nbe, wts[:, 0], wts[:, 1], length)
    return out3.reshape(m, D_MODEL)


def _final_kernel(x_ref, g_ref, o_ref):
    o_ref[...] = _rms(x_ref[...], g_ref[...])


def _final_norm(grp, x, g):
    return pl.pallas_call(
        _final_kernel, grid=grp.grid,
        in_specs=[grp.rows(D_MODEL), _const_spec((1, D_MODEL))],
        out_specs=grp.rows(D_MODEL),
        out_shape=jax.ShapeDtypeStruct((grp.m, D_MODEL), F32),
        compiler_params=_params(("arbitrary", "arbitrary")),
        name="final_norm",
    )(x, g)


def _relayout_weights(w):
    bf = lambda a: a.astype(BF16)
    depth = w["w_in"].shape[0]
    wi = w["w_in"]
    o = [0, 256, 640, 896, 928, 1184, 1440, 1696, 1952, 5024]
    kr = wi[:, :, o[3]:o[4]]
    half = MLA_ROPE // 2
    kr_rot = jnp.concatenate([-kr[..., half:], kr[..., :half]], axis=-1)
    zeros = lambda n: jnp.zeros(wi.shape[:2] + (n,), wi.dtype)
    w_in = bf(jnp.concatenate([
        wi[:, :, o[0]:o[1]], wi[:, :, o[2]:o[3]], wi[:, :, o[1]:o[2]], zeros(C_HZ - C_CQ - MLA_Q_LORA),
        wi[:, :, o[4]:o[8]], wi[:, :, o[8]:o[9]], kr, zeros(ROPE_PAD - MLA_ROPE),
        kr_rot, zeros(ROPE_PAD - MLA_ROPE)], axis=-1))
    uq = w["mla_w_uq"].reshape(depth, MLA_Q_LORA, MLA_HEADS, MLA_NOPE + MLA_ROPE)
    nope = uq[..., :MLA_NOPE].reshape(depth, MLA_Q_LORA, MLA_HEADS * MLA_NOPE)
    r1 = uq[..., MLA_NOPE:MLA_NOPE + half]
    r2 = uq[..., MLA_NOPE + half:]
    pad = jnp.zeros(uq.shape[:3] + (ROPE_PAD - MLA_ROPE,), uq.dtype)
    rope = jnp.concatenate([r1, r2, pad], axis=-1).reshape(depth, MLA_Q_LORA, MLA_HEADS * ROPE_PAD)
    rope_rot = jnp.concatenate([-r2, r1, pad], axis=-1).reshape(depth, MLA_Q_LORA, MLA_HEADS * ROPE_PAD)
    wuq = bf(jnp.concatenate([nope, rope, rope_rot], axis=-1))
    eye_h = jnp.eye(MLA_HEADS, dtype=F32)
    uk = w["mla_w_uk"].reshape(depth, MLA_KV_LORA, MLA_HEADS, MLA_NOPE)
    wuk = bf(jnp.einsum("lrhd,hg->lhdgr", uk, eye_h).reshape(
        depth, MLA_HEADS * MLA_NOPE, MLA_HEADS * MLA_KV_LORA))
    wuv = bf(w["mla_w_uv"].reshape(depth, MLA_KV_LORA, MLA_HEADS, MLA_V).transpose(0, 2, 1, 3))
    eye_g = jnp.eye(SSM_GROUPS, dtype=F32)
    b_blk = lambda b: bf(jnp.einsum("lgpc,gh->lgchp", b, eye_g).reshape(depth, SSM_WIDTH, SSM_FLAT))
    c_blk = lambda c: bf(jnp.einsum("lgcp,gh->lgphc", c, eye_g).reshape(depth, SSM_FLAT, SSM_WIDTH))
    n_moe = w["moe_router_w"].shape[0]
    rw = jnp.concatenate([w["moe_router_w"],
                          jnp.zeros((n_moe, D_MODEL, LANES - N_EXPERTS), F32)], axis=-1)
    rb = jnp.concatenate([w["moe_router_b"], jnp.zeros((n_moe, LANES - N_EXPERTS), F32)],
                         axis=-1).reshape(n_moe, 1, LANES)
    return dict(
        w_in=w_in, wuq=wuq, wuk=wuk, wuv=wuv,
        bre=b_blk(w["ssm_b_re"]), bim=b_blk(w["ssm_b_im"]),
        cre=c_blk(w["ssm_c_re"]), cim=c_blk(w["ssm_c_im"]),
        glu_w=bf(w["ssm_glu_w"]), proj_a=bf(w["proj_a"]), proj_b=bf(w["proj_b"]),
        proj_c=bf(w["proj_c"]), w_out=bf(w["w_out"]), rw=rw, rb=rb,
        ffn_g=bf(w["ffn_w_gate"]), ffn_u=bf(w["ffn_w_up"]), ffn_d=bf(w["ffn_w_down"]),
        hg_norm=jnp.tile(w["hgrn_norm_g"], (1, HG_HEADS)))


def _rope_tables(pos):
    half = MLA_ROPE // 2
    freq = ROPE_THETA ** (-jnp.arange(half, dtype=F32) / half)
    ang = pos.astype(F32)[:, None] * freq[None, :]
    cos, sin = jnp.cos(ang), jnp.sin(ang)
    pad = jnp.zeros((pos.shape[0], ROPE_PAD - MLA_ROPE), F32)
    cosk = jnp.concatenate([cos, cos, pad], axis=-1)
    sink = jnp.concatenate([sin, sin, pad], axis=-1)
    return jnp.tile(cosk, (1, MLA_HEADS)), jnp.tile(sink, (1, MLA_HEADS)), cosk, sink


def _state_to_blocks(s):
    eye = jnp.eye(HG_HEADS, dtype=F32)
    return jnp.einsum("bhkv,hg->bhvgk", s, eye).reshape(s.shape[0], HG_W, HG_W)


def _blocks_to_state(st):
    b = st.shape[0]
    return jnp.einsum("bhvhk->bhkv", st.reshape(b, HG_HEADS, HG_K, HG_HEADS, HG_K))


def _trunk(x, mod_all, w, rw, prm, pos0, ssm_re0, ssm_im0, hgrn0, cache_lat, cache_kr, page_table):
    bz, length, _ = x.shape
    m = bz * length
    depth = w["w_in"].shape[0]
    prompt = cache_lat is None
    abr, abi, cr, ci, lbs = prm
    if prompt:
        grp = _Group(bz, length, min(256, length), per_row=False)
        grp_in = _Group(bz, length, min(512, length), per_row=False)
        grp_f = _Group(bz, length, min(1024, length), per_row=False)
        tl = min(128, length)
        ch = min(32, length)
        tb = min(256, length)
    else:
        grp = _Group(bz, length, m, per_row=True)
        grp_in = grp_f = grp
        tl, ch, tb = length, length, length
    pos = pos0 + jnp.arange(length, dtype=jnp.int32)
    tables = _rope_tables(pos)
    if not prompt:
        tables = tuple(jnp.tile(t, (bz, 1)) for t in tables)
    cosq, sinq, cosk, sink = tables
    x = x.reshape(m, D_MODEL)
    lat_l, kr_l, sre_l, sim_l, hg_l = [], [], [], [], []
    row = lambda a: a.reshape(1, -1)
    for l in range(depth):
        mod = mod_all[l]
        mod = mod[:, None, :] if prompt else jnp.repeat(mod, length, axis=0)
        u, hz, gates, ql, qr, lat, latb, kr, krb = _in_proj(
            grp_in, x, mod, row(w["norm1_g"][l]), rw["w_in"][l], row(w["mla_q_norm_g"][l]), rw["wuq"][l],
            rw["wuk"][l], row(w["mla_kv_norm_g"][l]), cosq, sinq, cosk, sink)
        if prompt:
            h0r = jnp.zeros((bz, SSM_FLAT), F32)
            h0i = h0r
            u_tm = u.reshape(length, bz, SSM_WIDTH)
        else:
            h0r = ssm_re0[l].reshape(bz, SSM_FLAT)
            h0i = ssm_im0[l].reshape(bz, SSM_FLAT)
            u_tm = u.reshape(bz, length, SSM_WIDTH).transpose(1, 0, 2)
        ya, htr, hti = _s5(u_tm, h0r, h0i, row(abr[l]), row(abi[l]), row(cr[l]), row(ci[l]),
                           rw["bre"][l], rw["bim"][l], rw["cre"][l], rw["cim"][l], row(w["ssm_d"][l]),
                           rw["glu_w"][l], row(w["ssm_glu_b"][l]), tl)
        if prompt:
            ya = ya.reshape(length, bz * SSM_WIDTH)
        else:
            ya = ya.transpose(1, 0, 2).reshape(m, SSM_WIDTH)
        if prompt:
            yb = _attn_prompt(bz, length, ql, qr, latb, krb, rw["wuv"][l], grp.tm, min(512, length))
        else:
            yb = _attn_sample(l, page_table, ql, qr, latb, krb, cache_lat, cache_kr, rw["wuv"][l], bz, length)
        st0 = jnp.zeros((bz, HG_W, HG_W), F32) if prompt else _state_to_blocks(hgrn0[l])
        yc, st_t = _hgrn(bz, length, hz, st0, row(lbs[l]), row(rw["hg_norm"][l]), ch, tb)
        j = l // 2
        common = (grp, x, ya, yb, yc, gates, mod, row(w["norm2_g"][l]), rw["proj_a"][l], rw["proj_b"][l],
                  rw["proj_c"][l], rw["w_out"][l])
        if l % 2 == 0:
            x, h2 = _mix(*common)
            x = _ffn(grp_f, h2, x, mod, rw["ffn_g"], rw["ffn_u"], rw["ffn_d"], j, FF_DIM // 2)
        else:
            x, h2, gate, route, wts, cb, cnt = _mix(*common, router=(rw["rw"][j], rw["rb"][j]))
            experts = (w["moe_w_gate"], w["moe_w_up"], w["moe_w_down"], j, 256)
            if prompt and grp.tm == MOE_TM:
                x = _moe_routed(bz, length, h2, x, mod, route, wts, cb, cnt, *experts)
            else:
                x = _moe(grp_f, h2, x, mod, gate, *experts)
        lat_l.append(lat.reshape(bz, length, MLA_KV_LORA))
        kr_l.append(kr.reshape(bz, length, MLA_ROPE))
        sre_l.append(htr.reshape(bz, SSM_GROUPS, SSM_STATE))
        sim_l.append(hti.reshape(bz, SSM_GROUPS, SSM_STATE))
        hg_l.append(_blocks_to_state(st_t))
    y = _final_norm(grp, x, row(w["final_norm_g"])).reshape(bz, length, D_MODEL)
    return y, jnp.stack(lat_l), jnp.stack(kr_l), jnp.stack(sre_l), jnp.stack(sim_l), jnp.stack(hg_l)


def kernel(x_prompt, x_sample, c_prompt, c_sample, cache_kv_latent, cache_k_rope, state_ssm_re, state_ssm_im,
           state_hgrn, page_table, ada_w, ada_b, norm1_g, norm2_g, w_in, ssm_a_re, ssm_a_im, ssm_log_dt,
           ssm_b_re, ssm_b_im, ssm_c_re, ssm_c_im, ssm_d, ssm_glu_w, ssm_glu_b, mla_q_norm_g, mla_w_uq,
           mla_kv_norm_g, mla_w_uk, mla_w_uv, hgrn_lb_logits, hgrn_norm_g, proj_a, proj_b, proj_c, w_out,
           ffn_w_gate, ffn_w_up, ffn_w_down, moe_router_w, moe_router_b, moe_w_gate, moe_w_up, moe_w_down,
           final_norm_g):
    w = dict(norm1_g=norm1_g, norm2_g=norm2_g, w_in=w_in, ssm_b_re=ssm_b_re, ssm_b_im=ssm_b_im,
             ssm_c_re=ssm_c_re, ssm_c_im=ssm_c_im, ssm_d=ssm_d, ssm_glu_w=ssm_glu_w, ssm_glu_b=ssm_glu_b,
             mla_q_norm_g=mla_q_norm_g, mla_w_uq=mla_w_uq, mla_kv_norm_g=mla_kv_norm_g, mla_w_uk=mla_w_uk,
             mla_w_uv=mla_w_uv, hgrn_norm_g=hgrn_norm_g, proj_a=proj_a, proj_b=proj_b, proj_c=proj_c,
             w_out=w_out, ffn_w_gate=ffn_w_gate, ffn_w_up=ffn_w_up, ffn_w_down=ffn_w_down,
             moe_router_w=moe_router_w, moe_router_b=moe_router_b, moe_w_gate=moe_w_gate,
             moe_w_up=moe_w_up, moe_w_down=moe_w_down, final_norm_g=final_norm_g)
    rw = _relayout_weights(w)
    prm = _prep_params(ssm_a_re, ssm_a_im, ssm_log_dt, hgrn_lb_logits)
    n_p = c_prompt.shape[0]
    mod_all = _ada_mod(jnp.concatenate([c_prompt, c_sample], axis=0), ada_w, ada_b)
    y_p, lat_p, kr_p, sre_p, sim_p, hg_p = _trunk(
        x_prompt, mod_all[:, :n_p], w, rw, prm, 0, None, None, None, None, None, None)
    past_len = page_table.shape[1] * cache_kv_latent.shape[2]
    y_s, lat_s, kr_s, sre_s, sim_s, hg_s = _trunk(
        x_sample, mod_all[:, n_p:], w, rw, prm, past_len, state_ssm_re, state_ssm_im, state_hgrn,
        cache_kv_latent, jnp.swapaxes(cache_k_rope, 2, 3), page_table)
    return (y_p, y_s, lat_p, kr_p, sre_p, sim_p, hg_p, lat_s, kr_s, sre_s, sim_s, hg_s)
```

```python
import functools

import jax
import jax.numpy as jnp
from jax import lax
from jax.experimental import pallas as pl
from jax.experimental.pallas import tpu as pltpu

F32 = jnp.float32
BF16 = jnp.bfloat16

D_MODEL = 1024
SSM_GROUPS = 16
SSM_GROUP_CH = 16
SSM_WIDTH = 256
SSM_STATE = 64
SSM_FLAT = SSM_GROUPS * SSM_STATE
MLA_HEADS = 8
MLA_NOPE = 64
MLA_ROPE = 32
MLA_V = 64
MLA_Q_LORA = 384
MLA_KV_LORA = 256
MLA_SCALE = (MLA_NOPE + MLA_ROPE) ** -0.5
ROPE_THETA = 10000.0
HG_HEADS = 4
HG_K = 64
HG_W = 256
FF_DIM = 2816
N_EXPERTS = 8
EPS = 1e-6

LANES = 128
SUBLANES = 8
VMEM_LIMIT = 52 * 1024 * 1024
VMEM_LIMIT_IN_PROJ = 58 * 1024 * 1024
PREFETCH_AHEAD = 2
DMA_SLOTS = PREFETCH_AHEAD + 1
MOE_TM = 256
MOE_T = 1024

C_U, C_CKV, C_CQ, C_HZ, C_GATES, C_KR, C_KR_ROT, C_END = 0, 256, 512, 1024, 2048, 5120, 5248, 5376
ROPE_PAD = LANES


def _mm(a, b):
    return jnp.dot(a, b, preferred_element_type=F32)


def _mm_nt(a, b):
    return lax.dot_general(a, b, (((1,), (1,)), ((), ())), preferred_element_type=F32)


def _mm_tn(a, b):
    return lax.dot_general(a, b, (((0,), (0,)), ((), ())), preferred_element_type=F32)


def _mm_f32(a, b):
    return jnp.dot(a, b, preferred_element_type=F32, precision=lax.Precision.HIGHEST)


def _mm_split(a, b):
    ah, bh = a.astype(BF16), b.astype(BF16)
    al = (a - ah.astype(F32)).astype(BF16)
    bl = (b - bh.astype(F32)).astype(BF16)
    return _mm(ah, bh) + (_mm(ah, bl) + _mm(al, bh))


def _rms(x, g):
    return x * lax.rsqrt(jnp.mean(x * x, axis=-1, keepdims=True) + EPS) * g


def _params(sem, vmem_limit=VMEM_LIMIT):
    return pltpu.CompilerParams(dimension_semantics=sem, vmem_limit_bytes=vmem_limit)


def _const_spec(shape, single=False):
    nd = len(shape)
    if single:
        return pl.BlockSpec(shape, lambda *_: (0,) * nd, pipeline_mode=pl.Buffered(1))
    return pl.BlockSpec(shape, lambda *_: (0,) * nd)


class _Group:
    def __init__(self, bz, length, tm, per_row):
        self.bz, self.length, self.tm, self.per_row = bz, length, tm, per_row
        self.m = bz * length
        if per_row:
            self.nb = self.m // tm
            self.grid = (1, self.nb)
        else:
            self.nb = length // tm
            self.grid = (bz, self.nb)

    def rows(self, width):
        nb = self.nb
        return pl.BlockSpec((self.tm, width), lambda b, i, *_: (b * nb + i, 0))

    def mod(self, col):
        if self.per_row:
            return pl.BlockSpec((self.tm, D_MODEL), lambda b, i, *_: (i, col))
        return pl.BlockSpec((None, 1, D_MODEL), lambda b, i, *_: (b, 0, col))

    def pos(self, width):
        return pl.BlockSpec((self.tm, width), lambda b, i, *_: (i, 0))

    def tmajor(self, width):
        if self.per_row:
            return self.rows(width)
        return pl.BlockSpec((self.tm, width), lambda b, i, *_: (i, b))

    def tmajor_shape(self, width):
        return (self.m, width) if self.per_row else (self.length, self.bz * width)


def _param_kernel(ar_ref, ai_ref, ldt_ref, lbl_ref, abr_ref, abi_ref, cr_ref, ci_ref, lbs_ref):
    ar, ai = ar_ref[...], ai_ref[...]
    dt = jnp.exp(ldt_ref[...])
    mag = jnp.exp(dt * ar)
    abr, abi = mag * jnp.cos(dt * ai), mag * jnp.sin(dt * ai)
    den = ar * ar + ai * ai
    abr_ref[...] = abr
    abi_ref[...] = abi
    cr_ref[...] = ((abr - 1.0) * ar + abi * ai) / den
    ci_ref[...] = (abi * ar - (abr - 1.0) * ai) / den
    x = lbl_ref[...]
    e = jnp.exp(x - jnp.max(x, axis=0, keepdims=True))
    p = e / jnp.sum(e, axis=0, keepdims=True)
    rows, acc = [], p[0:1]
    for l in range(x.shape[0]):
        if l:
            acc = acc + p[l:l + 1]
        rows.append(acc - p[0:1])
    lbs_ref[...] = jnp.concatenate(rows, axis=0)


def _prep_params(a_re, a_im, log_dt, lb_logits):
    depth = a_re.shape[0]
    ar = a_re.reshape(depth, SSM_FLAT)
    ai = a_im.reshape(depth, SSM_FLAT)
    ldt = jnp.broadcast_to(log_dt[:, :, None], (depth, SSM_GROUPS, SSM_STATE)).reshape(depth, SSM_FLAT)
    flat = jax.ShapeDtypeStruct((depth, SSM_FLAT), F32)
    return pl.pallas_call(
        _param_kernel,
        out_shape=(flat, flat, flat, flat, jax.ShapeDtypeStruct((depth, HG_W), F32)),
        name="param_prep",
    )(ar, ai, ldt, lb_logits)


def _ada_kernel(c_ref, w_ref, b_ref, o_ref):
    c = c_ref[...]
    s = (c * jax.nn.sigmoid(c)).astype(BF16)
    o_ref[...] = _mm(s, w_ref[...].astype(BF16)) + b_ref[...]


def _ada_mod(c_all, ada_w, ada_b):
    depth, d, n = ada_w.shape
    rows = c_all.shape[0]
    tn = 1536
    return pl.pallas_call(
        _ada_kernel,
        grid=(depth, n // tn),
        in_specs=[pl.BlockSpec((rows, d), lambda l, j: (0, 0)),
                  pl.BlockSpec((None, d, tn), lambda l, j: (l, 0, j)),
                  pl.BlockSpec((None, 1, tn), lambda l, j: (l, 0, j))],
        out_specs=pl.BlockSpec((None, rows, tn), lambda l, j: (l, 0, j)),
        out_shape=jax.ShapeDtypeStruct((depth, rows, n), F32),
        compiler_params=_params(("arbitrary", "arbitrary")),
        name="ada_mod",
    )(c_all, ada_w, ada_b.reshape(depth, 1, n))


def _in_kernel(x_ref, sh_ref, sc_ref, g_ref, w_ref, qg_ref, wuq_ref, wuk_ref, kvg_ref,
               cq_ref, sq_ref, ck_ref, sk_ref,
               u_ref, hz_ref, gates_ref, ql_ref, qr_ref, lat_ref, latb_ref, kr_ref, krb_ref):
    x = x_ref[...]
    h = (_rms(x, g_ref[...]) * (1.0 + sc_ref[...]) + sh_ref[...]).astype(BF16)

    def seg(a, b):
        return _mm(h, w_ref[:, a:b])

    u_ref[...] = seg(C_U, C_CKV)
    hz_ref[...] = seg(C_HZ, C_GATES)
    gates_ref[...] = seg(C_GATES, C_KR)
    cqn = _rms(seg(C_CQ, C_CQ + MLA_Q_LORA), qg_ref[...]).astype(BF16)
    q = _mm(cqn, wuq_ref[...])
    n_nope = MLA_HEADS * MLA_NOPE
    n_rope = MLA_HEADS * ROPE_PAD
    ql = _mm(q[:, :n_nope].astype(BF16), wuk_ref[...]).astype(BF16)
    qr = (q[:, n_nope:n_nope + n_rope] * cq_ref[...] + q[:, n_nope + n_rope:] * sq_ref[...]).astype(BF16)
    for hd in range(MLA_HEADS):
        ql_ref[hd] = ql[:, hd * MLA_KV_LORA:(hd + 1) * MLA_KV_LORA]
        qr_ref[hd] = qr[:, hd * ROPE_PAD:(hd + 1) * ROPE_PAD]
    lat = _rms(seg(C_CKV, C_CQ), kvg_ref[...])
    lat_ref[...] = lat
    latb_ref[...] = lat.astype(BF16)
    kr = seg(C_KR, C_KR_ROT) * ck_ref[...] + seg(C_KR_ROT, C_END) * sk_ref[...]
    kr_ref[...] = kr[:, :MLA_ROPE]
    krb_ref[...] = kr.astype(BF16)


def _in_proj(grp, x, mod, g1, w_in, qg, wuq, wuk, kvg, cosq, sinq, cosk, sink):
    m, tm = grp.m, grp.tm
    n_rope = MLA_HEADS * ROPE_PAD
    nb = grp.nb

    def heads(width):
        return pl.BlockSpec((MLA_HEADS, tm, width), lambda b, i: (0, b * nb + i, 0))

    outs = [
        (grp.tmajor_shape(SSM_WIDTH), F32, grp.tmajor(SSM_WIDTH)),
        ((m, 4 * HG_W), F32, grp.rows(4 * HG_W)),
        ((m, 3 * D_MODEL), F32, grp.rows(3 * D_MODEL)),
        ((MLA_HEADS, m, MLA_KV_LORA), BF16, heads(MLA_KV_LORA)),
        ((MLA_HEADS, m, ROPE_PAD), BF16, heads(ROPE_PAD)),
        ((m, MLA_KV_LORA), F32, grp.rows(MLA_KV_LORA)),
        ((m, MLA_KV_LORA), BF16, grp.rows(MLA_KV_LORA)),
        ((m, MLA_ROPE), F32, grp.rows(MLA_ROPE)),
        ((m, ROPE_PAD), BF16, grp.rows(ROPE_PAD)),
    ]
    return pl.pallas_call(
        _in_kernel,
        grid=grp.grid,
        in_specs=[grp.rows(D_MODEL), grp.mod(0), grp.mod(1), _const_spec((1, D_MODEL)),
                  _const_spec(w_in.shape, True), _const_spec((1, MLA_Q_LORA)), _const_spec(wuq.shape, True),
                  _const_spec(wuk.shape, True), _const_spec((1, MLA_KV_LORA)),
                  grp.pos(n_rope), grp.pos(n_rope), grp.pos(ROPE_PAD), grp.pos(ROPE_PAD)],
        out_specs=[o[2] for o in outs],
        out_shape=[jax.ShapeDtypeStruct(o[0], o[1]) for o in outs],
        compiler_params=_params(("arbitrary", "arbitrary"), VMEM_LIMIT_IN_PROJ),
        name="in_proj",
    )(x, mod, mod, g1, w_in, qg, wuq, wuk, kvg, cosq, sinq, cosk, sink)


def _s5_kernel(u_ref, h0r_ref, h0i_ref, abr_ref, abi_ref, cr_ref, ci_ref, bre_ref, bim_ref,
               cre_ref, cim_ref, d_ref, gw_ref, gb_ref,
               y_ref, htr_ref, hti_ref, hr_s, hi_s, xr_s, xi_s, *, tl, bz):
    i = pl.program_id(0)

    @pl.when(i == 0)
    def _():
        hr_s[...] = h0r_ref[...]
        hi_s[...] = h0i_ref[...]

    u = u_ref[...].reshape(tl * bz, SSM_WIDTH)
    ub = u.astype(BF16)
    bur, bui = _mm(ub, bre_ref[...]), _mm(ub, bim_ref[...])
    cr, ci = cr_ref[...], ci_ref[...]
    xr_s[...] = cr * bur - ci * bui
    xi_s[...] = cr * bui + ci * bur
    abr, abi = abr_ref[...], abi_ref[...]

    def step(t, carry):
        hr, hi = carry
        r0 = pl.multiple_of(t * bz, bz)
        nr = abr * hr - abi * hi + xr_s[pl.ds(r0, bz), :]
        ni = abr * hi + abi * hr + xi_s[pl.ds(r0, bz), :]
        xr_s[pl.ds(r0, bz), :] = nr
        xi_s[pl.ds(r0, bz), :] = ni
        return nr, ni

    hr, hi = lax.fori_loop(0, tl, step, (hr_s[...], hi_s[...]))
    hr_s[...] = hr
    hi_s[...] = hi
    htr_ref[...] = hr
    hti_ref[...] = hi
    y = (_mm(xr_s[...].astype(BF16), cre_ref[...]) - _mm(xi_s[...].astype(BF16), cim_ref[...])
         + d_ref[...] * u)
    y = jax.nn.gelu(y)
    y = y * jax.nn.sigmoid(_mm(y.astype(BF16), gw_ref[...]) + gb_ref[...])
    y_ref[...] = y.reshape(tl, bz, SSM_WIDTH)


def _s5(u_tm, h0r, h0i, abr, abi, cr, ci, bre, bim, cre, cim, d, gw, gb, tl):
    length, bz, _ = u_tm.shape
    row = _const_spec((1, SSM_FLAT))
    st = _const_spec((bz, SSM_FLAT))
    blk = pl.BlockSpec((tl, bz, SSM_WIDTH), lambda i: (i, 0, 0))
    return pl.pallas_call(
        functools.partial(_s5_kernel, tl=tl, bz=bz),
        grid=(length // tl,),
        in_specs=[blk, st, st, row, row, row, row, _const_spec(bre.shape), _const_spec(bim.shape),
                  _const_spec(cre.shape), _const_spec(cim.shape), _const_spec((1, SSM_WIDTH)),
                  _const_spec(gw.shape), _const_spec((1, SSM_WIDTH))],
        out_specs=[blk, st, st],
        out_shape=[jax.ShapeDtypeStruct(u_tm.shape, F32), jax.ShapeDtypeStruct((bz, SSM_FLAT), F32),
                   jax.ShapeDtypeStruct((bz, SSM_FLAT), F32)],
        scratch_shapes=[pltpu.VMEM((bz, SSM_FLAT), F32), pltpu.VMEM((bz, SSM_FLAT), F32),
                        pltpu.VMEM((tl * bz, SSM_FLAT), F32), pltpu.VMEM((tl * bz, SSM_FLAT), F32)],
        compiler_params=_params(("arbitrary",)),
        name="s5_scan",
    )(u_tm, h0r, h0i, abr, abi, cr, ci, bre, bim, cre, cim, d, gw, gb)


def _softmax_step(s, keys, m_ref, l_ref, acc_ref):
    m_old = m_ref[...]
    m_new = jnp.maximum(m_old, jnp.max(s, axis=-1, keepdims=True))
    alpha = jnp.exp(m_old - m_new)
    p = jnp.exp(s - jnp.tile(m_new, (1, s.shape[-1] // LANES)))
    l_ref[...] = alpha * l_ref[...] + jnp.sum(p, axis=-1, keepdims=True)
    acc_ref[...] = (acc_ref[...] * jnp.tile(alpha, (1, acc_ref.shape[-1] // LANES))
                    + _mm(p.astype(BF16), keys))
    m_ref[...] = m_new


def _attn_prompt_kernel(ql_ref, qr_ref, lat_ref, kr_ref, wuv_ref, y_ref, m_s, l_s, acc_s, *, tq, tk):
    i = pl.program_id(1)
    rows = MLA_HEADS * tq
    m_s[...] = jnp.full(m_s.shape, -jnp.inf, F32)
    l_s[...] = jnp.zeros(l_s.shape, F32)
    acc_s[...] = jnp.zeros(acc_s.shape, F32)

    def block(k0, width, diagonal):
        kl = lat_ref[pl.ds(k0, width), :]
        kr = kr_ref[pl.ds(k0, width), :]
        s = (_mm_nt(ql_ref[...].reshape(rows, MLA_KV_LORA), kl)
             + _mm_nt(qr_ref[...].reshape(rows, ROPE_PAD), kr)) * MLA_SCALE
        if diagonal:
            q_t = lax.broadcasted_iota(jnp.int32, (rows, width), 0) % tq
            k_t = lax.broadcasted_iota(jnp.int32, (rows, width), 1)
            s = jnp.where(k_t <= q_t, s, -jnp.inf)
        _softmax_step(s, kl, m_s, l_s, acc_s)

    first = i * tq
    n_wide = first // tk

    def wide(j, carry):
        block(pl.multiple_of(j * tk, tk), tk, False)
        return carry

    def narrow(j, carry):
        block(pl.multiple_of(n_wide * tk + j * tq, tq), tq, False)
        return carry

    lax.fori_loop(0, n_wide, wide, 0)
    lax.fori_loop(0, (first - n_wide * tk) // tq, narrow, 0)
    block(pl.multiple_of(first, tq), tq, True)
    ctx = acc_s[...] / jnp.tile(l_s[...], (1, MLA_KV_LORA // LANES))
    ys = [_mm(ctx[h * tq:(h + 1) * tq, :].astype(BF16), wuv_ref[h]) for h in range(MLA_HEADS)]
    y_ref[...] = jnp.concatenate(ys, axis=-1).astype(BF16)


def _attn_prompt(grp_bz, length, ql, qr, latb, krb, wuv, tq, tk):
    nb = length // tq
    rows = MLA_HEADS * tq
    heads = lambda w: pl.BlockSpec((MLA_HEADS, tq, w), lambda b, i: (0, b * nb + i, 0))
    return pl.pallas_call(
        functools.partial(_attn_prompt_kernel, tq=tq, tk=tk),
        grid=(grp_bz, nb),
        in_specs=[heads(MLA_KV_LORA), heads(ROPE_PAD),
                  pl.BlockSpec((None, length, MLA_KV_LORA), lambda b, i: (b, 0, 0)),
                  pl.BlockSpec((None, length, ROPE_PAD), lambda b, i: (b, 0, 0)),
                  _const_spec(wuv.shape)],
        out_specs=pl.BlockSpec((tq, MLA_HEADS * MLA_V), lambda b, i: (b * nb + i, 0)),
        out_shape=jax.ShapeDtypeStruct((grp_bz * length, MLA_HEADS * MLA_V), BF16),
        scratch_shapes=[pltpu.VMEM((rows, LANES), F32), pltpu.VMEM((rows, LANES), F32),
                        pltpu.VMEM((rows, MLA_KV_LORA), F32)],
        compiler_params=_params(("arbitrary", "arbitrary")),
        name="attn_prompt",
    )(ql, qr, latb.reshape(grp_bz, length, MLA_KV_LORA), krb.reshape(grp_bz, length, ROPE_PAD), wuv)


def _attn_sample_kernel(pt_ref, ql_ref, qr_ref, nl_ref, nk_ref, wuv_ref, lat_hbm, krt_hbm, y_ref,
                        lat_buf, krt_buf, sem, m_s, l_s, acc_s, *, layer, pg, ns, lq):
    b, s_idx = pl.program_id(0), pl.program_id(1)
    n_steps = pl.num_programs(1)
    total = pl.num_programs(0) * n_steps
    step = b * n_steps + s_idx
    slot = step % DMA_SLOTS

    def page_copies(at_step):
        seq, st, sl = at_step // n_steps, at_step % n_steps, at_step % DMA_SLOTS
        cps = []
        for i in range(pg):
            page = pt_ref[seq, st * pg + i]
            cps.append(pltpu.make_async_copy(lat_hbm.at[layer, page], lat_buf.at[sl, i], sem.at[sl]))
            cps.append(pltpu.make_async_copy(krt_hbm.at[layer, page], krt_buf.at[sl, i], sem.at[sl]))
        return cps

    for first in range(PREFETCH_AHEAD):
        @pl.when((step == 0) & (first < total))
        def _(first=first):
            for cp in page_copies(first):
                cp.start()

    for cp in page_copies(step):
        cp.wait()

    @pl.when(step + PREFETCH_AHEAD < total)
    def _():
        for cp in page_copies(step + PREFETCH_AHEAD):
            cp.start()

    @pl.when(s_idx == 0)
    def _():
        m_s[...] = jnp.full(m_s.shape, -jnp.inf, F32)
        l_s[...] = jnp.zeros(l_s.shape, F32)
        acc_s[...] = jnp.zeros(acc_s.shape, F32)

    q_l, q_r = ql_ref[...], qr_ref[...]
    per = pg // ns
    for st in range(ns):
        keys = jnp.concatenate([lat_buf[slot, i].astype(BF16) for i in range(st * per, (st + 1) * per)], axis=0)
        krt = jnp.concatenate([krt_buf[slot, i].astype(BF16) for i in range(st * per, (st + 1) * per)], axis=1)
        s = (_mm_nt(q_l, keys) + _mm(q_r, krt)) * MLA_SCALE
        _softmax_step(s, keys, m_s.at[st], l_s.at[st], acc_s.at[st])

    @pl.when(s_idx == n_steps - 1)
    def _():
        rows = MLA_HEADS * lq
        nl, nk = nl_ref[...], nk_ref[...]
        s = (_mm_nt(q_l, nl) + _mm_nt(q_r, nk)) * MLA_SCALE
        q_t = lax.broadcasted_iota(jnp.int32, (rows, lq), 0) % lq
        k_t = lax.broadcasted_iota(jnp.int32, (rows, lq), 1)
        s = jnp.where(k_t <= q_t, s, -jnp.inf)
        m_old = m_s[0]
        m_new = jnp.maximum(m_old, jnp.max(s, axis=-1, keepdims=True))
        for st in range(1, ns):
            m_new = jnp.maximum(m_new, m_s[st])
        p = jnp.exp(s - m_new[:, :1])
        l_tot = jnp.sum(p, axis=-1, keepdims=True)
        acc = _mm(p.astype(BF16), nl)
        for st in range(ns):
            a = jnp.exp(m_s[st] - m_new)
            l_tot = l_tot + a * l_s[st]
            acc = acc + jnp.tile(a, (1, MLA_KV_LORA // LANES)) * acc_s[st]
        ctx = acc / jnp.tile(l_tot, (1, MLA_KV_LORA // LANES))
        ys = [_mm(ctx[h * lq:(h + 1) * lq, :].astype(BF16), wuv_ref[h]) for h in range(MLA_HEADS)]
        y_ref[...] = jnp.concatenate(ys, axis=-1).astype(BF16)


def _attn_sample(layer, page_table, ql, qr, latb, krb, cache_lat, cache_krt, wuv, bz, lq):
    n_pages = page_table.shape[1]
    pg = min(16, n_pages)
    ns = 2 if pg % 2 == 0 else 1
    page = cache_lat.shape[2]
    rows = MLA_HEADS * lq
    qlh = ql.reshape(MLA_HEADS, bz, lq, MLA_KV_LORA).transpose(1, 0, 2, 3).reshape(bz, rows, MLA_KV_LORA)
    qrh = qr[:, :, :MLA_ROPE].reshape(MLA_HEADS, bz, lq, MLA_ROPE).transpose(1, 0, 2, 3).reshape(
        bz, rows, MLA_ROPE)

    def seq(n, w):
        return pl.BlockSpec((None, n, w), lambda b, s, pt: (b, 0, 0))

    in_specs = [seq(rows, MLA_KV_LORA), seq(rows, MLA_ROPE), seq(lq, MLA_KV_LORA), seq(lq, MLA_ROPE),
                pl.BlockSpec(wuv.shape, lambda b, s, pt: (0, 0, 0)),
                pl.BlockSpec(memory_space=pl.ANY), pl.BlockSpec(memory_space=pl.ANY)]
    y = pl.pallas_call(
        functools.partial(_attn_sample_kernel, layer=layer, pg=pg, ns=ns, lq=lq),
        grid_spec=pltpu.PrefetchScalarGridSpec(
            num_scalar_prefetch=1,
            grid=(bz, n_pages // pg),
            in_specs=in_specs,
            out_specs=pl.BlockSpec((None, lq, MLA_HEADS * MLA_V), lambda b, s, pt: (b, 0, 0)),
            scratch_shapes=[pltpu.VMEM((DMA_SLOTS, pg, page, MLA_KV_LORA), F32),
                            pltpu.VMEM((DMA_SLOTS, pg, MLA_ROPE, page), F32),
                            pltpu.SemaphoreType.DMA((DMA_SLOTS,)),
                            pltpu.VMEM((ns, rows, LANES), F32), pltpu.VMEM((ns, rows, LANES), F32),
                            pltpu.VMEM((ns, rows, MLA_KV_LORA), F32)]),
        out_shape=jax.ShapeDtypeStruct((bz, lq, MLA_HEADS * MLA_V), BF16),
        compiler_params=_params(("arbitrary", "arbitrary")),
        name="attn_sample",
    )(page_table, qlh, qrh, latb.reshape(bz, lq, MLA_KV_LORA),
      krb[:, :MLA_ROPE].reshape(bz, lq, MLA_ROPE), wuv, cache_lat, cache_krt)
    return y.reshape(bz * lq, MLA_HEADS * MLA_V)


def _hgrn_kernel(hz_ref, st0_ref, lb_ref, ng_ref, y_ref, stt_ref, st_s, w_s, *, ch, tb, nseq):
    j = pl.program_id(1)

    @pl.when(j == 0)
    def _():
        st_s[...] = st0_ref[...]

    lb = lb_ref[...]
    log_lb, log1m_lb, one_m_lb = jnp.log(lb), jnp.log1p(-lb), 1.0 - lb
    tri = (lax.broadcasted_iota(jnp.int32, (ch, ch), 0)
           >= lax.broadcasted_iota(jnp.int32, (ch, ch), 1)).astype(F32)
    same_head = (lax.broadcasted_iota(jnp.int32, (HG_W, HG_W), 0) // HG_K
                 == lax.broadcasted_iota(jnp.int32, (HG_W, HG_W), 1) // HG_K)
    head_ones = same_head.astype(F32)
    head_ones_b = same_head.astype(BF16)
    ng = ng_ref[...]

    def chunk(c, carry):
        r0 = pl.multiple_of(c * ch, ch)
        for sq in range(nseq):
            chunk_of(sq, r0)
        return carry

    def chunk_of(sq, r0):
        q = hz_ref[sq, pl.ds(r0, ch), 0:HG_W]
        fp = hz_ref[sq, pl.ds(r0, ch), HG_W:2 * HG_W]
        v = hz_ref[sq, pl.ds(r0, ch), 2 * HG_W:3 * HG_W]
        g = hz_ref[sq, pl.ds(r0, ch), 3 * HG_W:4 * HG_W]
        logf = jnp.logaddexp(log_lb, log1m_lb + jax.nn.log_sigmoid(fp))
        k = one_m_lb * jax.nn.sigmoid(-fp)
        b = _mm_f32(tri, logf)
        st = st_s[sq]
        o = _mm_nt((q * jnp.exp(b)).astype(BF16), st.astype(BF16))
        starts, r = [], 0
        for s in range(ch):
            t0 = (s // SUBLANES) * SUBLANES
            t_idx = t0 + lax.broadcasted_iota(jnp.int32, (ch - t0, HG_W), 0)
            e = jnp.exp(jnp.where(t_idx >= s, b[t0:] - b[s:s + 1, :], -jnp.inf))
            w_s[sq, r:r + ch - t0, :] = q[t0:] * k[s:s + 1, :] * e
            starts.append(r)
            r += ch - t0
        att = _mm(w_s[sq].astype(BF16), head_ones_b)
        tiles = [o[t0:t0 + SUBLANES] for t0 in range(0, ch, SUBLANES)]
        for s in range(ch):
            for kt in range(s // SUBLANES, ch // SUBLANES):
                a0 = starts[s] + (kt - s // SUBLANES) * SUBLANES
                tiles[kt] = tiles[kt] + att[a0:a0 + SUBLANES, :] * v[s:s + 1, :]
        o = jnp.concatenate(tiles, axis=0)
        bl = b[ch - 1:ch, :]
        kd = k * jnp.exp(bl - b)
        upd = _mm_tn(v.astype(BF16), kd.astype(BF16))
        st_s[sq] = st * jnp.exp(bl) + jnp.where(same_head, upd, 0.0)
        ms = _mm_f32(o * o, head_ones) * (1.0 / HG_K)
        on = o * lax.rsqrt(ms + EPS) * ng
        y_ref[sq, pl.ds(r0, ch), :] = on * (g * jax.nn.sigmoid(g))

    lax.fori_loop(0, tb // ch, chunk, 0)

    @pl.when(j == pl.num_programs(1) - 1)
    def _():
        stt_ref[...] = st_s[...]


def _hgrn(bz, length, hz, st0, lb, ng, ch, tb):
    nb = length // tb
    nseq = 2 if bz % 2 == 0 else 1
    st_spec = pl.BlockSpec((nseq, HG_W, HG_W), lambda b, j: (b, 0, 0))
    pair_rows = sum(ch - s // SUBLANES * SUBLANES for s in range(ch))
    y, st_t = pl.pallas_call(
        functools.partial(_hgrn_kernel, ch=ch, tb=tb, nseq=nseq),
        grid=(bz // nseq, nb),
        in_specs=[pl.BlockSpec((nseq, tb, 4 * HG_W), lambda b, j: (b, j, 0)), st_spec,
                  _const_spec((1, HG_W)), _const_spec((1, HG_W))],
        out_specs=[pl.BlockSpec((nseq, tb, HG_W), lambda b, j: (b, j, 0)), st_spec],
        out_shape=[jax.ShapeDtypeStruct((bz, length, HG_W), F32),
                   jax.ShapeDtypeStruct((bz, HG_W, HG_W), F32)],
        scratch_shapes=[pltpu.VMEM((nseq, HG_W, HG_W), F32), pltpu.VMEM((nseq, pair_rows, HG_W), F32)],
        compiler_params=_params(("arbitrary", "arbitrary")),
        name="hgrn2",
    )(hz.reshape(bz, length, 4 * HG_W), st0, lb, ng)
    return y.reshape(bz * length, HG_W), st_t


def _mix_kernel(*refs, moe):
    if moe:
        (x_ref, ya_ref, yb_ref, yc_ref, gates_ref, g1_ref, sh2_ref, sc2_ref, n2_ref,
         pa_ref, pb_ref, pc_ref, wo_ref, rw_ref, rb_ref,
         xo_ref, h2_ref, gate_ref, route_ref, wts_ref, cb_ref, cnt_ref, run_s) = refs
    else:
        (x_ref, ya_ref, yb_ref, yc_ref, gates_ref, g1_ref, sh2_ref, sc2_ref, n2_ref,
         pa_ref, pb_ref, pc_ref, wo_ref, xo_ref, h2_ref) = refs
    d = D_MODEL
    mix = (jax.nn.sigmoid(gates_ref[:, 0:d]) * _mm(ya_ref[...].astype(BF16), pa_ref[...])
           + jax.nn.sigmoid(gates_ref[:, d:2 * d]) * _mm(yb_ref[...], pb_ref[...])
           + jax.nn.sigmoid(gates_ref[:, 2 * d:3 * d]) * _mm(yc_ref[...].astype(BF16), pc_ref[...]))
    xo = x_ref[...] + g1_ref[...] * _mm(mix.astype(BF16), wo_ref[...])
    xo_ref[...] = xo
    h2 = _rms(xo, n2_ref[...]) * (1.0 + sc2_ref[...]) + sh2_ref[...]
    h2_ref[...] = h2.astype(BF16)
    if moe:
        logits = _mm_split(h2, rw_ref[...]) + rb_ref[...]
        lane = lax.broadcasted_iota(jnp.int32, logits.shape, 1)
        lg = jnp.where(lane < N_EXPERTS, logits, -jnp.inf)
        m1 = jnp.max(lg, axis=-1, keepdims=True)
        i1 = jnp.min(jnp.where(lg == m1, lane, LANES), axis=-1, keepdims=True)
        lg2 = jnp.where(lane == i1, -jnp.inf, lg)
        m2 = jnp.max(lg2, axis=-1, keepdims=True)
        i2 = jnp.min(jnp.where(lg2 == m2, lane, LANES), axis=-1, keepdims=True)
        e2 = jnp.exp(m2 - m1)
        den = 1.0 + e2
        w1, w2 = 1.0 / den, e2 / den
        gate_ref[...] = jnp.where(lane == i1, w1, 0.0) + jnp.where(lane == i2, w2, 0.0)
        @pl.when((pl.program_id(0) == 0) & (pl.program_id(1) == 0))
        def _():
            run_s[...] = jnp.zeros(run_s.shape, F32)

        tm = logits.shape[0]
        sel = jnp.where(lane == i1, 1.0, 0.0) + jnp.where(lane == i2, 1.0, 0.0)
        before = (lax.broadcasted_iota(jnp.int32, (tm, tm), 0) > lax.broadcasted_iota(jnp.int32, (tm, tm), 1))
        rank = _mm(jnp.where(before, 1.0, 0.0).astype(BF16), sel.astype(BF16))
        r1 = jnp.sum(jnp.where(lane == i1, rank, 0.0), axis=-1, keepdims=True).astype(jnp.int32)
        r2 = jnp.sum(jnp.where(lane == i2, rank, 0.0), axis=-1, keepdims=True).astype(jnp.int32)
        route_ref[...] = jnp.where(lane == 0, i1 * tm + r1, jnp.where(lane == 1, i2 * tm + r2, 0))
        wts_ref[...] = jnp.where(lane == 0, w1, jnp.where(lane == 1, w2, 0.0))
        cb_ref[...] = run_s[...]
        total = run_s[...] + jnp.sum(sel, axis=0, keepdims=True)
        run_s[...] = total
        cnt_ref[...] = total


def _mix(grp, x, ya, yb, yc, gates, mod, n2, pa, pb, pc, wo, router=None):
    m = grp.m
    moe = router is not None
    in_specs = [grp.rows(D_MODEL), grp.tmajor(SSM_WIDTH), grp.rows(MLA_HEADS * MLA_V), grp.rows(HG_W),
                grp.rows(3 * D_MODEL), grp.mod(2), grp.mod(3), grp.mod(4), _const_spec((1, D_MODEL)),
                _const_spec(pa.shape), _const_spec(pb.shape), _const_spec(pc.shape), _const_spec(wo.shape)]
    args = [x, ya, yb, yc, gates, mod, mod, mod, n2, pa, pb, pc, wo]
    out_specs = [grp.rows(D_MODEL), grp.rows(D_MODEL)]
    out_shape = [jax.ShapeDtypeStruct((m, D_MODEL), F32), jax.ShapeDtypeStruct((m, D_MODEL), BF16)]
    if moe:
        in_specs += [_const_spec(router[0].shape), _const_spec(router[1].shape)]
        args += list(router)
        nb, n_tiles = grp.nb, m // grp.tm
        out_specs += [grp.rows(LANES), grp.rows(LANES), grp.rows(LANES),
                      pl.BlockSpec((None, 1, LANES), lambda b, i: (b * nb + i, 0, 0)), _const_spec((1, LANES))]
        out_shape += [jax.ShapeDtypeStruct((m, LANES), F32), jax.ShapeDtypeStruct((m, LANES), jnp.int32),
                      jax.ShapeDtypeStruct((m, LANES), F32), jax.ShapeDtypeStruct((n_tiles, 1, LANES), F32),
                      jax.ShapeDtypeStruct((1, LANES), F32)]
    return pl.pallas_call(
        functools.partial(_mix_kernel, moe=moe),
        grid=grp.grid, in_specs=in_specs, out_specs=out_specs, out_shape=out_shape,
        scratch_shapes=[pltpu.VMEM((1, LANES), F32)] if moe else [],
        compiler_params=_params(("arbitrary", "arbitrary")),
        name="mix_moe" if moe else "mix",
    )(*args)


def _swiglu_tile(h2_ref, wg_ref, wu_ref, wd_ref):
    h = h2_ref[...]
    g = _mm(h, wg_ref[...].astype(BF16))
    u = _mm(h, wu_ref[...].astype(BF16))
    a = (g * jax.nn.sigmoid(g) * u).astype(BF16)
    return _mm(a, wd_ref[...].astype(BF16))


def _ffn_kernel(h2_ref, x_ref, g2_ref, wg_ref, wu_ref, wd_ref, o_ref, acc_s):
    f = pl.program_id(2)

    @pl.when(f == 0)
    def _():
        acc_s[...] = jnp.zeros(acc_s.shape, F32)

    acc_s[...] += _swiglu_tile(h2_ref, wg_ref, wu_ref, wd_ref)

    @pl.when(f == pl.num_programs(2) - 1)
    def _():
        o_ref[...] = x_ref[...] + g2_ref[...] * acc_s[...]


def _ffn(grp, h2, x, mod, wg, wu, wd, j, tf):
    nf = FF_DIM // tf
    return pl.pallas_call(
        _ffn_kernel,
        grid=grp.grid + (nf,),
        in_specs=[grp.rows(D_MODEL), grp.rows(D_MODEL), grp.mod(5),
                  pl.BlockSpec((None, D_MODEL, tf), lambda b, i, f: (j, 0, f)),
                  pl.BlockSpec((None, D_MODEL, tf), lambda b, i, f: (j, 0, f)),
                  pl.BlockSpec((None, tf, D_MODEL), lambda b, i, f: (j, f, 0))],
        out_specs=grp.rows(D_MODEL),
        out_shape=jax.ShapeDtypeStruct((grp.m, D_MODEL), F32),
        scratch_shapes=[pltpu.VMEM((grp.tm, D_MODEL), F32)],
        compiler_params=_params(("arbitrary", "arbitrary", "arbitrary")),
        name="ffn_dense",
    )(h2, x, mod, wg, wu, wd)


def _moe_kernel(h2_ref, x_ref, g2_ref, gate_ref, wg_ref, wu_ref, wd_ref, o_ref, acc_s):
    e, f = pl.program_id(2), pl.program_id(3)

    @pl.when((e == 0) & (f == 0))
    def _():
        acc_s[...] = jnp.zeros(acc_s.shape, F32)

    gate = gate_ref[...]
    lane = lax.broadcasted_iota(jnp.int32, gate.shape, 1)
    gcol = jnp.sum(jnp.where(lane == e, gate, 0.0), axis=-1, keepdims=True)
    acc_s[...] += gcol * _swiglu_tile(h2_ref, wg_ref, wu_ref, wd_ref)

    @pl.when((e == pl.num_programs(2) - 1) & (f == pl.num_programs(3) - 1))
    def _():
        o_ref[...] = x_ref[...] + g2_ref[...] * acc_s[...]


def _moe(grp, h2, x, mod, gate, wg, wu, wd, j, tf):
    nf = FF_DIM // tf
    return pl.pallas_call(
        _moe_kernel,
        grid=grp.grid + (N_EXPERTS, nf),
        in_specs=[grp.rows(D_MODEL), grp.rows(D_MODEL), grp.mod(5), grp.rows(LANES),
                  pl.BlockSpec((None, None, D_MODEL, tf), lambda b, i, e, f: (j, e, 0, f)),
                  pl.BlockSpec((None, None, D_MODEL, tf), lambda b, i, e, f: (j, e, 0, f)),
                  pl.BlockSpec((None, None, tf, D_MODEL), lambda b, i, e, f: (j, e, f, 0))],
        out_specs=grp.rows(D_MODEL),
        out_shape=jax.ShapeDtypeStruct((grp.m, D_MODEL), F32),
        scratch_shapes=[pltpu.VMEM((grp.tm, D_MODEL), F32)],
        compiler_params=_params(("arbitrary",) * 4),
        name="ffn_moe",
    )(h2, x, mod, gate, wg, wu, wd)


def _route_plan(route, cb, cnt, n_tok):
    cnt8 = cnt[0, :N_EXPERTS].astype(jnp.int32)
    cb8 = cb[:, 0, :N_EXPERTS].astype(jnp.int32)
    p1, p2 = route[:, 0], route[:, 1]
    gp = (cnt8 + MOE_T - 1) // MOE_T * MOE_T
    off_end = jnp.cumsum(gp)
    off = off_end - gp
    start = (off[None, :] + cb8).reshape(-1)
    nbe = (jnp.concatenate([cb8[1:], cnt8[None]], axis=0) - cb8).reshape(-1)
    n_tiles = (2 * n_tok + N_EXPERTS * (MOE_T - 1)) // MOE_T
    idx = jnp.arange(n_tiles, dtype=jnp.int32)
    valid = idx * MOE_T < off_end[-1]
    blk = jnp.where(valid, idx, off_end[-1] // MOE_T - 1)
    te = jnp.sum((blk * MOE_T)[:, None] >= off_end[None, :], axis=1).astype(jnp.int32)
    fill = jnp.concatenate([off + cnt8, gp - cnt8, off_end[-1:] // MOE_T])
    return p1, p2, start, nbe, fill, te, blk, valid.astype(jnp.int32), n_tiles


def _segment_copies(n, src_of, dst_of, sem, max_rows=MOE_TM):
    out = []
    for k in range(max_rows.bit_length()):
        size = 1 << k
        pos = (n >> (k + 1)) << (k + 1)
        out.append((((n >> k) & 1) == 1, pltpu.make_async_copy(src_of(pos, size), dst_of(pos, size), sem)))
    return out


def _moe_gather_kernel(p1_ref, p2_ref, start_ref, nbe_ref, fill_ref, x_ref, xs_ref, buf, zeros_s, sem, zsem,
                       *, n_tiles):
    b = pl.program_id(0)
    half = b % 2

    def tok(i, carry):
        row = x_ref[i]
        buf[half, p1_ref[b * MOE_TM + i]] = row
        buf[half, p2_ref[b * MOE_TM + i]] = row
        return carry

    lax.fori_loop(0, MOE_TM, tok, 0, unroll=8)

    def tile_copies(tile):
        copies = []
        for e in range(N_EXPERTS):
            s = start_ref[tile * N_EXPERTS + e]
            copies += _segment_copies(
                nbe_ref[tile * N_EXPERTS + e],
                lambda pos, size, e=e: buf.at[tile % 2, pl.ds(e * MOE_TM + pos, size)],
                lambda pos, size, s=s: xs_ref.at[pl.ds(s + pos, size)], sem.at[tile % 2])
        return copies

    @pl.when(b > 0)
    def _():
        for cond, cp in tile_copies(b - 1):
            pl.when(cond)(cp.wait)

    mine = tile_copies(b)
    for cond, cp in mine:
        pl.when(cond)(cp.start)

    @pl.when(b == pl.num_programs(0) - 1)
    def _():
        for cond, cp in mine:
            pl.when(cond)(cp.wait)
        zeros_s[...] = jnp.zeros(zeros_s.shape, F32)
        fills = []
        for e in range(N_EXPERTS):
            s = fill_ref[e]
            fills += _segment_copies(
                fill_ref[N_EXPERTS + e],
                lambda pos, size: zeros_s.at[pl.ds(0, size)],
                lambda pos, size, s=s: xs_ref.at[pl.ds(s + pos, size)], zsem, max_rows=MOE_T - 1)
        for j in range(n_tiles):
            fills.append((j >= fill_ref[2 * N_EXPERTS],
                          pltpu.make_async_copy(zeros_s, xs_ref.at[pl.ds(j * MOE_T, MOE_T)], zsem)))
        for cond, cp in fills:
            pl.when(cond)(cp.start)
        for cond, cp in fills:
            pl.when(cond)(cp.wait)


def _moe_gather(x3, p1, p2, start, nbe, fill, n_tiles):
    n_tok = x3.shape[0]
    tile = pl.BlockSpec((MOE_TM, SUBLANES, LANES), lambda b, *_: (b, 0, 0))
    return pl.pallas_call(
        functools.partial(_moe_gather_kernel, n_tiles=n_tiles),
        grid_spec=pltpu.PrefetchScalarGridSpec(
            num_scalar_prefetch=5, grid=(n_tok // MOE_TM,),
            in_specs=[tile],
            out_specs=pl.BlockSpec(memory_space=pl.ANY),
            scratch_shapes=[pltpu.VMEM((2, N_EXPERTS * MOE_TM, SUBLANES, LANES), F32),
                            pltpu.VMEM((MOE_T, SUBLANES, LANES), F32),
                            pltpu.SemaphoreType.DMA((2,)), pltpu.SemaphoreType.DMA]),
        out_shape=jax.ShapeDtypeStruct((n_tiles * MOE_T, SUBLANES, LANES), F32),
        compiler_params=_params(("arbitrary",)),
        name="moe_gather",
    )(p1, p2, start, nbe, fill, x3)


def _moe_ffn_kernel(te_ref, blk_ref, valid_ref, x_ref, wg_ref, wu_ref, wd_ref, y_ref, xb_s, acc_s):
    del te_ref, blk_ref
    j, f = pl.program_id(0), pl.program_id(1)

    @pl.when(valid_ref[j] == 1)
    def _():
        @pl.when(f == 0)
        def _():
            xb_s[...] = x_ref[...].astype(BF16)
            acc_s[...] = jnp.zeros(acc_s.shape, F32)

        acc_s[...] += _swiglu_tile(xb_s, wg_ref, wu_ref, wd_ref)

        @pl.when(f == pl.num_programs(1) - 1)
        def _():
            y_ref[...] = acc_s[...]

    @pl.when((valid_ref[j] == 0) & (f == pl.num_programs(1) - 1))
    def _():
        y_ref[...] = jnp.zeros(y_ref.shape, F32)


def _moe_ffn(xs, te, blk, valid, wg, wu, wd, layer, tf):
    n_rows = xs.shape[0]
    nf = FF_DIM // tf
    rows = pl.BlockSpec((MOE_T, D_MODEL), lambda j, f, te, blk, valid: (blk[j], 0))

    def fsel(j, f, valid):
        return jnp.where(valid[j] == 1, f, nf - 1)

    return pl.pallas_call(
        _moe_ffn_kernel,
        grid_spec=pltpu.PrefetchScalarGridSpec(
            num_scalar_prefetch=3, grid=(n_rows // MOE_T, nf),
            in_specs=[rows,
                      pl.BlockSpec((None, None, D_MODEL, tf),
                                   lambda j, f, te, blk, valid: (layer, te[j], 0, fsel(j, f, valid))),
                      pl.BlockSpec((None, None, D_MODEL, tf),
                                   lambda j, f, te, blk, valid: (layer, te[j], 0, fsel(j, f, valid))),
                      pl.BlockSpec((None, None, tf, D_MODEL),
                                   lambda j, f, te, blk, valid: (layer, te[j], fsel(j, f, valid), 0))],
            out_specs=pl.BlockSpec((MOE_T, D_MODEL), lambda j, f, te, blk, valid: (j, 0)),
            scratch_shapes=[pltpu.VMEM((MOE_T, D_MODEL), BF16), pltpu.VMEM((MOE_T, D_MODEL), F32)]),
        out_shape=jax.ShapeDtypeStruct((n_rows, D_MODEL), F32),
        compiler_params=_params(("arbitrary", "arbitrary")),
        name="moe_ffn",
    )(te, blk, valid, xs, wg, wu, wd)


def _moe_combine_kernel(p1_ref, p2_ref, start_ref, nbe_ref, w1_ref, w2_ref, ys_ref, x_ref, g2_ref, o_ref,
                        buf, sem):
    b = pl.program_id(0)
    n_tiles = pl.num_programs(0)
    slot = b % DMA_SLOTS

    def tile_copies(tile):
        copies = []
        for e in range(N_EXPERTS):
            s = start_ref[tile * N_EXPERTS + e]
            copies += _segment_copies(
                nbe_ref[tile * N_EXPERTS + e],
                lambda pos, size, s=s: ys_ref.at[pl.ds(s + pos, size)],
                lambda pos, size, e=e: buf.at[tile % DMA_SLOTS, pl.ds(e * MOE_TM + pos, size)],
                sem.at[tile % DMA_SLOTS])
        return copies

    for first in range(PREFETCH_AHEAD):
        @pl.when((b == 0) & (first < n_tiles))
        def _(first=first):
            for cond, cp in tile_copies(first):
                pl.when(cond)(cp.start)

    for cond, cp in tile_copies(b):
        pl.when(cond)(cp.wait)

    @pl.when(b + PREFETCH_AHEAD < n_tiles)
    def _():
        for cond, cp in tile_copies(b + PREFETCH_AHEAD):
            pl.when(cond)(cp.start)

    g2 = g2_ref[...]

    def tok(i, carry):
        t = b * MOE_TM + i
        o_ref[i] = x_ref[i] + g2 * (w1_ref[t] * buf[slot, p1_ref[t]] + w2_ref[t] * buf[slot, p2_ref[t]])
        return carry

    lax.fori_loop(0, MOE_TM, tok, 0, unroll=8)


def _moe_combine(ys3, x3, g2_3, p1, p2, start, nbe, w1, w2, seq_len):
    n_tok = x3.shape[0]
    per_seq = seq_len // MOE_TM
    tile = pl.BlockSpec((MOE_TM, SUBLANES, LANES), lambda b, *_: (b, 0, 0))
    return pl.pallas_call(
        _moe_combine_kernel,
        grid_spec=pltpu.PrefetchScalarGridSpec(
            num_scalar_prefetch=6, grid=(n_tok // MOE_TM,),
            in_specs=[pl.BlockSpec(memory_space=pl.ANY), tile,
                      pl.BlockSpec((None, SUBLANES, LANES), lambda b, *_: (b // per_seq, 0, 0))],
            out_specs=tile,
            scratch_shapes=[pltpu.VMEM((DMA_SLOTS, N_EXPERTS * MOE_TM, SUBLANES, LANES), F32),
                            pltpu.SemaphoreType.DMA((DMA_SLOTS,))]),
        out_shape=jax.ShapeDtypeStruct(x3.shape, F32),
        compiler_params=_params(("arbitrary",)),
        name="moe_combine",
    )(p1, p2, start, nbe, w1, w2, ys3, x3, g2_3)


def _moe_routed(bz, length, h2, x, mod, route, wts, cb, cnt, wg, wu, wd, layer, tf):
    m = bz * length
    as_tiles = lambda a: a.reshape(a.shape[0], SUBLANES, LANES)
    p1, p2, start, nbe, fill, te, blk, valid, n_tiles = _route_plan(route, cb, cnt, m)
    xs3 = _moe_gather(as_tiles(h2.astype(F32)), p1, p2, start, nbe, fill, n_tiles)
    ys = _moe_ffn(xs3.reshape(n_tiles * MOE_T, D_MODEL), te, blk, valid, wg, wu, wd, layer, tf)
    g2_3 = mod[:, 0, 5 * D_MODEL:6 * D_MODEL].reshape(bz, SUBLANES, LANES)
    out3 = _moe_combine(as_tiles(ys), as_tiles(x), g2_3, p1, p2, start, nbe, wts[:, 0], wts[:, 1], length)
    return out3.reshape(m, D_MODEL)


def _final_kernel(x_ref, g_ref, o_ref):
    o_ref[...] = _rms(x_ref[...], g_ref[...])


def _final_norm(grp, x, g):
    return pl.pallas_call(
        _final_kernel, grid=grp.grid,
        in_specs=[grp.rows(D_MODEL), _const_spec((1, D_MODEL))],
        out_specs=grp.rows(D_MODEL),
        out_shape=jax.ShapeDtypeStruct((grp.m, D_MODEL), F32),
        compiler_params=_params(("arbitrary", "arbitrary")),
        name="final_norm",
    )(x, g)


def _relayout_weights(w):
    bf = lambda a: a.astype(BF16)
    depth = w["w_in"].shape[0]
    wi = w["w_in"]
    o = [0, 256, 640, 896, 928, 1184, 1440, 1696, 1952, 5024]
    kr = wi[:, :, o[3]:o[4]]
    half = MLA_ROPE // 2
    kr_rot = jnp.concatenate([-kr[..., half:], kr[..., :half]], axis=-1)
    zeros = lambda n: jnp.zeros(wi.shape[:2] + (n,), wi.dtype)
    w_in = bf(jnp.concatenate([
        wi[:, :, o[0]:o[1]], wi[:, :, o[2]:o[3]], wi[:, :, o[1]:o[2]], zeros(C_HZ - C_CQ - MLA_Q_LORA),
        wi[:, :, o[4]:o[8]], wi[:, :, o[8]:o[9]], kr, zeros(ROPE_PAD - MLA_ROPE),
        kr_rot, zeros(ROPE_PAD - MLA_ROPE)], axis=-1))
    uq = w["mla_w_uq"].reshape(depth, MLA_Q_LORA, MLA_HEADS, MLA_NOPE + MLA_ROPE)
    nope = uq[..., :MLA_NOPE].reshape(depth, MLA_Q_LORA, MLA_HEADS * MLA_NOPE)
    r1 = uq[..., MLA_NOPE:MLA_NOPE + half]
    r2 = uq[..., MLA_NOPE + half:]
    pad = jnp.zeros(uq.shape[:3] + (ROPE_PAD - MLA_ROPE,), uq.dtype)
    rope = jnp.concatenate([r1, r2, pad], axis=-1).reshape(depth, MLA_Q_LORA, MLA_HEADS * ROPE_PAD)
    rope_rot = jnp.concatenate([-r2, r1, pad], axis=-1).reshape(depth, MLA_Q_LORA, MLA_HEADS * ROPE_PAD)
    wuq = bf(jnp.concatenate([nope, rope, rope_rot], axis=-1))
    eye_h = jnp.eye(MLA_HEADS, dtype=F32)
    uk = w["mla_w_uk"].reshape(depth, MLA_KV_LORA, MLA_HEADS, MLA_NOPE)
    wuk = bf(jnp.einsum("lrhd,hg->lhdgr", uk, eye_h).reshape(
        depth, MLA_HEADS * MLA_NOPE, MLA_HEADS * MLA_KV_LORA))
    wuv = bf(w["mla_w_uv"].reshape(depth, MLA_KV_LORA, MLA_HEADS, MLA_V).transpose(0, 2, 1, 3))
    eye_g = jnp.eye(SSM_GROUPS, dtype=F32)
    b_blk = lambda b: bf(jnp.einsum("lgpc,gh->lgchp", b, eye_g).reshape(depth, SSM_WIDTH, SSM_FLAT))
    c_blk = lambda c: bf(jnp.einsum("lgcp,gh->lgphc", c, eye_g).reshape(depth, SSM_FLAT, SSM_WIDTH))
    n_moe = w["moe_router_w"].shape[0]
    rw = jnp.concatenate([w["moe_router_w"],
                          jnp.zeros((n_moe, D_MODEL, LANES - N_EXPERTS), F32)], axis=-1)
    rb = jnp.concatenate([w["moe_router_b"], jnp.zeros((n_moe, LANES - N_EXPERTS), F32)],
                         axis=-1).reshape(n_moe, 1, LANES)
    return dict(
        w_in=w_in, wuq=wuq, wuk=wuk, wuv=wuv,
        bre=b_blk(w["ssm_b_re"]), bim=b_blk(w["ssm_b_im"]),
        cre=c_blk(w["ssm_c_re"]), cim=c_blk(w["ssm_c_im"]),
        glu_w=bf(w["ssm_glu_w"]), proj_a=bf(w["proj_a"]), proj_b=bf(w["proj_b"]),
        proj_c=bf(w["proj_c"]), w_out=bf(w["w_out"]), rw=rw, rb=rb,
        ffn_g=bf(w["ffn_w_gate"]), ffn_u=bf(w["ffn_w_up"]), ffn_d=bf(w["ffn_w_down"]),
        hg_norm=jnp.tile(w["hgrn_norm_g"], (1, HG_HEADS)))


def _rope_tables(pos):
    half = MLA_ROPE // 2
    freq = ROPE_THETA ** (-jnp.arange(half, dtype=F32) / half)
    ang = pos.astype(F32)[:, None] * freq[None, :]
    cos, sin = jnp.cos(ang), jnp.sin(ang)
    pad = jnp.zeros((pos.shape[0], ROPE_PAD - MLA_ROPE), F32)
    cosk = jnp.concatenate([cos, cos, pad], axis=-1)
    sink = jnp.concatenate([sin, sin, pad], axis=-1)
    return jnp.tile(cosk, (1, MLA_HEADS)), jnp.tile(sink, (1, MLA_HEADS)), cosk, sink


def _state_to_blocks(s):
    eye = jnp.eye(HG_HEADS, dtype=F32)
    return jnp.einsum("bhkv,hg->bhvgk", s, eye).reshape(s.shape[0], HG_W, HG_W)


def _blocks_to_state(st):
    b = st.shape[0]
    return jnp.einsum("bhvhk->bhkv", st.reshape(b, HG_HEADS, HG_K, HG_HEADS, HG_K))


def _trunk(x, mod_all, w, rw, prm, pos0, ssm_re0, ssm_im0, hgrn0, cache_lat, cache_kr, page_table):
    bz, length, _ = x.shape
    m = bz * length
    depth = w["w_in"].shape[0]
    prompt = cache_lat is None
    abr, abi, cr, ci, lbs = prm
    if prompt:
        grp = _Group(bz, length, min(256, length), per_row=False)
        grp_in = _Group(bz, length, min(512, length), per_row=False)
        grp_f = _Group(bz, length, min(1024, length), per_row=False)
        tl = min(128, length)
        ch = min(32, length)
        tb = min(256, length)
    else:
        grp = _Group(bz, length, m, per_row=True)
        grp_in = grp_f = grp
        tl, ch, tb = length, length, length
    pos = pos0 + jnp.arange(length, dtype=jnp.int32)
    tables = _rope_tables(pos)
    if not prompt:
        tables = tuple(jnp.tile(t, (bz, 1)) for t in tables)
    cosq, sinq, cosk, sink = tables
    x = x.reshape(m, D_MODEL)
    lat_l, kr_l, sre_l, sim_l, hg_l = [], [], [], [], []
    row = lambda a: a.reshape(1, -1)
    for l in range(depth):
        mod = mod_all[l]
        mod = mod[:, None, :] if prompt else jnp.repeat(mod, length, axis=0)
        u, hz, gates, ql, qr, lat, latb, kr, krb = _in_proj(
            grp_in, x, mod, row(w["norm1_g"][l]), rw["w_in"][l], row(w["mla_q_norm_g"][l]), rw["wuq"][l],
            rw["wuk"][l], row(w["mla_kv_norm_g"][l]), cosq, sinq, cosk, sink)
        if prompt:
            h0r = jnp.zeros((bz, SSM_FLAT), F32)
            h0i = h0r
            u_tm = u.reshape(length, bz, SSM_WIDTH)
        else:
            h0r = ssm_re0[l].reshape(bz, SSM_FLAT)
            h0i = ssm_im0[l].reshape(bz, SSM_FLAT)
            u_tm = u.reshape(bz, length, SSM_WIDTH).transpose(1, 0, 2)
        ya, htr, hti = _s5(u_tm, h0r, h0i, row(abr[l]), row(abi[l]), row(cr[l]), row(ci[l]),
                           rw["bre"][l], rw["bim"][l], rw["cre"][l], rw["cim"][l], row(w["ssm_d"][l]),
                           rw["glu_w"][l], row(w["ssm_glu_b"][l]), tl)
        if prompt:
            ya = ya.reshape(length, bz * SSM_WIDTH)
        else:
            ya = ya.transpose(1, 0, 2).reshape(m, SSM_WIDTH)
        if prompt:
            yb = _attn_prompt(bz, length, ql, qr, latb, krb, rw["wuv"][l], grp.tm, min(512, length))
        else:
            yb = _attn_sample(l, page_table, ql, qr, latb, krb, cache_lat, cache_kr, rw["wuv"][l], bz, length)
        st0 = jnp.zeros((bz, HG_W, HG_W), F32) if prompt else _state_to_blocks(hgrn0[l])
        yc, st_t = _hgrn(bz, length, hz, st0, row(lbs[l]), row(rw["hg_norm"][l]), ch, tb)
        j = l // 2
        common = (grp, x, ya, yb, yc, gates, mod, row(w["norm2_g"][l]), rw["proj_a"][l], rw["proj_b"][l],
                  rw["proj_c"][l], rw["w_out"][l])
        if l % 2 == 0:
            x, h2 = _mix(*common)
            x = _ffn(grp_f, h2, x, mod, rw["ffn_g"], rw["ffn_u"], rw["ffn_d"], j, FF_DIM // 2)
        else:
            x, h2, gate, route, wts, cb, cnt = _mix(*common, router=(rw["rw"][j], rw["rb"][j]))
            experts = (w["moe_w_gate"], w["moe_w_up"], w["moe_w_down"], j, 256)
            if prompt and grp.tm == MOE_TM:
                x = _moe_routed(bz, length, h2, x, mod, route, wts, cb, cnt, *experts)
            else:
                x = _moe(grp_f, h2, x, mod, gate, *experts)
        lat_l.append(lat.reshape(bz, length, MLA_KV_LORA))
        kr_l.append(kr.reshape(bz, length, MLA_ROPE))
        sre_l.append(htr.reshape(bz, SSM_GROUPS, SSM_STATE))
        sim_l.append(hti.reshape(bz, SSM_GROUPS, SSM_STATE))
        hg_l.append(_blocks_to_state(st_t))
    y = _final_norm(grp, x, row(w["final_norm_g"])).reshape(bz, length, D_MODEL)
    return y, jnp.stack(lat_l), jnp.stack(kr_l), jnp.stack(sre_l), jnp.stack(sim_l), jnp.stack(hg_l)


def kernel(x_prompt, x_sample, c_prompt, c_sample, cache_kv_latent, cache_k_rope, state_ssm_re, state_ssm_im,
           state_hgrn, page_table, ada_w, ada_b, norm1_g, norm2_g, w_in, ssm_a_re, ssm_a_im, ssm_log_dt,
           ssm_b_re, ssm_b_im, ssm_c_re, ssm_c_im, ssm_d, ssm_glu_w, ssm_glu_b, mla_q_norm_g, mla_w_uq,
           mla_kv_norm_g, mla_w_uk, mla_w_uv, hgrn_lb_logits, hgrn_norm_g, proj_a, proj_b, proj_c, w_out,
           ffn_w_gate, ffn_w_up, ffn_w_down, moe_router_w, moe_router_b, moe_w_gate, moe_w_up, moe_w_down,
           final_norm_g):
    w = dict(norm1_g=norm1_g, norm2_g=norm2_g, w_in=w_in, ssm_b_re=ssm_b_re, ssm_b_im=ssm_b_im,
             ssm_c_re=ssm_c_re, ssm_c_im=ssm_c_im, ssm_d=ssm_d, ssm_glu_w=ssm_glu_w, ssm_glu_b=ssm_glu_b,
             mla_q_norm_g=mla_q_norm_g, mla_w_uq=mla_w_uq, mla_kv_norm_g=mla_kv_norm_g, mla_w_uk=mla_w_uk,
             mla_w_uv=mla_w_uv, hgrn_norm_g=hgrn_norm_g, proj_a=proj_a, proj_b=proj_b, proj_c=proj_c,
             w_out=w_out, ffn_w_gate=ffn_w_gate, ffn_w_up=ffn_w_up, ffn_w_down=ffn_w_down,
             moe_router_w=moe_router_w, moe_router_b=moe_router_b, moe_w_gate=moe_w_gate,
             moe_w_up=moe_w_up, moe_w_down=moe_w_down, final_norm_g=final_norm_g)
    rw = _relayout_weights(w)
    prm = _prep_params(ssm_a_re, ssm_a_im, ssm_log_dt, hgrn_lb_logits)
    n_p = c_prompt.shape[0]
    mod_all = _ada_mod(jnp.concatenate([c_prompt, c_sample], axis=0), ada_w, ada_b)
    y_p, lat_p, kr_p, sre_p, sim_p, hg_p = _trunk(
        x_prompt, mod_all[:, :n_p], w, rw, prm, 0, None, None, None, None, None, None)
    past_len = page_table.shape[1] * cache_kv_latent.shape[2]
    y_s, lat_s, kr_s, sre_s, sim_s, hg_s = _trunk(
        x_sample, mod_all[:, n_p:], w, rw, prm, past_len, state_ssm_re, state_ssm_im, state_hgrn,
        cache_kv_latent, jnp.swapaxes(cache_k_rope, 2, 3), page_table)
    return (y_p, y_s, lat_p, kr_p, sre_p, sim_p, hg_p, lat_s, kr_s, sre_s, sim_s, hg_s)
```

```python
import functools

import jax
import jax.numpy as jnp
from jax import lax
from jax.experimental import pallas as pl
from jax.experimental.pallas import tpu as pltpu

F32 = jnp.float32
BF16 = jnp.bfloat16

D_MODEL = 1024
SSM_GROUPS = 16
SSM_GROUP_CH = 16
SSM_WIDTH = 256
SSM_STATE = 64
SSM_FLAT = SSM_GROUPS * SSM_STATE
MLA_HEADS = 8
MLA_NOPE = 64
MLA_ROPE = 32
MLA_V = 64
MLA_Q_LORA = 384
MLA_KV_LORA = 256
MLA_SCALE = (MLA_NOPE + MLA_ROPE) ** -0.5
ROPE_THETA = 10000.0
HG_HEADS = 4
HG_K = 64
HG_W = 256
FF_DIM = 2816
N_EXPERTS = 8
EPS = 1e-6

LANES = 128
SUBLANES = 8
VMEM_LIMIT = 52 * 1024 * 1024
VMEM_LIMIT_IN_PROJ = 58 * 1024 * 1024
HEADS_PER_STRIP = 2
PREFETCH_AHEAD = 2
DMA_SLOTS = PREFETCH_AHEAD + 1
MOE_TM = 256
MOE_T = 512

C_U, C_CKV, C_CQ, C_HZ, C_GATES, C_KR, C_KR_ROT, C_END = 0, 256, 512, 1024, 2048, 5120, 5248, 5376
ROPE_PAD = LANES


def _mm(a, b):
    return jnp.dot(a, b, preferred_element_type=F32)


def _mm_nt(a, b):
    return lax.dot_general(a, b, (((1,), (1,)), ((), ())), preferred_element_type=F32)


def _mm_tn(a, b):
    return lax.dot_general(a, b, (((0,), (0,)), ((), ())), preferred_element_type=F32)


def _mm_f32(a, b):
    return jnp.dot(a, b, preferred_element_type=F32, precision=lax.Precision.HIGHEST)


def _mm_split(a, b):
    ah, bh = a.astype(BF16), b.astype(BF16)
    al = (a - ah.astype(F32)).astype(BF16)
    bl = (b - bh.astype(F32)).astype(BF16)
    return _mm(ah, bh) + (_mm(ah, bl) + _mm(al, bh))


def _rms(x, g):
    return x * lax.rsqrt(jnp.mean(x * x, axis=-1, keepdims=True) + EPS) * g


def _params(sem, vmem_limit=VMEM_LIMIT):
    return pltpu.CompilerParams(dimension_semantics=sem, vmem_limit_bytes=vmem_limit)


def _const_spec(shape, single=False):
    nd = len(shape)
    if single:
        return pl.BlockSpec(shape, lambda *_: (0,) * nd, pipeline_mode=pl.Buffered(1))
    return pl.BlockSpec(shape, lambda *_: (0,) * nd)


class _Group:
    def __init__(self, bz, length, tm, per_row):
        self.bz, self.length, self.tm, self.per_row = bz, length, tm, per_row
        self.m = bz * length
        if per_row:
            self.nb = self.m // tm
            self.grid = (1, self.nb)
        else:
            self.nb = length // tm
            self.grid = (bz, self.nb)

    def rows(self, width):
        nb = self.nb
        return pl.BlockSpec((self.tm, width), lambda b, i, *_: (b * nb + i, 0))

    def mod(self, col):
        if self.per_row:
            return pl.BlockSpec((self.tm, D_MODEL), lambda b, i, *_: (i, col))
        return pl.BlockSpec((None, 1, D_MODEL), lambda b, i, *_: (b, 0, col))

    def pos(self, width):
        return pl.BlockSpec((self.tm, width), lambda b, i, *_: (i, 0))

    def tmajor(self, width):
        if self.per_row:
            return self.rows(width)
        return pl.BlockSpec((self.tm, width), lambda b, i, *_: (i, b))

    def tmajor_shape(self, width):
        return (self.m, width) if self.per_row else (self.length, self.bz * width)


def _param_kernel(ar_ref, ai_ref, ldt_ref, lbl_ref, abr_ref, abi_ref, cr_ref, ci_ref, lbs_ref):
    ar, ai = ar_ref[...], ai_ref[...]
    dt = jnp.exp(ldt_ref[...])
    mag = jnp.exp(dt * ar)
    abr, abi = mag * jnp.cos(dt * ai), mag * jnp.sin(dt * ai)
    den = ar * ar + ai * ai
    abr_ref[...] = abr
    abi_ref[...] = abi
    cr_ref[...] = ((abr - 1.0) * ar + abi * ai) / den
    ci_ref[...] = (abi * ar - (abr - 1.0) * ai) / den
    x = lbl_ref[...]
    e = jnp.exp(x - jnp.max(x, axis=0, keepdims=True))
    p = e / jnp.sum(e, axis=0, keepdims=True)
    rows, acc = [], p[0:1]
    for l in range(x.shape[0]):
        if l:
            acc = acc + p[l:l + 1]
        rows.append(acc - p[0:1])
    lbs_ref[...] = jnp.concatenate(rows, axis=0)


def _prep_params(a_re, a_im, log_dt, lb_logits):
    depth = a_re.shape[0]
    ar = a_re.reshape(depth, SSM_FLAT)
    ai = a_im.reshape(depth, SSM_FLAT)
    ldt = jnp.broadcast_to(log_dt[:, :, None], (depth, SSM_GROUPS, SSM_STATE)).reshape(depth, SSM_FLAT)
    flat = jax.ShapeDtypeStruct((depth, SSM_FLAT), F32)
    return pl.pallas_call(
        _param_kernel,
        out_shape=(flat, flat, flat, flat, jax.ShapeDtypeStruct((depth, HG_W), F32)),
        name="param_prep",
    )(ar, ai, ldt, lb_logits)


def _ada_kernel(c_ref, w_ref, b_ref, o_ref):
    c = c_ref[...]
    s = (c * jax.nn.sigmoid(c)).astype(BF16)
    o_ref[...] = _mm(s, w_ref[...].astype(BF16)) + b_ref[...]


def _ada_mod(c_all, ada_w, ada_b):
    depth, d, n = ada_w.shape
    rows = c_all.shape[0]
    tn = 1536
    return pl.pallas_call(
        _ada_kernel,
        grid=(depth, n // tn),
        in_specs=[pl.BlockSpec((rows, d), lambda l, j: (0, 0)),
                  pl.BlockSpec((None, d, tn), lambda l, j: (l, 0, j)),
                  pl.BlockSpec((None, 1, tn), lambda l, j: (l, 0, j))],
        out_specs=pl.BlockSpec((None, rows, tn), lambda l, j: (l, 0, j)),
        out_shape=jax.ShapeDtypeStruct((depth, rows, n), F32),
        compiler_params=_params(("arbitrary", "arbitrary")),
        name="ada_mod",
    )(c_all, ada_w, ada_b.reshape(depth, 1, n))


def _in_kernel(x_ref, sh_ref, sc_ref, g_ref, w_ref, qg_ref, wuq_ref, wuk_ref, kvg_ref,
               cq_ref, sq_ref, ck_ref, sk_ref,
               u_ref, hz_ref, gates_ref, ql_ref, qr_ref, lat_ref, latb_ref, kr_ref, krb_ref):
    x = x_ref[...]
    h = (_rms(x, g_ref[...]) * (1.0 + sc_ref[...]) + sh_ref[...]).astype(BF16)

    def seg(a, b):
        return _mm(h, w_ref[:, a:b])

    u_ref[...] = seg(C_U, C_CKV)
    hz_ref[...] = seg(C_HZ, C_GATES)
    gates_ref[...] = seg(C_GATES, C_KR)
    cqn = _rms(seg(C_CQ, C_CQ + MLA_Q_LORA), qg_ref[...]).astype(BF16)
    q = _mm(cqn, wuq_ref[...])
    n_nope = MLA_HEADS * MLA_NOPE
    n_rope = MLA_HEADS * ROPE_PAD
    ql = _mm(q[:, :n_nope].astype(BF16), wuk_ref[...]).astype(BF16)
    qr = (q[:, n_nope:n_nope + n_rope] * cq_ref[...] + q[:, n_nope + n_rope:] * sq_ref[...]).astype(BF16)
    for hd in range(MLA_HEADS):
        ql_ref[hd] = ql[:, hd * MLA_KV_LORA:(hd + 1) * MLA_KV_LORA]
        qr_ref[hd] = qr[:, hd * ROPE_PAD:(hd + 1) * ROPE_PAD]
    lat = _rms(seg(C_CKV, C_CQ), kvg_ref[...])
    lat_ref[...] = lat
    latb_ref[...] = lat.astype(BF16)
    kr = seg(C_KR, C_KR_ROT) * ck_ref[...] + seg(C_KR_ROT, C_END) * sk_ref[...]
    kr_ref[...] = kr[:, :MLA_ROPE]
    krb_ref[...] = kr.astype(BF16)


def _in_proj(grp, x, mod, g1, w_in, qg, wuq, wuk, kvg, cosq, sinq, cosk, sink):
    m, tm = grp.m, grp.tm
    n_rope = MLA_HEADS * ROPE_PAD
    nb = grp.nb

    def heads(width):
        return pl.BlockSpec((MLA_HEADS, tm, width), lambda b, i: (0, b * nb + i, 0))

    outs = [
        (grp.tmajor_shape(SSM_WIDTH), F32, grp.tmajor(SSM_WIDTH)),
        ((m, 4 * HG_W), F32, grp.rows(4 * HG_W)),
        ((m, 3 * D_MODEL), F32, grp.rows(3 * D_MODEL)),
        ((MLA_HEADS, m, MLA_KV_LORA), BF16, heads(MLA_KV_LORA)),
        ((MLA_HEADS, m, ROPE_PAD), BF16, heads(ROPE_PAD)),
        ((m, MLA_KV_LORA), F32, grp.rows(MLA_KV_LORA)),
        ((m, MLA_KV_LORA), BF16, grp.rows(MLA_KV_LORA)),
        ((m, MLA_ROPE), F32, grp.rows(MLA_ROPE)),
        ((m, ROPE_PAD), BF16, grp.rows(ROPE_PAD)),
    ]
    return pl.pallas_call(
        _in_kernel,
        grid=grp.grid,
        in_specs=[grp.rows(D_MODEL), grp.mod(0), grp.mod(1), _const_spec((1, D_MODEL)),
                  _const_spec(w_in.shape, True), _const_spec((1, MLA_Q_LORA)), _const_spec(wuq.shape, True),
                  _const_spec(wuk.shape, True), _const_spec((1, MLA_KV_LORA)),
                  grp.pos(n_rope), grp.pos(n_rope), grp.pos(ROPE_PAD), grp.pos(ROPE_PAD)],
        out_specs=[o[2] for o in outs],
        out_shape=[jax.ShapeDtypeStruct(o[0], o[1]) for o in outs],
        compiler_params=_params(("arbitrary", "arbitrary"), VMEM_LIMIT_IN_PROJ),
        name="in_proj",
    )(x, mod, mod, g1, w_in, qg, wuq, wuk, kvg, cosq, sinq, cosk, sink)


def _s5_kernel(u_ref, h0r_ref, h0i_ref, abr_ref, abi_ref, cr_ref, ci_ref, bre_ref, bim_ref,
               cre_ref, cim_ref, d_ref, gw_ref, gb_ref,
               y_ref, htr_ref, hti_ref, hr_s, hi_s, xr_s, xi_s, *, tl, bz):
    i = pl.program_id(0)

    @pl.when(i == 0)
    def _():
        hr_s[...] = h0r_ref[...]
        hi_s[...] = h0i_ref[...]

    u = u_ref[...].reshape(tl * bz, SSM_WIDTH)
    ub = u.astype(BF16)
    bur, bui = _mm(ub, bre_ref[...]), _mm(ub, bim_ref[...])
    cr, ci = cr_ref[...], ci_ref[...]
    xr_s[...] = cr * bur - ci * bui
    xi_s[...] = cr * bui + ci * bur
    abr, abi = abr_ref[...], abi_ref[...]

    def step(t, carry):
        hr, hi = carry
        r0 = pl.multiple_of(t * bz, bz)
        nr = abr * hr - abi * hi + xr_s[pl.ds(r0, bz), :]
        ni = abr * hi + abi * hr + xi_s[pl.ds(r0, bz), :]
        xr_s[pl.ds(r0, bz), :] = nr
        xi_s[pl.ds(r0, bz), :] = ni
        return nr, ni

    hr, hi = lax.fori_loop(0, tl, step, (hr_s[...], hi_s[...]))
    hr_s[...] = hr
    hi_s[...] = hi
    htr_ref[...] = hr
    hti_ref[...] = hi
    y = (_mm(xr_s[...].astype(BF16), cre_ref[...]) - _mm(xi_s[...].astype(BF16), cim_ref[...])
         + d_ref[...] * u)
    y = jax.nn.gelu(y)
    y = y * jax.nn.sigmoid(_mm(y.astype(BF16), gw_ref[...]) + gb_ref[...])
    y_ref[...] = y.reshape(tl, bz, SSM_WIDTH)


def _s5(u_tm, h0r, h0i, abr, abi, cr, ci, bre, bim, cre, cim, d, gw, gb, tl):
    length, bz, _ = u_tm.shape
    row = _const_spec((1, SSM_FLAT))
    st = _const_spec((bz, SSM_FLAT))
    blk = pl.BlockSpec((tl, bz, SSM_WIDTH), lambda i: (i, 0, 0))
    return pl.pallas_call(
        functools.partial(_s5_kernel, tl=tl, bz=bz),
        grid=(length // tl,),
        in_specs=[blk, st, st, row, row, row, row, _const_spec(bre.shape), _const_spec(bim.shape),
                  _const_spec(cre.shape), _const_spec(cim.shape), _const_spec((1, SSM_WIDTH)),
                  _const_spec(gw.shape), _const_spec((1, SSM_WIDTH))],
        out_specs=[blk, st, st],
        out_shape=[jax.ShapeDtypeStruct(u_tm.shape, F32), jax.ShapeDtypeStruct((bz, SSM_FLAT), F32),
                   jax.ShapeDtypeStruct((bz, SSM_FLAT), F32)],
        scratch_shapes=[pltpu.VMEM((bz, SSM_FLAT), F32), pltpu.VMEM((bz, SSM_FLAT), F32),
                        pltpu.VMEM((tl * bz, SSM_FLAT), F32), pltpu.VMEM((tl * bz, SSM_FLAT), F32)],
        compiler_params=_params(("arbitrary",)),
        name="s5_scan",
    )(u_tm, h0r, h0i, abr, abi, cr, ci, bre, bim, cre, cim, d, gw, gb)


def _softmax_step(s, keys, m_ref, l_ref, acc_ref):
    m_old = m_ref[...]
    m_new = jnp.maximum(m_old, jnp.max(s, axis=-1, keepdims=True))
    alpha = jnp.exp(m_old - m_new)
    p = jnp.exp(s - jnp.tile(m_new, (1, s.shape[-1] // LANES)))
    l_ref[...] = alpha * l_ref[...] + jnp.sum(p, axis=-1, keepdims=True)
    acc_ref[...] = (acc_ref[...] * jnp.tile(alpha, (1, acc_ref.shape[-1] // LANES))
                    + _mm(p.astype(BF16), keys))
    m_ref[...] = m_new


def _attn_prompt_kernel(ql_ref, qr_ref, lat_ref, kr_ref, wuv_ref, y_ref, m_s, l_s, acc_s, *, tq, tk):
    i = pl.program_id(1)
    rows = MLA_HEADS * tq
    m_s[...] = jnp.full(m_s.shape, -jnp.inf, F32)
    l_s[...] = jnp.zeros(l_s.shape, F32)
    acc_s[...] = jnp.zeros(acc_s.shape, F32)

    def block(k0, width, diagonal):
        kl = lat_ref[pl.ds(k0, width), :]
        kr = kr_ref[pl.ds(k0, width), :]
        per = MLA_HEADS if diagonal or width < tk else HEADS_PER_STRIP
        for h0 in range(0, MLA_HEADS, per):
            h1 = h0 + per
            n = per * tq
            s = (_mm_nt(ql_ref[h0:h1].reshape(n, MLA_KV_LORA), kl)
                 + _mm_nt(qr_ref[h0:h1].reshape(n, ROPE_PAD), kr)) * MLA_SCALE
            if diagonal:
                q_t = lax.broadcasted_iota(jnp.int32, (n, width), 0) % tq
                k_t = lax.broadcasted_iota(jnp.int32, (n, width), 1)
                s = jnp.where(k_t <= q_t, s, -jnp.inf)
            strip = pl.ds(h0 * tq, n)
            _softmax_step(s, kl, m_s.at[strip], l_s.at[strip], acc_s.at[strip])

    first = i * tq
    n_wide = first // tk

    def wide(j, carry):
        block(pl.multiple_of(j * tk, tk), tk, False)
        return carry

    def narrow(j, carry):
        block(pl.multiple_of(n_wide * tk + j * tq, tq), tq, False)
        return carry

    lax.fori_loop(0, n_wide, wide, 0)
    lax.fori_loop(0, (first - n_wide * tk) // tq, narrow, 0)
    block(pl.multiple_of(first, tq), tq, True)
    ctx = acc_s[...] / jnp.tile(l_s[...], (1, MLA_KV_LORA // LANES))
    ys = [_mm(ctx[h * tq:(h + 1) * tq, :].astype(BF16), wuv_ref[h]) for h in range(MLA_HEADS)]
    y_ref[...] = jnp.concatenate(ys, axis=-1).astype(BF16)


def _attn_prompt(grp_bz, length, ql, qr, latb, krb, wuv, tq, tk):
    nb = length // tq
    rows = MLA_HEADS * tq
    heads = lambda w: pl.BlockSpec((MLA_HEADS, tq, w), lambda b, i: (0, b * nb + i, 0))
    return pl.pallas_call(
        functools.partial(_attn_prompt_kernel, tq=tq, tk=tk),
        grid=(grp_bz, nb),
        in_specs=[heads(MLA_KV_LORA), heads(ROPE_PAD),
                  pl.BlockSpec((None, length, MLA_KV_LORA), lambda b, i: (b, 0, 0)),
                  pl.BlockSpec((None, length, ROPE_PAD), lambda b, i: (b, 0, 0)),
                  _const_spec(wuv.shape)],
        out_specs=pl.BlockSpec((tq, MLA_HEADS * MLA_V), lambda b, i: (b * nb + i, 0)),
        out_shape=jax.ShapeDtypeStruct((grp_bz * length, MLA_HEADS * MLA_V), BF16),
        scratch_shapes=[pltpu.VMEM((rows, LANES), F32), pltpu.VMEM((rows, LANES), F32),
                        pltpu.VMEM((rows, MLA_KV_LORA), F32)],
        compiler_params=_params(("arbitrary", "arbitrary")),
        name="attn_prompt",
    )(ql, qr, latb.reshape(grp_bz, length, MLA_KV_LORA), krb.reshape(grp_bz, length, ROPE_PAD), wuv)


def _attn_sample_kernel(pt_ref, ql_ref, qr_ref, nl_ref, nk_ref, wuv_ref, lat_hbm, krt_hbm, y_ref,
                        lat_buf, krt_buf, sem, m_s, l_s, acc_s, *, layer, pg, ns, lq):
    b, s_idx = pl.program_id(0), pl.program_id(1)
    n_steps = pl.num_programs(1)
    total = pl.num_programs(0) * n_steps
    step = b * n_steps + s_idx
    slot = step % DMA_SLOTS

    def page_copies(at_step):
        seq, st, sl = at_step // n_steps, at_step % n_steps, at_step % DMA_SLOTS
        cps = []
        for i in range(pg):
            page = pt_ref[seq, st * pg + i]
            cps.append(pltpu.make_async_copy(lat_hbm.at[layer, page], lat_buf.at[sl, i], sem.at[sl]))
            cps.append(pltpu.make_async_copy(krt_hbm.at[layer, page], krt_buf.at[sl, i], sem.at[sl]))
        return cps

    for first in range(PREFETCH_AHEAD):
        @pl.when((step == 0) & (first < total))
        def _(first=first):
            for cp in page_copies(first):
                cp.start()

    for cp in page_copies(step):
        cp.wait()

    @pl.when(step + PREFETCH_AHEAD < total)
    def _():
        for cp in page_copies(step + PREFETCH_AHEAD):
            cp.start()

    @pl.when(s_idx == 0)
    def _():
        m_s[...] = jnp.full(m_s.shape, -jnp.inf, F32)
        l_s[...] = jnp.zeros(l_s.shape, F32)
        acc_s[...] = jnp.zeros(acc_s.shape, F32)

    q_l, q_r = ql_ref[...], qr_ref[...]
    per = pg // ns
    for st in range(ns):
        keys = jnp.concatenate([lat_buf[slot, i].astype(BF16) for i in range(st * per, (st + 1) * per)], axis=0)
        krt = jnp.concatenate([krt_buf[slot, i].astype(BF16) for i in range(st * per, (st + 1) * per)], axis=1)
        s = (_mm_nt(q_l, keys) + _mm(q_r, krt)) * MLA_SCALE
        _softmax_step(s, keys, m_s.at[st], l_s.at[st], acc_s.at[st])

    @pl.when(s_idx == n_steps - 1)
    def _():
        rows = MLA_HEADS * lq
        nl, nk = nl_ref[...], nk_ref[...]
        s = (_mm_nt(q_l, nl) + _mm_nt(q_r, nk)) * MLA_SCALE
        q_t = lax.broadcasted_iota(jnp.int32, (rows, lq), 0) % lq
        k_t = lax.broadcasted_iota(jnp.int32, (rows, lq), 1)
        s = jnp.where(k_t <= q_t, s, -jnp.inf)
        m_old = m_s[0]
        m_new = jnp.maximum(m_old, jnp.max(s, axis=-1, keepdims=True))
        for st in range(1, ns):
            m_new = jnp.maximum(m_new, m_s[st])
        p = jnp.exp(s - m_new[:, :1])
        l_tot = jnp.sum(p, axis=-1, keepdims=True)
        acc = _mm(p.astype(BF16), nl)
        for st in range(ns):
            a = jnp.exp(m_s[st] - m_new)
            l_tot = l_tot + a * l_s[st]
            acc = acc + jnp.tile(a, (1, MLA_KV_LORA // LANES)) * acc_s[st]
        ctx = acc / jnp.tile(l_tot, (1, MLA_KV_LORA // LANES))
        ys = [_mm(ctx[h * lq:(h + 1) * lq, :].astype(BF16), wuv_ref[h]) for h in range(MLA_HEADS)]
        y_ref[...] = jnp.concatenate(ys, axis=-1).astype(BF16)


def _attn_sample(layer, page_table, ql, qr, latb, krb, cache_lat, cache_krt, wuv, bz, lq):
    n_pages = page_table.shape[1]
    pg = min(16, n_pages)
    ns = 2 if pg % 2 == 0 else 1
    page = cache_lat.shape[2]
    rows = MLA_HEADS * lq
    qlh = ql.reshape(MLA_HEADS, bz, lq, MLA_KV_LORA).transpose(1, 0, 2, 3).reshape(bz, rows, MLA_KV_LORA)
    qrh = qr[:, :, :MLA_ROPE].reshape(MLA_HEADS, bz, lq, MLA_ROPE).transpose(1, 0, 2, 3).reshape(
        bz, rows, MLA_ROPE)

    def seq(n, w):
        return pl.BlockSpec((None, n, w), lambda b, s, pt: (b, 0, 0))

    in_specs = [seq(rows, MLA_KV_LORA), seq(rows, MLA_ROPE), seq(lq, MLA_KV_LORA), seq(lq, MLA_ROPE),
                pl.BlockSpec(wuv.shape, lambda b, s, pt: (0, 0, 0)),
                pl.BlockSpec(memory_space=pl.ANY), pl.BlockSpec(memory_space=pl.ANY)]
    y = pl.pallas_call(
        functools.partial(_attn_sample_kernel, layer=layer, pg=pg, ns=ns, lq=lq),
        grid_spec=pltpu.PrefetchScalarGridSpec(
            num_scalar_prefetch=1,
            grid=(bz, n_pages // pg),
            in_specs=in_specs,
            out_specs=pl.BlockSpec((None, lq, MLA_HEADS * MLA_V), lambda b, s, pt: (b, 0, 0)),
            scratch_shapes=[pltpu.VMEM((DMA_SLOTS, pg, page, MLA_KV_LORA), F32),
                            pltpu.VMEM((DMA_SLOTS, pg, MLA_ROPE, page), F32),
                            pltpu.SemaphoreType.DMA((DMA_SLOTS,)),
                            pltpu.VMEM((ns, rows, LANES), F32), pltpu.VMEM((ns, rows, LANES), F32),
                            pltpu.VMEM((ns, rows, MLA_KV_LORA), F32)]),
        out_shape=jax.ShapeDtypeStruct((bz, lq, MLA_HEADS * MLA_V), BF16),
        compiler_params=_params(("arbitrary", "arbitrary")),
        name="attn_sample",
    )(page_table, qlh, qrh, latb.reshape(bz, lq, MLA_KV_LORA),
      krb[:, :MLA_ROPE].reshape(bz, lq, MLA_ROPE), wuv, cache_lat, cache_krt)
    return y.reshape(bz * lq, MLA_HEADS * MLA_V)


def _hgrn_kernel(hz_ref, st0_ref, lb_ref, ng_ref, y_ref, stt_ref, st_s, w_s, *, ch, tb, nseq):
    j = pl.program_id(1)

    @pl.when(j == 0)
    def _():
        st_s[...] = st0_ref[...]

    lb = lb_ref[...]
    log_lb, log1m_lb, one_m_lb = jnp.log(lb), jnp.log1p(-lb), 1.0 - lb
    tri = (lax.broadcasted_iota(jnp.int32, (ch, ch), 0)
           >= lax.broadcasted_iota(jnp.int32, (ch, ch), 1)).astype(F32)
    same_head = (lax.broadcasted_iota(jnp.int32, (HG_W, HG_W), 0) // HG_K
                 == lax.broadcasted_iota(jnp.int32, (HG_W, HG_W), 1) // HG_K)
    head_ones = same_head.astype(F32)
    head_ones_b = same_head.astype(BF16)
    ng = ng_ref[...]

    def chunk(c, carry):
        r0 = pl.multiple_of(c * ch, ch)
        for sq in range(nseq):
            chunk_of(sq, r0)
        return carry

    def chunk_of(sq, r0):
        q = hz_ref[sq, pl.ds(r0, ch), 0:HG_W]
        fp = hz_ref[sq, pl.ds(r0, ch), HG_W:2 * HG_W]
        v = hz_ref[sq, pl.ds(r0, ch), 2 * HG_W:3 * HG_W]
        g = hz_ref[sq, pl.ds(r0, ch), 3 * HG_W:4 * HG_W]
        logf = jnp.logaddexp(log_lb, log1m_lb + jax.nn.log_sigmoid(fp))
        k = one_m_lb * jax.nn.sigmoid(-fp)
        b = _mm_f32(tri, logf)
        st = st_s[sq]
        o = _mm_nt((q * jnp.exp(b)).astype(BF16), st.astype(BF16))
        starts, r = [], 0
        for s in range(ch):
            t0 = (s // SUBLANES) * SUBLANES
            t_idx = t0 + lax.broadcasted_iota(jnp.int32, (ch - t0, HG_W), 0)
            e = jnp.exp(jnp.where(t_idx >= s, b[t0:] - b[s:s + 1, :], -jnp.inf))
            w_s[sq, r:r + ch - t0, :] = q[t0:] * k[s:s + 1, :] * e
            starts.append(r)
            r += ch - t0
        att = _mm(w_s[sq].astype(BF16), head_ones_b)
        tiles = [o[t0:t0 + SUBLANES] for t0 in range(0, ch, SUBLANES)]
        for s in range(ch):
            for kt in range(s // SUBLANES, ch // SUBLANES):
                a0 = starts[s] + (kt - s // SUBLANES) * SUBLANES
                tiles[kt] = tiles[kt] + att[a0:a0 + SUBLANES, :] * v[s:s + 1, :]
        o = jnp.concatenate(tiles, axis=0)
        bl = b[ch - 1:ch, :]
        kd = k * jnp.exp(bl - b)
        upd = _mm_tn(v.astype(BF16), kd.astype(BF16))
        st_s[sq] = st * jnp.exp(bl) + jnp.where(same_head, upd, 0.0)
        ms = _mm_f32(o * o, head_ones) * (1.0 / HG_K)
        on = o * lax.rsqrt(ms + EPS) * ng
        y_ref[sq, pl.ds(r0, ch), :] = on * (g * jax.nn.sigmoid(g))

    lax.fori_loop(0, tb // ch, chunk, 0)

    @pl.when(j == pl.num_programs(1) - 1)
    def _():
        stt_ref[...] = st_s[...]


def _hgrn(bz, length, hz, st0, lb, ng, ch, tb):
    nb = length // tb
    nseq = 2 if bz % 2 == 0 else 1
    st_spec = pl.BlockSpec((nseq, HG_W, HG_W), lambda b, j: (b, 0, 0))
    pair_rows = sum(ch - s // SUBLANES * SUBLANES for s in range(ch))
    y, st_t = pl.pallas_call(
        functools.partial(_hgrn_kernel, ch=ch, tb=tb, nseq=nseq),
        grid=(bz // nseq, nb),
        in_specs=[pl.BlockSpec((nseq, tb, 4 * HG_W), lambda b, j: (b, j, 0)), st_spec,
                  _const_spec((1, HG_W)), _const_spec((1, HG_W))],
        out_specs=[pl.BlockSpec((nseq, tb, HG_W), lambda b, j: (b, j, 0)), st_spec],
        out_shape=[jax.ShapeDtypeStruct((bz, length, HG_W), F32),
                   jax.ShapeDtypeStruct((bz, HG_W, HG_W), F32)],
        scratch_shapes=[pltpu.VMEM((nseq, HG_W, HG_W), F32), pltpu.VMEM((nseq, pair_rows, HG_W), F32)],
        compiler_params=_params(("arbitrary", "arbitrary")),
        name="hgrn2",
    )(hz.reshape(bz, length, 4 * HG_W), st0, lb, ng)
    return y.reshape(bz * length, HG_W), st_t


def _mix_kernel(*refs, moe):
    if moe:
        (x_ref, ya_ref, yb_ref, yc_ref, gates_ref, g1_ref, sh2_ref, sc2_ref, n2_ref,
         pa_ref, pb_ref, pc_ref, wo_ref, rw_ref, rb_ref,
         xo_ref, h2_ref, gate_ref, route_ref, wts_ref, cb_ref, cnt_ref, run_s) = refs
    else:
        (x_ref, ya_ref, yb_ref, yc_ref, gates_ref, g1_ref, sh2_ref, sc2_ref, n2_ref,
         pa_ref, pb_ref, pc_ref, wo_ref, xo_ref, h2_ref) = refs
    d = D_MODEL
    mix = (jax.nn.sigmoid(gates_ref[:, 0:d]) * _mm(ya_ref[...].astype(BF16), pa_ref[...])
           + jax.nn.sigmoid(gates_ref[:, d:2 * d]) * _mm(yb_ref[...], pb_ref[...])
           + jax.nn.sigmoid(gates_ref[:, 2 * d:3 * d]) * _mm(yc_ref[...].astype(BF16), pc_ref[...]))
    xo = x_ref[...] + g1_ref[...] * _mm(mix.astype(BF16), wo_ref[...])
    xo_ref[...] = xo
    h2 = _rms(xo, n2_ref[...]) * (1.0 + sc2_ref[...]) + sh2_ref[...]
    h2_ref[...] = h2.astype(BF16)
    if moe:
        logits = _mm_split(h2, rw_ref[...]) + rb_ref[...]
        lane = lax.broadcasted_iota(jnp.int32, logits.shape, 1)
        lg = jnp.where(lane < N_EXPERTS, logits, -jnp.inf)
        m1 = jnp.max(lg, axis=-1, keepdims=True)
        i1 = jnp.min(jnp.where(lg == m1, lane, LANES), axis=-1, keepdims=True)
        lg2 = jnp.where(lane == i1, -jnp.inf, lg)
        m2 = jnp.max(lg2, axis=-1, keepdims=True)
        i2 = jnp.min(jnp.where(lg2 == m2, lane, LANES), axis=-1, keepdims=True)
        e2 = jnp.exp(m2 - m1)
        den = 1.0 + e2
        w1, w2 = 1.0 / den, e2 / den
        gate_ref[...] = jnp.where(lane == i1, w1, 0.0) + jnp.where(lane == i2, w2, 0.0)
        @pl.when((pl.program_id(0) == 0) & (pl.program_id(1) == 0))
        def _():
            run_s[...] = jnp.zeros(run_s.shape, F32)

        tm = logits.shape[0]
        sel = jnp.where(lane == i1, 1.0, 0.0) + jnp.where(lane == i2, 1.0, 0.0)
        before = (lax.broadcasted_iota(jnp.int32, (tm, tm), 0) > lax.broadcasted_iota(jnp.int32, (tm, tm), 1))
        rank = _mm(jnp.where(before, 1.0, 0.0).astype(BF16), sel.astype(BF16))
        r1 = jnp.sum(jnp.where(lane == i1, rank, 0.0), axis=-1, keepdims=True).astype(jnp.int32)
        r2 = jnp.sum(jnp.where(lane == i2, rank, 0.0), axis=-1, keepdims=True).astype(jnp.int32)
        route_ref[...] = jnp.where(lane == 0, i1 * tm + r1, jnp.where(lane == 1, i2 * tm + r2, 0))
        wts_ref[...] = jnp.where(lane == 0, w1, jnp.where(lane == 1, w2, 0.0))
        cb_ref[...] = run_s[...]
        total = run_s[...] + jnp.sum(sel, axis=0, keepdims=True)
        run_s[...] = total
        cnt_ref[...] = total


def _mix(grp, x, ya, yb, yc, gates, mod, n2, pa, pb, pc, wo, router=None):
    m = grp.m
    moe = router is not None
    in_specs = [grp.rows(D_MODEL), grp.tmajor(SSM_WIDTH), grp.rows(MLA_HEADS * MLA_V), grp.rows(HG_W),
                grp.rows(3 * D_MODEL), grp.mod(2), grp.mod(3), grp.mod(4), _const_spec((1, D_MODEL)),
                _const_spec(pa.shape), _const_spec(pb.shape), _const_spec(pc.shape), _const_spec(wo.shape)]
    args = [x, ya, yb, yc, gates, mod, mod, mod, n2, pa, pb, pc, wo]
    out_specs = [grp.rows(D_MODEL), grp.rows(D_MODEL)]
    out_shape = [jax.ShapeDtypeStruct((m, D_MODEL), F32), jax.ShapeDtypeStruct((m, D_MODEL), BF16)]
    if moe:
        in_specs += [_const_spec(router[0].shape), _const_spec(router[1].shape)]
        args += list(router)
        nb, n_tiles = grp.nb, m // grp.tm
        out_specs += [grp.rows(LANES), grp.rows(LANES), grp.rows(LANES),
                      pl.BlockSpec((None, 1, LANES), lambda b, i: (b * nb + i, 0, 0)), _const_spec((1, LANES))]
        out_shape += [jax.ShapeDtypeStruct((m, LANES), F32), jax.ShapeDtypeStruct((m, LANES), jnp.int32),
                      jax.ShapeDtypeStruct((m, LANES), F32), jax.ShapeDtypeStruct((n_tiles, 1, LANES), F32),
                      jax.ShapeDtypeStruct((1, LANES), F32)]
    return pl.pallas_call(
        functools.partial(_mix_kernel, moe=moe),
        grid=grp.grid, in_specs=in_specs, out_specs=out_specs, out_shape=out_shape,
        scratch_shapes=[pltpu.VMEM((1, LANES), F32)] if moe else [],
        compiler_params=_params(("arbitrary", "arbitrary")),
        name="mix_moe" if moe else "mix",
    )(*args)


def _swiglu_tile(h2_ref, wg_ref, wu_ref, wd_ref):
    h = h2_ref[...]
    g = _mm(h, wg_ref[...])
    u = _mm(h, wu_ref[...])
    a = (g * jax.nn.sigmoid(g) * u).astype(BF16)
    return _mm(a, wd_ref[...])


def _ffn_kernel(h2_ref, x_ref, g2_ref, wg_ref, wu_ref, wd_ref, o_ref, acc_s):
    f = pl.program_id(2)

    @pl.when(f == 0)
    def _():
        acc_s[...] = jnp.zeros(acc_s.shape, F32)

    acc_s[...] += _swiglu_tile(h2_ref, wg_ref, wu_ref, wd_ref)

    @pl.when(f == pl.num_programs(2) - 1)
    def _():
        o_ref[...] = x_ref[...] + g2_ref[...] * acc_s[...]


def _ffn(grp, h2, x, mod, wg, wu, wd, j, tf):
    nf = FF_DIM // tf
    return pl.pallas_call(
        _ffn_kernel,
        grid=grp.grid + (nf,),
        in_specs=[grp.rows(D_MODEL), grp.rows(D_MODEL), grp.mod(5),
                  pl.BlockSpec((None, D_MODEL, tf), lambda b, i, f: (j, 0, f)),
                  pl.BlockSpec((None, D_MODEL, tf), lambda b, i, f: (j, 0, f)),
                  pl.BlockSpec((None, tf, D_MODEL), lambda b, i, f: (j, f, 0))],
        out_specs=grp.rows(D_MODEL),
        out_shape=jax.ShapeDtypeStruct((grp.m, D_MODEL), F32),
        scratch_shapes=[pltpu.VMEM((grp.tm, D_MODEL), F32)],
        compiler_params=_params(("arbitrary", "arbitrary", "arbitrary")),
        name="ffn_dense",
    )(h2, x, mod, wg, wu, wd)


def _moe_kernel(h2_ref, x_ref, g2_ref, gate_ref, wg_ref, wu_ref, wd_ref, o_ref, acc_s):
    e, f = pl.program_id(2), pl.program_id(3)

    @pl.when((e == 0) & (f == 0))
    def _():
        acc_s[...] = jnp.zeros(acc_s.shape, F32)

    gate = gate_ref[...]
    lane = lax.broadcasted_iota(jnp.int32, gate.shape, 1)
    gcol = jnp.sum(jnp.where(lane == e, gate, 0.0), axis=-1, keepdims=True)
    acc_s[...] += gcol * _swiglu_tile(h2_ref, wg_ref, wu_ref, wd_ref)

    @pl.when((e == pl.num_programs(2) - 1) & (f == pl.num_programs(3) - 1))
    def _():
        o_ref[...] = x_ref[...] + g2_ref[...] * acc_s[...]


def _moe(grp, h2, x, mod, gate, wg, wu, wd, j, tf):
    nf = FF_DIM // tf
    return pl.pallas_call(
        _moe_kernel,
        grid=grp.grid + (N_EXPERTS, nf),
        in_specs=[grp.rows(D_MODEL), grp.rows(D_MODEL), grp.mod(5), grp.rows(LANES),
                  pl.BlockSpec((None, None, D_MODEL, tf), lambda b, i, e, f: (j, e, 0, f)),
                  pl.BlockSpec((None, None, D_MODEL, tf), lambda b, i, e, f: (j, e, 0, f)),
                  pl.BlockSpec((None, None, tf, D_MODEL), lambda b, i, e, f: (j, e, f, 0))],
        out_specs=grp.rows(D_MODEL),
        out_shape=jax.ShapeDtypeStruct((grp.m, D_MODEL), F32),
        scratch_shapes=[pltpu.VMEM((grp.tm, D_MODEL), F32)],
        compiler_params=_params(("arbitrary",) * 4),
        name="ffn_moe",
    )(h2, x, mod, gate, wg, wu, wd)


def _route_plan(route, cb, cnt, n_tok):
    cnt8 = cnt[0, :N_EXPERTS].astype(jnp.int32)
    cb8 = cb[:, 0, :N_EXPERTS].astype(jnp.int32)
    p1, p2 = route[:, 0], route[:, 1]
    gp = (cnt8 + MOE_T - 1) // MOE_T * MOE_T
    off_end = jnp.cumsum(gp)
    off = off_end - gp
    start = (off[None, :] + cb8).reshape(-1)
    nbe = (jnp.concatenate([cb8[1:], cnt8[None]], axis=0) - cb8).reshape(-1)
    n_tiles = (2 * n_tok + N_EXPERTS * (MOE_T - 1)) // MOE_T
    idx = jnp.arange(n_tiles, dtype=jnp.int32)
    valid = idx * MOE_T < off_end[-1]
    blk = jnp.where(valid, idx, off_end[-1] // MOE_T - 1)
    te = jnp.sum((blk * MOE_T)[:, None] >= off_end[None, :], axis=1).astype(jnp.int32)
    fill = jnp.concatenate([off + cnt8, gp - cnt8, off_end[-1:] // MOE_T])
    return p1, p2, start, nbe, fill, te, blk, valid.astype(jnp.int32), n_tiles


def _segment_copies(n, src_of, dst_of, sem, max_rows=MOE_TM):
    out = []
    for k in range(max_rows.bit_length()):
        size = 1 << k
        pos = (n >> (k + 1)) << (k + 1)
        out.append((((n >> k) & 1) == 1, pltpu.make_async_copy(src_of(pos, size), dst_of(pos, size), sem)))
    return out


def _moe_gather_kernel(p1_ref, p2_ref, start_ref, nbe_ref, fill_ref, x_ref, xs_ref, buf, zeros_s, sem, zsem,
                       *, n_tiles):
    b = pl.program_id(0)
    half = b % 2

    def tok(i, carry):
        row = x_ref[i]
        buf[half, p1_ref[b * MOE_TM + i]] = row
        buf[half, p2_ref[b * MOE_TM + i]] = row
        return carry

    lax.fori_loop(0, MOE_TM, tok, 0, unroll=8)

    def tile_copies(tile):
        copies = []
        for e in range(N_EXPERTS):
            s = start_ref[tile * N_EXPERTS + e]
            copies += _segment_copies(
                nbe_ref[tile * N_EXPERTS + e],
                lambda pos, size, e=e: buf.at[tile % 2, pl.ds(e * MOE_TM + pos, size)],
                lambda pos, size, s=s: xs_ref.at[pl.ds(s + pos, size)], sem.at[tile % 2])
        return copies

    @pl.when(b > 0)
    def _():
        for cond, cp in tile_copies(b - 1):
            pl.when(cond)(cp.wait)

    mine = tile_copies(b)
    for cond, cp in mine:
        pl.when(cond)(cp.start)

    @pl.when(b == pl.num_programs(0) - 1)
    def _():
        for cond, cp in mine:
            pl.when(cond)(cp.wait)
        zeros_s[...] = jnp.zeros(zeros_s.shape, F32)
        fills = []
        for e in range(N_EXPERTS):
            s = fill_ref[e]
            fills += _segment_copies(
                fill_ref[N_EXPERTS + e],
                lambda pos, size: zeros_s.at[pl.ds(0, size)],
                lambda pos, size, s=s: xs_ref.at[pl.ds(s + pos, size)], zsem, max_rows=MOE_T - 1)
        for j in range(n_tiles):
            fills.append((j >= fill_ref[2 * N_EXPERTS],
                          pltpu.make_async_copy(zeros_s, xs_ref.at[pl.ds(j * MOE_T, MOE_T)], zsem)))
        for cond, cp in fills:
            pl.when(cond)(cp.start)
        for cond, cp in fills:
            pl.when(cond)(cp.wait)


def _moe_gather(x3, p1, p2, start, nbe, fill, n_tiles):
    n_tok = x3.shape[0]
    tile = pl.BlockSpec((MOE_TM, SUBLANES, LANES), lambda b, *_: (b, 0, 0))
    return pl.pallas_call(
        functools.partial(_moe_gather_kernel, n_tiles=n_tiles),
        grid_spec=pltpu.PrefetchScalarGridSpec(
            num_scalar_prefetch=5, grid=(n_tok // MOE_TM,),
            in_specs=[tile],
            out_specs=pl.BlockSpec(memory_space=pl.ANY),
            scratch_shapes=[pltpu.VMEM((2, N_EXPERTS * MOE_TM, SUBLANES, LANES), F32),
                            pltpu.VMEM((MOE_T, SUBLANES, LANES), F32),
                            pltpu.SemaphoreType.DMA((2,)), pltpu.SemaphoreType.DMA]),
        out_shape=jax.ShapeDtypeStruct((n_tiles * MOE_T, SUBLANES, LANES), F32),
        compiler_params=_params(("arbitrary",)),
        name="moe_gather",
    )(p1, p2, start, nbe, fill, x3)


def _moe_ffn_kernel(te_ref, blk_ref, valid_ref, x_ref, wg_ref, wu_ref, wd_ref, y_ref, xb_s, acc_s):
    del te_ref, blk_ref
    j, f = pl.program_id(0), pl.program_id(1)

    @pl.when(valid_ref[j] == 1)
    def _():
        @pl.when(f == 0)
        def _():
            xb_s[...] = x_ref[...].astype(BF16)
            acc_s[...] = jnp.zeros(acc_s.shape, F32)

        acc_s[...] += _swiglu_tile(xb_s, wg_ref, wu_ref, wd_ref)

        @pl.when(f == pl.num_programs(1) - 1)
        def _():
            y_ref[...] = acc_s[...]

    @pl.when((valid_ref[j] == 0) & (f == pl.num_programs(1) - 1))
    def _():
        y_ref[...] = jnp.zeros(y_ref.shape, F32)


def _moe_ffn(xs, te, blk, valid, wg, wu, wd, layer, tf):
    n_rows = xs.shape[0]
    nf = FF_DIM // tf
    rows = pl.BlockSpec((MOE_T, D_MODEL), lambda j, f, te, blk, valid: (blk[j], 0))

    def fsel(j, f, valid):
        return jnp.where(valid[j] == 1, f, nf - 1)

    return pl.pallas_call(
        _moe_ffn_kernel,
        grid_spec=pltpu.PrefetchScalarGridSpec(
            num_scalar_prefetch=3, grid=(n_rows // MOE_T, nf),
            in_specs=[rows,
                      pl.BlockSpec((None, None, D_MODEL, tf),
                                   lambda j, f, te, blk, valid: (layer, te[j], 0, fsel(j, f, valid))),
                      pl.BlockSpec((None, None, D_MODEL, tf),
                                   lambda j, f, te, blk, valid: (layer, te[j], 0, fsel(j, f, valid))),
                      pl.BlockSpec((None, None, tf, D_MODEL),
                                   lambda j, f, te, blk, valid: (layer, te[j], fsel(j, f, valid), 0))],
            out_specs=pl.BlockSpec((MOE_T, D_MODEL), lambda j, f, te, blk, valid: (j, 0)),
            scratch_shapes=[pltpu.VMEM((MOE_T, D_MODEL), BF16), pltpu.VMEM((MOE_T, D_MODEL), F32)]),
        out_shape=jax.ShapeDtypeStruct((n_rows, D_MODEL), F32),
        compiler_params=_params(("arbitrary", "arbitrary")),
        name="moe_ffn",
    )(te, blk, valid, xs, wg, wu, wd)


def _moe_combine_kernel(p1_ref, p2_ref, start_ref, nbe_ref, w1_ref, w2_ref, ys_ref, x_ref, g2_ref, o_ref,
                        buf, sem):
    b = pl.program_id(0)
    n_tiles = pl.num_programs(0)
    slot = b % DMA_SLOTS

    def tile_copies(tile):
        copies = []
        for e in range(N_EXPERTS):
            s = start_ref[tile * N_EXPERTS + e]
            copies += _segment_copies(
                nbe_ref[tile * N_EXPERTS + e],
                lambda pos, size, s=s: ys_ref.at[pl.ds(s + pos, size)],
                lambda pos, size, e=e: buf.at[tile % DMA_SLOTS, pl.ds(e * MOE_TM + pos, size)],
                sem.at[tile % DMA_SLOTS])
        return copies

    for first in range(PREFETCH_AHEAD):
        @pl.when((b == 0) & (first < n_tiles))
        def _(first=first):
            for cond, cp in tile_copies(first):
                pl.when(cond)(cp.start)

    for cond, cp in tile_copies(b):
        pl.when(cond)(cp.wait)

    @pl.when(b + PREFETCH_AHEAD < n_tiles)
    def _():
        for cond, cp in tile_copies(b + PREFETCH_AHEAD):
            pl.when(cond)(cp.start)

    g2 = g2_ref[...]

    def tok(i, carry):
        t = b * MOE_TM + i
        o_ref[i] = x_ref[i] + g2 * (w1_ref[t] * buf[slot, p1_ref[t]] + w2_ref[t] * buf[slot, p2_ref[t]])
        return carry

    lax.fori_loop(0, MOE_TM, tok, 0, unroll=8)


def _moe_combine(ys3, x3, g2_3, p1, p2, start, nbe, w1, w2, seq_len):
    n_tok = x3.shape[0]
    per_seq = seq_len // MOE_TM
    tile = pl.BlockSpec((MOE_TM, SUBLANES, LANES), lambda b, *_: (b, 0, 0))
    return pl.pallas_call(
        _moe_combine_kernel,
        grid_spec=pltpu.PrefetchScalarGridSpec(
            num_scalar_prefetch=6, grid=(n_tok // MOE_TM,),
            in_specs=[pl.BlockSpec(memory_space=pl.ANY), tile,
                      pl.BlockSpec((None, SUBLANES, LANES), lambda b, *_: (b // per_seq, 0, 0))],
            out_specs=tile,
            scratch_shapes=[pltpu.VMEM((DMA_SLOTS, N_EXPERTS * MOE_TM, SUBLANES, LANES), F32),
                            pltpu.SemaphoreType.DMA((DMA_SLOTS,))]),
        out_shape=jax.ShapeDtypeStruct(x3.shape, F32),
        compiler_params=_params(("arbitrary",)),
        name="moe_combine",
    )(p1, p2, start, nbe, w1, w2, ys3, x3, g2_3)


def _moe_routed(bz, length, h2, x, mod, route, wts, cb, cnt, wg, wu, wd, layer, tf):
    m = bz * length
    as_tiles = lambda a: a.reshape(a.shape[0], SUBLANES, LANES)
    p1, p2, start, nbe, fill, te, blk, valid, n_tiles = _route_plan(route, cb, cnt, m)
    xs3 = _moe_gather(as_tiles(h2.astype(F32)), p1, p2, start, nbe, fill, n_tiles)
    ys = _moe_ffn(xs3.reshape(n_tiles * MOE_T, D_MODEL), te, blk, valid, wg, wu, wd, layer, tf)
    g2_3 = mod[:, 0, 5 * D_MODEL:6 * D_MODEL].reshape(bz, SUBLANES, LANES)
    out3 = _moe_combine(as_tiles(ys), as_tiles(x), g2_3, p1, p2, start, nbe, wts[:, 0], wts[:, 1], length)
    return out3.reshape(m, D_MODEL)


def _final_kernel(x_ref, g_ref, o_ref):
    o_ref[...] = _rms(x_ref[...], g_ref[...])


def _final_norm(grp, x, g):
    return pl.pallas_call(
        _final_kernel, grid=grp.grid,
        in_specs=[grp.rows(D_MODEL), _const_spec((1, D_MODEL))],
        out_specs=grp.rows(D_MODEL),
        out_shape=jax.ShapeDtypeStruct((grp.m, D_MODEL), F32),
        compiler_params=_params(("arbitrary", "arbitrary")),
        name="final_norm",
    )(x, g)


def _relayout_weights(w):
    bf = lambda a: a.astype(BF16)
    depth = w["w_in"].shape[0]
    wi = w["w_in"]
    o = [0, 256, 640, 896, 928, 1184, 1440, 1696, 1952, 5024]
    kr = wi[:, :, o[3]:o[4]]
    half = MLA_ROPE // 2
    kr_rot = jnp.concatenate([-kr[..., half:], kr[..., :half]], axis=-1)
    zeros = lambda n: jnp.zeros(wi.shape[:2] + (n,), wi.dtype)
    w_in = bf(jnp.concatenate([
        wi[:, :, o[0]:o[1]], wi[:, :, o[2]:o[3]], wi[:, :, o[1]:o[2]], zeros(C_HZ - C_CQ - MLA_Q_LORA),
        wi[:, :, o[4]:o[8]], wi[:, :, o[8]:o[9]], kr, zeros(ROPE_PAD - MLA_ROPE),
        kr_rot, zeros(ROPE_PAD - MLA_ROPE)], axis=-1))
    uq = w["mla_w_uq"].reshape(depth, MLA_Q_LORA, MLA_HEADS, MLA_NOPE + MLA_ROPE)
    nope = uq[..., :MLA_NOPE].reshape(depth, MLA_Q_LORA, MLA_HEADS * MLA_NOPE)
    r1 = uq[..., MLA_NOPE:MLA_NOPE + half]
    r2 = uq[..., MLA_NOPE + half:]
    pad = jnp.zeros(uq.shape[:3] + (ROPE_PAD - MLA_ROPE,), uq.dtype)
    rope = jnp.concatenate([r1, r2, pad], axis=-1).reshape(depth, MLA_Q_LORA, MLA_HEADS * ROPE_PAD)
    rope_rot = jnp.concatenate([-r2, r1, pad], axis=-1).reshape(depth, MLA_Q_LORA, MLA_HEADS * ROPE_PAD)
    wuq = bf(jnp.concatenate([nope, rope, rope_rot], axis=-1))
    eye_h = jnp.eye(MLA_HEADS, dtype=F32)
    uk = w["mla_w_uk"].reshape(depth, MLA_KV_LORA, MLA_HEADS, MLA_NOPE)
    wuk = bf(jnp.einsum("lrhd,hg->lhdgr", uk, eye_h).reshape(
        depth, MLA_HEADS * MLA_NOPE, MLA_HEADS * MLA_KV_LORA))
    wuv = bf(w["mla_w_uv"].reshape(depth, MLA_KV_LORA, MLA_HEADS, MLA_V).transpose(0, 2, 1, 3))
    eye_g = jnp.eye(SSM_GROUPS, dtype=F32)
    b_blk = lambda b: bf(jnp.einsum("lgpc,gh->lgchp", b, eye_g).reshape(depth, SSM_WIDTH, SSM_FLAT))
    c_blk = lambda c: bf(jnp.einsum("lgcp,gh->lgphc", c, eye_g).reshape(depth, SSM_FLAT, SSM_WIDTH))
    n_moe = w["moe_router_w"].shape[0]
    rw = jnp.concatenate([w["moe_router_w"],
                          jnp.zeros((n_moe, D_MODEL, LANES - N_EXPERTS), F32)], axis=-1)
    rb = jnp.concatenate([w["moe_router_b"], jnp.zeros((n_moe, LANES - N_EXPERTS), F32)],
                         axis=-1).reshape(n_moe, 1, LANES)
    return dict(
        w_in=w_in, wuq=wuq, wuk=wuk, wuv=wuv,
        bre=b_blk(w["ssm_b_re"]), bim=b_blk(w["ssm_b_im"]),
        cre=c_blk(w["ssm_c_re"]), cim=c_blk(w["ssm_c_im"]),
        glu_w=bf(w["ssm_glu_w"]), proj_a=bf(w["proj_a"]), proj_b=bf(w["proj_b"]),
        proj_c=bf(w["proj_c"]), w_out=bf(w["w_out"]), rw=rw, rb=rb,
        ffn_g=bf(w["ffn_w_gate"]), ffn_u=bf(w["ffn_w_up"]), ffn_d=bf(w["ffn_w_down"]),
        moe_g=bf(w["moe_w_gate"]), moe_u=bf(w["moe_w_up"]), moe_d=bf(w["moe_w_down"]),
        hg_norm=jnp.tile(w["hgrn_norm_g"], (1, HG_HEADS)))


def _rope_tables(pos):
    half = MLA_ROPE // 2
    freq = ROPE_THETA ** (-jnp.arange(half, dtype=F32) / half)
    ang = pos.astype(F32)[:, None] * freq[None, :]
    cos, sin = jnp.cos(ang), jnp.sin(ang)
    pad = jnp.zeros((pos.shape[0], ROPE_PAD - MLA_ROPE), F32)
    cosk = jnp.concatenate([cos, cos, pad], axis=-1)
    sink = jnp.concatenate([sin, sin, pad], axis=-1)
    return jnp.tile(cosk, (1, MLA_HEADS)), jnp.tile(sink, (1, MLA_HEADS)), cosk, sink


def _state_to_blocks(s):
    eye = jnp.eye(HG_HEADS, dtype=F32)
    return jnp.einsum("bhkv,hg->bhvgk", s, eye).reshape(s.shape[0], HG_W, HG_W)


def _blocks_to_state(st):
    b = st.shape[0]
    return jnp.einsum("bhvhk->bhkv", st.reshape(b, HG_HEADS, HG_K, HG_HEADS, HG_K))


def _trunk(x, mod_all, w, rw, prm, pos0, ssm_re0, ssm_im0, hgrn0, cache_lat, cache_kr, page_table):
    bz, length, _ = x.shape
    m = bz * length
    depth = w["w_in"].shape[0]
    prompt = cache_lat is None
    abr, abi, cr, ci, lbs = prm
    if prompt:
        grp = _Group(bz, length, min(256, length), per_row=False)
        grp_in = _Group(bz, length, min(512, length), per_row=False)
        grp_f = _Group(bz, length, min(1024, length), per_row=False)
        tl = min(128, length)
        ch = min(32, length)
        tb = min(256, length)
    else:
        grp = _Group(bz, length, m, per_row=True)
        grp_in = grp_f = grp
        tl, ch, tb = length, length, length
    pos = pos0 + jnp.arange(length, dtype=jnp.int32)
    tables = _rope_tables(pos)
    if not prompt:
        tables = tuple(jnp.tile(t, (bz, 1)) for t in tables)
    cosq, sinq, cosk, sink = tables
    x = x.reshape(m, D_MODEL)
    lat_l, kr_l, sre_l, sim_l, hg_l = [], [], [], [], []
    row = lambda a: a.reshape(1, -1)
    for l in range(depth):
        mod = mod_all[l]
        mod = mod[:, None, :] if prompt else jnp.repeat(mod, length, axis=0)
        u, hz, gates, ql, qr, lat, latb, kr, krb = _in_proj(
            grp_in, x, mod, row(w["norm1_g"][l]), rw["w_in"][l], row(w["mla_q_norm_g"][l]), rw["wuq"][l],
            rw["wuk"][l], row(w["mla_kv_norm_g"][l]), cosq, sinq, cosk, sink)
        if prompt:
            h0r = jnp.zeros((bz, SSM_FLAT), F32)
            h0i = h0r
            u_tm = u.reshape(length, bz, SSM_WIDTH)
        else:
            h0r = ssm_re0[l].reshape(bz, SSM_FLAT)
            h0i = ssm_im0[l].reshape(bz, SSM_FLAT)
            u_tm = u.reshape(bz, length, SSM_WIDTH).transpose(1, 0, 2)
        ya, htr, hti = _s5(u_tm, h0r, h0i, row(abr[l]), row(abi[l]), row(cr[l]), row(ci[l]),
                           rw["bre"][l], rw["bim"][l], rw["cre"][l], rw["cim"][l], row(w["ssm_d"][l]),
                           rw["glu_w"][l], row(w["ssm_glu_b"][l]), tl)
        if prompt:
            ya = ya.reshape(length, bz * SSM_WIDTH)
        else:
            ya = ya.transpose(1, 0, 2).reshape(m, SSM_WIDTH)
        if prompt:
            yb = _attn_prompt(bz, length, ql, qr, latb, krb, rw["wuv"][l], grp.tm, min(512, length))
        else:
            yb = _attn_sample(l, page_table, ql, qr, latb, krb, cache_lat, cache_kr, rw["wuv"][l], bz, length)
        st0 = jnp.zeros((bz, HG_W, HG_W), F32) if prompt else _state_to_blocks(hgrn0[l])
        yc, st_t = _hgrn(bz, length, hz, st0, row(lbs[l]), row(rw["hg_norm"][l]), ch, tb)
        j = l // 2
        common = (grp, x, ya, yb, yc, gates, mod, row(w["norm2_g"][l]), rw["proj_a"][l], rw["proj_b"][l],
                  rw["proj_c"][l], rw["w_out"][l])
        if l % 2 == 0:
            x, h2 = _mix(*common)
            x = _ffn(grp_f, h2, x, mod, rw["ffn_g"], rw["ffn_u"], rw["ffn_d"], j, FF_DIM // 2)
        else:
            x, h2, gate, route, wts, cb, cnt = _mix(*common, router=(rw["rw"][j], rw["rb"][j]))
            experts = (rw["moe_g"], rw["moe_u"], rw["moe_d"], j, FF_DIM // 2)
            if prompt and grp.tm == MOE_TM:
                x = _moe_routed(bz, length, h2, x, mod, route, wts, cb, cnt, *experts)
            else:
                x = _moe(grp_f, h2, x, mod, gate, *experts)
        lat_l.append(lat.reshape(bz, length, MLA_KV_LORA))
        kr_l.append(kr.reshape(bz, length, MLA_ROPE))
        sre_l.append(htr.reshape(bz, SSM_GROUPS, SSM_STATE))
        sim_l.append(hti.reshape(bz, SSM_GROUPS, SSM_STATE))
        hg_l.append(_blocks_to_state(st_t))
    y = _final_norm(grp, x, row(w["final_norm_g"])).reshape(bz, length, D_MODEL)
    return y, jnp.stack(lat_l), jnp.stack(kr_l), jnp.stack(sre_l), jnp.stack(sim_l), jnp.stack(hg_l)


def kernel(x_prompt, x_sample, c_prompt, c_sample, cache_kv_latent, cache_k_rope, state_ssm_re, state_ssm_im,
           state_hgrn, page_table, ada_w, ada_b, norm1_g, norm2_g, w_in, ssm_a_re, ssm_a_im, ssm_log_dt,
           ssm_b_re, ssm_b_im, ssm_c_re, ssm_c_im, ssm_d, ssm_glu_w, ssm_glu_b, mla_q_norm_g, mla_w_uq,
           mla_kv_norm_g, mla_w_uk, mla_w_uv, hgrn_lb_logits, hgrn_norm_g, proj_a, proj_b, proj_c, w_out,
           ffn_w_gate, ffn_w_up, ffn_w_down, moe_router_w, moe_router_b, moe_w_gate, moe_w_up, moe_w_down,
           final_norm_g):
    w = dict(norm1_g=norm1_g, norm2_g=norm2_g, w_in=w_in, ssm_b_re=ssm_b_re, ssm_b_im=ssm_b_im,
             ssm_c_re=ssm_c_re, ssm_c_im=ssm_c_im, ssm_d=ssm_d, ssm_glu_w=ssm_glu_w, ssm_glu_b=ssm_glu_b,
             mla_q_norm_g=mla_q_norm_g, mla_w_uq=mla_w_uq, mla_kv_norm_g=mla_kv_norm_g, mla_w_uk=mla_w_uk,
             mla_w_uv=mla_w_uv, hgrn_norm_g=hgrn_norm_g, proj_a=proj_a, proj_b=proj_b, proj_c=proj_c,
             w_out=w_out, ffn_w_gate=ffn_w_gate, ffn_w_up=ffn_w_up, ffn_w_down=ffn_w_down,
             moe_router_w=moe_router_w, moe_router_b=moe_router_b, moe_w_gate=moe_w_gate,
             moe_w_up=moe_w_up, moe_w_down=moe_w_down, final_norm_g=final_norm_g)
    rw = _relayout_weights(w)
    prm = _prep_params(ssm_a_re, ssm_a_im, ssm_log_dt, hgrn_lb_logits)
    n_p = c_prompt.shape[0]
    mod_all = _ada_mod(jnp.concatenate([c_prompt, c_sample], axis=0), ada_w, ada_b)
    y_p, lat_p, kr_p, sre_p, sim_p, hg_p = _trunk(
        x_prompt, mod_all[:, :n_p], w, rw, prm, 0, None, None, None, None, None, None)
    past_len = page_table.shape[1] * cache_kv_latent.shape[2]
    y_s, lat_s, kr_s, sre_s, sim_s, hg_s = _trunk(
        x_sample, mod_all[:, n_p:], w, rw, prm, past_len, state_ssm_re, state_ssm_im, state_hgrn,
        cache_kv_latent, jnp.swapaxes(cache_k_rope, 2, 3), page_table)
    return (y_p, y_s, lat_p, kr_p, sre_p, sim_p, hg_p, lat_s, kr_s, sre_s, sim_s, hg_s)
```

```python
import functools

import jax
import jax.numpy as jnp
from jax import lax
from jax.experimental import pallas as pl
from jax.experimental.pallas import tpu as pltpu

F32 = jnp.float32
BF16 = jnp.bfloat16

D_MODEL = 1024
SSM_GROUPS = 16
SSM_GROUP_CH = 16
SSM_WIDTH = 256
SSM_STATE = 64
SSM_FLAT = SSM_GROUPS * SSM_STATE
MLA_HEADS = 8
MLA_NOPE = 64
MLA_ROPE = 32
MLA_V = 64
MLA_Q_LORA = 384
MLA_KV_LORA = 256
MLA_SCALE = (MLA_NOPE + MLA_ROPE) ** -0.5
ROPE_THETA = 10000.0
HG_HEADS = 4
HG_K = 64
HG_W = 256
FF_DIM = 2816
N_EXPERTS = 8
EPS = 1e-6

LANES = 128
SUBLANES = 8
VMEM_LIMIT = 52 * 1024 * 1024
VMEM_LIMIT_IN_PROJ = 58 * 1024 * 1024
HEADS_PER_STRIP = 2
PREFETCH_AHEAD = 2
DMA_SLOTS = PREFETCH_AHEAD + 1
MOE_TM = 256
MOE_T = 512

C_U, C_CKV, C_CQ, C_HZ, C_GATES, C_KR, C_KR_ROT, C_END = 0, 256, 512, 1024, 2048, 5120, 5248, 5376
ROPE_PAD = LANES


def _mm(a, b):
    return jnp.dot(a, b, preferred_element_type=F32)


def _mm_nt(a, b):
    return lax.dot_general(a, b, (((1,), (1,)), ((), ())), preferred_element_type=F32)


def _mm_tn(a, b):
    return lax.dot_general(a, b, (((0,), (0,)), ((), ())), preferred_element_type=F32)


def _mm_f32(a, b):
    return jnp.dot(a, b, preferred_element_type=F32, precision=lax.Precision.HIGHEST)


def _mm_split(a, b):
    ah, bh = a.astype(BF16), b.astype(BF16)
    al = (a - ah.astype(F32)).astype(BF16)
    bl = (b - bh.astype(F32)).astype(BF16)
    return _mm(ah, bh) + (_mm(ah, bl) + _mm(al, bh))


def _rms(x, g):
    return x * lax.rsqrt(jnp.mean(x * x, axis=-1, keepdims=True) + EPS) * g


def _params(sem, vmem_limit=VMEM_LIMIT):
    return pltpu.CompilerParams(dimension_semantics=sem, vmem_limit_bytes=vmem_limit)


def _const_spec(shape, single=False):
    nd = len(shape)
    if single:
        return pl.BlockSpec(shape, lambda *_: (0,) * nd, pipeline_mode=pl.Buffered(1))
    return pl.BlockSpec(shape, lambda *_: (0,) * nd)


class _Group:
    def __init__(self, bz, length, tm, per_row):
        self.bz, self.length, self.tm, self.per_row = bz, length, tm, per_row
        self.m = bz * length
        if per_row:
            self.nb = self.m // tm
            self.grid = (1, self.nb)
        else:
            self.nb = length // tm
            self.grid = (bz, self.nb)

    def rows(self, width):
        nb = self.nb
        return pl.BlockSpec((self.tm, width), lambda b, i, *_: (b * nb + i, 0))

    def mod(self, col):
        if self.per_row:
            return pl.BlockSpec((self.tm, D_MODEL), lambda b, i, *_: (i, col))
        return pl.BlockSpec((None, 1, D_MODEL), lambda b, i, *_: (b, 0, col))

    def pos(self, width):
        return pl.BlockSpec((self.tm, width), lambda b, i, *_: (i, 0))

    def tmajor(self, width):
        if self.per_row:
            return self.rows(width)
        return pl.BlockSpec((self.tm, width), lambda b, i, *_: (i, b))

    def tmajor_shape(self, width):
        return (self.m, width) if self.per_row else (self.length, self.bz * width)


def _param_kernel(ar_ref, ai_ref, ldt_ref, lbl_ref, abr_ref, abi_ref, cr_ref, ci_ref, lbs_ref):
    ar, ai = ar_ref[...], ai_ref[...]
    dt = jnp.exp(ldt_ref[...])
    mag = jnp.exp(dt * ar)
    abr, abi = mag * jnp.cos(dt * ai), mag * jnp.sin(dt * ai)
    den = ar * ar + ai * ai
    abr_ref[...] = abr
    abi_ref[...] = abi
    cr_ref[...] = ((abr - 1.0) * ar + abi * ai) / den
    ci_ref[...] = (abi * ar - (abr - 1.0) * ai) / den
    x = lbl_ref[...]
    e = jnp.exp(x - jnp.max(x, axis=0, keepdims=True))
    p = e / jnp.sum(e, axis=0, keepdims=True)
    rows, acc = [], p[0:1]
    for l in range(x.shape[0]):
        if l:
            acc = acc + p[l:l + 1]
        rows.append(acc - p[0:1])
    lbs_ref[...] = jnp.concatenate(rows, axis=0)


def _prep_params(a_re, a_im, log_dt, lb_logits):
    depth = a_re.shape[0]
    ar = a_re.reshape(depth, SSM_FLAT)
    ai = a_im.reshape(depth, SSM_FLAT)
    ldt = jnp.broadcast_to(log_dt[:, :, None], (depth, SSM_GROUPS, SSM_STATE)).reshape(depth, SSM_FLAT)
    flat = jax.ShapeDtypeStruct((depth, SSM_FLAT), F32)
    return pl.pallas_call(
        _param_kernel,
        out_shape=(flat, flat, flat, flat, jax.ShapeDtypeStruct((depth, HG_W), F32)),
        name="param_prep",
    )(ar, ai, ldt, lb_logits)


def _ada_kernel(c_ref, w_ref, b_ref, o_ref):
    c = c_ref[...]
    s = (c * jax.nn.sigmoid(c)).astype(BF16)
    o_ref[...] = _mm(s, w_ref[...].astype(BF16)) + b_ref[...]


def _ada_mod(c_all, ada_w, ada_b):
    depth, d, n = ada_w.shape
    rows = c_all.shape[0]
    tn = 1536
    return pl.pallas_call(
        _ada_kernel,
        grid=(depth, n // tn),
        in_specs=[pl.BlockSpec((rows, d), lambda l, j: (0, 0)),
                  pl.BlockSpec((None, d, tn), lambda l, j: (l, 0, j)),
                  pl.BlockSpec((None, 1, tn), lambda l, j: (l, 0, j))],
        out_specs=pl.BlockSpec((None, rows, tn), lambda l, j: (l, 0, j)),
        out_shape=jax.ShapeDtypeStruct((depth, rows, n), F32),
        compiler_params=_params(("arbitrary", "arbitrary")),
        name="ada_mod",
    )(c_all, ada_w, ada_b.reshape(depth, 1, n))


def _in_kernel(x_ref, sh_ref, sc_ref, g_ref, w_ref, qg_ref, wuq_ref, wuk_ref, kvg_ref,
               cq_ref, sq_ref, ck_ref, sk_ref,
               u_ref, hz_ref, gates_ref, ql_ref, qr_ref, lat_ref, latb_ref, kr_ref, krb_ref):
    x = x_ref[...]
    h = (_rms(x, g_ref[...]) * (1.0 + sc_ref[...]) + sh_ref[...]).astype(BF16)

    def seg(a, b):
        return _mm(h, w_ref[:, a:b])

    u_ref[...] = seg(C_U, C_CKV)
    hz_ref[...] = seg(C_HZ, C_GATES)
    gates_ref[...] = seg(C_GATES, C_KR)
    cqn = _rms(seg(C_CQ, C_CQ + MLA_Q_LORA), qg_ref[...]).astype(BF16)
    q = _mm(cqn, wuq_ref[...])
    n_nope = MLA_HEADS * MLA_NOPE
    n_rope = MLA_HEADS * ROPE_PAD
    ql = _mm(q[:, :n_nope].astype(BF16), wuk_ref[...]).astype(BF16)
    qr = (q[:, n_nope:n_nope + n_rope] * cq_ref[...] + q[:, n_nope + n_rope:] * sq_ref[...]).astype(BF16)
    for hd in range(MLA_HEADS):
        ql_ref[hd] = ql[:, hd * MLA_KV_LORA:(hd + 1) * MLA_KV_LORA]
        qr_ref[hd] = qr[:, hd * ROPE_PAD:(hd + 1) * ROPE_PAD]
    lat = _rms(seg(C_CKV, C_CQ), kvg_ref[...])
    lat_ref[...] = lat
    latb_ref[...] = lat.astype(BF16)
    kr = seg(C_KR, C_KR_ROT) * ck_ref[...] + seg(C_KR_ROT, C_END) * sk_ref[...]
    kr_ref[...] = kr[:, :MLA_ROPE]
    krb_ref[...] = kr.astype(BF16)


def _in_proj(grp, x, mod, g1, w_in, qg, wuq, wuk, kvg, cosq, sinq, cosk, sink):
    m, tm = grp.m, grp.tm
    n_rope = MLA_HEADS * ROPE_PAD
    nb = grp.nb

    def heads(width):
        return pl.BlockSpec((MLA_HEADS, tm, width), lambda b, i: (0, b * nb + i, 0))

    outs = [
        (grp.tmajor_shape(SSM_WIDTH), F32, grp.tmajor(SSM_WIDTH)),
        ((m, 4 * HG_W), F32, grp.rows(4 * HG_W)),
        ((m, 3 * D_MODEL), F32, grp.rows(3 * D_MODEL)),
        ((MLA_HEADS, m, MLA_KV_LORA), BF16, heads(MLA_KV_LORA)),
        ((MLA_HEADS, m, ROPE_PAD), BF16, heads(ROPE_PAD)),
        ((m, MLA_KV_LORA), F32, grp.rows(MLA_KV_LORA)),
        ((m, MLA_KV_LORA), BF16, grp.rows(MLA_KV_LORA)),
        ((m, MLA_ROPE), F32, grp.rows(MLA_ROPE)),
        ((m, ROPE_PAD), BF16, grp.rows(ROPE_PAD)),
    ]
    return pl.pallas_call(
        _in_kernel,
        grid=grp.grid,
        in_specs=[grp.rows(D_MODEL), grp.mod(0), grp.mod(1), _const_spec((1, D_MODEL)),
                  _const_spec(w_in.shape, True), _const_spec((1, MLA_Q_LORA)), _const_spec(wuq.shape, True),
                  _const_spec(wuk.shape, True), _const_spec((1, MLA_KV_LORA)),
                  grp.pos(n_rope), grp.pos(n_rope), grp.pos(ROPE_PAD), grp.pos(ROPE_PAD)],
        out_specs=[o[2] for o in outs],
        out_shape=[jax.ShapeDtypeStruct(o[0], o[1]) for o in outs],
        compiler_params=_params(("arbitrary", "arbitrary"), VMEM_LIMIT_IN_PROJ),
        name="in_proj",
    )(x, mod, mod, g1, w_in, qg, wuq, wuk, kvg, cosq, sinq, cosk, sink)


def _s5_kernel(u_ref, h0r_ref, h0i_ref, abr_ref, abi_ref, cr_ref, ci_ref, bre_ref, bim_ref,
               cre_ref, cim_ref, d_ref, gw_ref, gb_ref,
               y_ref, htr_ref, hti_ref, hr_s, hi_s, xr_s, xi_s, *, tl, bz):
    i = pl.program_id(0)

    @pl.when(i == 0)
    def _():
        hr_s[...] = h0r_ref[...]
        hi_s[...] = h0i_ref[...]

    u = u_ref[...].reshape(tl * bz, SSM_WIDTH)
    ub = u.astype(BF16)
    bur, bui = _mm(ub, bre_ref[...]), _mm(ub, bim_ref[...])
    cr, ci = cr_ref[...], ci_ref[...]
    xr_s[...] = cr * bur - ci * bui
    xi_s[...] = cr * bui + ci * bur
    abr, abi = abr_ref[...], abi_ref[...]

    def step(t, carry):
        hr, hi = carry
        r0 = pl.multiple_of(t * bz, bz)
        nr = abr * hr - abi * hi + xr_s[pl.ds(r0, bz), :]
        ni = abr * hi + abi * hr + xi_s[pl.ds(r0, bz), :]
        xr_s[pl.ds(r0, bz), :] = nr
        xi_s[pl.ds(r0, bz), :] = ni
        return nr, ni

    hr, hi = lax.fori_loop(0, tl, step, (hr_s[...], hi_s[...]))
    hr_s[...] = hr
    hi_s[...] = hi
    htr_ref[...] = hr
    hti_ref[...] = hi
    y = (_mm(xr_s[...].astype(BF16), cre_ref[...]) - _mm(xi_s[...].astype(BF16), cim_ref[...])
         + d_ref[...] * u)
    y = jax.nn.gelu(y)
    y = y * jax.nn.sigmoid(_mm(y.astype(BF16), gw_ref[...]) + gb_ref[...])
    y_ref[...] = y.reshape(tl, bz, SSM_WIDTH)


def _s5(u_tm, h0r, h0i, abr, abi, cr, ci, bre, bim, cre, cim, d, gw, gb, tl):
    length, bz, _ = u_tm.shape
    row = _const_spec((1, SSM_FLAT))
    st = _const_spec((bz, SSM_FLAT))
    blk = pl.BlockSpec((tl, bz, SSM_WIDTH), lambda i: (i, 0, 0))
    return pl.pallas_call(
        functools.partial(_s5_kernel, tl=tl, bz=bz),
        grid=(length // tl,),
        in_specs=[blk, st, st, row, row, row, row, _const_spec(bre.shape), _const_spec(bim.shape),
                  _const_spec(cre.shape), _const_spec(cim.shape), _const_spec((1, SSM_WIDTH)),
                  _const_spec(gw.shape), _const_spec((1, SSM_WIDTH))],
        out_specs=[blk, st, st],
        out_shape=[jax.ShapeDtypeStruct(u_tm.shape, F32), jax.ShapeDtypeStruct((bz, SSM_FLAT), F32),
                   jax.ShapeDtypeStruct((bz, SSM_FLAT), F32)],
        scratch_shapes=[pltpu.VMEM((bz, SSM_FLAT), F32), pltpu.VMEM((bz, SSM_FLAT), F32),
                        pltpu.VMEM((tl * bz, SSM_FLAT), F32), pltpu.VMEM((tl * bz, SSM_FLAT), F32)],
        compiler_params=_params(("arbitrary",)),
        name="s5_scan",
    )(u_tm, h0r, h0i, abr, abi, cr, ci, bre, bim, cre, cim, d, gw, gb)


def _softmax_step(s, keys, m_ref, l_ref, acc_ref):
    m_old = m_ref[...]
    m_new = jnp.maximum(m_old, jnp.max(s, axis=-1, keepdims=True))
    alpha = jnp.exp(m_old - m_new)
    p = jnp.exp(s - jnp.tile(m_new, (1, s.shape[-1] // LANES)))
    l_ref[...] = alpha * l_ref[...] + jnp.sum(p, axis=-1, keepdims=True)
    acc_ref[...] = (acc_ref[...] * jnp.tile(alpha, (1, acc_ref.shape[-1] // LANES))
                    + _mm(p.astype(BF16), keys))
    m_ref[...] = m_new


def _attn_prompt_kernel(ql_ref, qr_ref, lat_ref, kr_ref, wuv_ref, y_ref, m_s, l_s, acc_s, *, tq, tk):
    i = pl.program_id(1)
    rows = MLA_HEADS * tq
    m_s[...] = jnp.full(m_s.shape, -jnp.inf, F32)
    l_s[...] = jnp.zeros(l_s.shape, F32)
    acc_s[...] = jnp.zeros(acc_s.shape, F32)

    def block(k0, width, diagonal):
        kl = lat_ref[pl.ds(k0, width), :]
        kr = kr_ref[pl.ds(k0, width), :]
        per = MLA_HEADS if diagonal or width < tk else HEADS_PER_STRIP
        for h0 in range(0, MLA_HEADS, per):
            h1 = h0 + per
            n = per * tq
            s = (_mm_nt(ql_ref[h0:h1].reshape(n, MLA_KV_LORA), kl)
                 + _mm_nt(qr_ref[h0:h1].reshape(n, ROPE_PAD), kr)) * MLA_SCALE
            if diagonal:
                q_t = lax.broadcasted_iota(jnp.int32, (n, width), 0) % tq
                k_t = lax.broadcasted_iota(jnp.int32, (n, width), 1)
                s = jnp.where(k_t <= q_t, s, -jnp.inf)
            strip = pl.ds(h0 * tq, n)
            _softmax_step(s, kl, m_s.at[strip], l_s.at[strip], acc_s.at[strip])

    first = i * tq
    n_wide = first // tk

    def wide(j, carry):
        block(pl.multiple_of(j * tk, tk), tk, False)
        return carry

    def narrow(j, carry):
        block(pl.multiple_of(n_wide * tk + j * tq, tq), tq, False)
        return carry

    lax.fori_loop(0, n_wide, wide, 0)
    lax.fori_loop(0, (first - n_wide * tk) // tq, narrow, 0)
    block(pl.multiple_of(first, tq), tq, True)
    ctx = acc_s[...] / jnp.tile(l_s[...], (1, MLA_KV_LORA // LANES))
    ys = [_mm(ctx[h * tq:(h + 1) * tq, :].astype(BF16), wuv_ref[h]) for h in range(MLA_HEADS)]
    y_ref[...] = jnp.concatenate(ys, axis=-1).astype(BF16)


def _attn_prompt(grp_bz, length, ql, qr, latb, krb, wuv, tq, tk):
    nb = length // tq
    rows = MLA_HEADS * tq
    heads = lambda w: pl.BlockSpec((MLA_HEADS, tq, w), lambda b, i: (0, b * nb + i, 0))
    return pl.pallas_call(
        functools.partial(_attn_prompt_kernel, tq=tq, tk=tk),
        grid=(grp_bz, nb),
        in_specs=[heads(MLA_KV_LORA), heads(ROPE_PAD),
                  pl.BlockSpec((None, length, MLA_KV_LORA), lambda b, i: (b, 0, 0)),
                  pl.BlockSpec((None, length, ROPE_PAD), lambda b, i: (b, 0, 0)),
                  _const_spec(wuv.shape)],
        out_specs=pl.BlockSpec((tq, MLA_HEADS * MLA_V), lambda b, i: (b * nb + i, 0)),
        out_shape=jax.ShapeDtypeStruct((grp_bz * length, MLA_HEADS * MLA_V), BF16),
        scratch_shapes=[pltpu.VMEM((rows, LANES), F32), pltpu.VMEM((rows, LANES), F32),
                        pltpu.VMEM((rows, MLA_KV_LORA), F32)],
        compiler_params=_params(("arbitrary", "arbitrary")),
        name="attn_prompt",
    )(ql, qr, latb.reshape(grp_bz, length, MLA_KV_LORA), krb.reshape(grp_bz, length, ROPE_PAD), wuv)


def _attn_sample_kernel(pt_ref, ql_ref, qr_ref, nl_ref, nk_ref, wuv_ref, lat_hbm, krt_hbm, y_ref,
                        lat_buf, krt_buf, sem, m_s, l_s, acc_s, *, layer, pg, ns, lq):
    b, s_idx = pl.program_id(0), pl.program_id(1)
    n_steps = pl.num_programs(1)
    total = pl.num_programs(0) * n_steps
    step = b * n_steps + s_idx
    slot = step % DMA_SLOTS

    def page_copies(at_step):
        seq, st, sl = at_step // n_steps, at_step % n_steps, at_step % DMA_SLOTS
        cps = []
        for i in range(pg):
            page = pt_ref[seq, st * pg + i]
            cps.append(pltpu.make_async_copy(lat_hbm.at[layer, page], lat_buf.at[sl, i], sem.at[sl]))
            cps.append(pltpu.make_async_copy(krt_hbm.at[layer, page], krt_buf.at[sl, i], sem.at[sl]))
        return cps

    for first in range(PREFETCH_AHEAD):
        @pl.when((step == 0) & (first < total))
        def _(first=first):
            for cp in page_copies(first):
                cp.start()

    for cp in page_copies(step):
        cp.wait()

    @pl.when(step + PREFETCH_AHEAD < total)
    def _():
        for cp in page_copies(step + PREFETCH_AHEAD):
            cp.start()

    @pl.when(s_idx == 0)
    def _():
        m_s[...] = jnp.full(m_s.shape, -jnp.inf, F32)
        l_s[...] = jnp.zeros(l_s.shape, F32)
        acc_s[...] = jnp.zeros(acc_s.shape, F32)

    q_l, q_r = ql_ref[...], qr_ref[...]
    per = pg // ns
    for st in range(ns):
        keys = jnp.concatenate([lat_buf[slot, i].astype(BF16) for i in range(st * per, (st + 1) * per)], axis=0)
        krt = jnp.concatenate([krt_buf[slot, i].astype(BF16) for i in range(st * per, (st + 1) * per)], axis=1)
        s = (_mm_nt(q_l, keys) + _mm(q_r, krt)) * MLA_SCALE
        _softmax_step(s, keys, m_s.at[st], l_s.at[st], acc_s.at[st])

    @pl.when(s_idx == n_steps - 1)
    def _():
        rows = MLA_HEADS * lq
        nl, nk = nl_ref[...], nk_ref[...]
        s = (_mm_nt(q_l, nl) + _mm_nt(q_r, nk)) * MLA_SCALE
        q_t = lax.broadcasted_iota(jnp.int32, (rows, lq), 0) % lq
        k_t = lax.broadcasted_iota(jnp.int32, (rows, lq), 1)
        s = jnp.where(k_t <= q_t, s, -jnp.inf)
        m_old = m_s[0]
        m_new = jnp.maximum(m_old, jnp.max(s, axis=-1, keepdims=True))
        for st in range(1, ns):
            m_new = jnp.maximum(m_new, m_s[st])
        p = jnp.exp(s - m_new[:, :1])
        l_tot = jnp.sum(p, axis=-1, keepdims=True)
        acc = _mm(p.astype(BF16), nl)
        for st in range(ns):
            a = jnp.exp(m_s[st] - m_new)
            l_tot = l_tot + a * l_s[st]
            acc = acc + jnp.tile(a, (1, MLA_KV_LORA // LANES)) * acc_s[st]
        ctx = acc / jnp.tile(l_tot, (1, MLA_KV_LORA // LANES))
        ys = [_mm(ctx[h * lq:(h + 1) * lq, :].astype(BF16), wuv_ref[h]) for h in range(MLA_HEADS)]
        y_ref[...] = jnp.concatenate(ys, axis=-1).astype(BF16)


def _attn_sample(layer, page_table, ql, qr, latb, krb, cache_lat, cache_krt, wuv, bz, lq):
    n_pages = page_table.shape[1]
    pg = min(32, n_pages)
    ns = 2 if pg % 2 == 0 else 1
    page = cache_lat.shape[2]
    rows = MLA_HEADS * lq
    qlh = ql.reshape(MLA_HEADS, bz, lq, MLA_KV_LORA).transpose(1, 0, 2, 3).reshape(bz, rows, MLA_KV_LORA)
    qrh = qr[:, :, :MLA_ROPE].reshape(MLA_HEADS, bz, lq, MLA_ROPE).transpose(1, 0, 2, 3).reshape(
        bz, rows, MLA_ROPE)

    def seq(n, w):
        return pl.BlockSpec((None, n, w), lambda b, s, pt: (b, 0, 0))

    in_specs = [seq(rows, MLA_KV_LORA), seq(rows, MLA_ROPE), seq(lq, MLA_KV_LORA), seq(lq, MLA_ROPE),
                pl.BlockSpec(wuv.shape, lambda b, s, pt: (0, 0, 0)),
                pl.BlockSpec(memory_space=pl.ANY), pl.BlockSpec(memory_space=pl.ANY)]
    y = pl.pallas_call(
        functools.partial(_attn_sample_kernel, layer=layer, pg=pg, ns=ns, lq=lq),
        grid_spec=pltpu.PrefetchScalarGridSpec(
            num_scalar_prefetch=1,
            grid=(bz, n_pages // pg),
            in_specs=in_specs,
            out_specs=pl.BlockSpec((None, lq, MLA_HEADS * MLA_V), lambda b, s, pt: (b, 0, 0)),
            scratch_shapes=[pltpu.VMEM((DMA_SLOTS, pg, page, MLA_KV_LORA), F32),
                            pltpu.VMEM((DMA_SLOTS, pg, MLA_ROPE, page), F32),
                            pltpu.SemaphoreType.DMA((DMA_SLOTS,)),
                            pltpu.VMEM((ns, rows, LANES), F32), pltpu.VMEM((ns, rows, LANES), F32),
                            pltpu.VMEM((ns, rows, MLA_KV_LORA), F32)]),
        out_shape=jax.ShapeDtypeStruct((bz, lq, MLA_HEADS * MLA_V), BF16),
        compiler_params=_params(("arbitrary", "arbitrary")),
        name="attn_sample",
    )(page_table, qlh, qrh, latb.reshape(bz, lq, MLA_KV_LORA),
      krb[:, :MLA_ROPE].reshape(bz, lq, MLA_ROPE), wuv, cache_lat, cache_krt)
    return y.reshape(bz * lq, MLA_HEADS * MLA_V)


def _hgrn_kernel(hz_ref, st0_ref, lb_ref, ng_ref, y_ref, stt_ref, st_s, w_s, *, ch, tb, nseq):
    j = pl.program_id(1)

    @pl.when(j == 0)
    def _():
        st_s[...] = st0_ref[...]

    lb = lb_ref[...]
    log_lb, log1m_lb, one_m_lb = jnp.log(lb), jnp.log1p(-lb), 1.0 - lb
    tri = (lax.broadcasted_iota(jnp.int32, (ch, ch), 0)
           >= lax.broadcasted_iota(jnp.int32, (ch, ch), 1)).astype(F32)
    same_head = (lax.broadcasted_iota(jnp.int32, (HG_W, HG_W), 0) // HG_K
                 == lax.broadcasted_iota(jnp.int32, (HG_W, HG_W), 1) // HG_K)
    head_ones = same_head.astype(F32)
    head_ones_b = same_head.astype(BF16)
    ng = ng_ref[...]

    def chunk(c, carry):
        r0 = pl.multiple_of(c * ch, ch)
        for sq in range(nseq):
            chunk_of(sq, r0)
        return carry

    def chunk_of(sq, r0):
        q = hz_ref[sq, pl.ds(r0, ch), 0:HG_W]
        fp = hz_ref[sq, pl.ds(r0, ch), HG_W:2 * HG_W]
        v = hz_ref[sq, pl.ds(r0, ch), 2 * HG_W:3 * HG_W]
        g = hz_ref[sq, pl.ds(r0, ch), 3 * HG_W:4 * HG_W]
        logf = jnp.logaddexp(log_lb, log1m_lb + jax.nn.log_sigmoid(fp))
        k = one_m_lb * jax.nn.sigmoid(-fp)
        b = _mm_f32(tri, logf)
        st = st_s[sq]
        o = _mm_nt((q * jnp.exp(b)).astype(BF16), st.astype(BF16))
        starts, r = [], 0
        for s in range(ch):
            t0 = (s // SUBLANES) * SUBLANES
            t_idx = t0 + lax.broadcasted_iota(jnp.int32, (ch - t0, HG_W), 0)
            e = jnp.exp(jnp.where(t_idx >= s, b[t0:] - b[s:s + 1, :], -jnp.inf))
            w_s[sq, r:r + ch - t0, :] = q[t0:] * k[s:s + 1, :] * e
            starts.append(r)
            r += ch - t0
        att = _mm(w_s[sq].astype(BF16), head_ones_b)
        tiles = [o[t0:t0 + SUBLANES] for t0 in range(0, ch, SUBLANES)]
        for s in range(ch):
            for kt in range(s // SUBLANES, ch // SUBLANES):
                a0 = starts[s] + (kt - s // SUBLANES) * SUBLANES
                tiles[kt] = tiles[kt] + att[a0:a0 + SUBLANES, :] * v[s:s + 1, :]
        o = jnp.concatenate(tiles, axis=0)
        bl = b[ch - 1:ch, :]
        kd = k * jnp.exp(bl - b)
        upd = _mm_tn(v.astype(BF16), kd.astype(BF16))
        st_s[sq] = st * jnp.exp(bl) + jnp.where(same_head, upd, 0.0)
        ms = _mm_f32(o * o, head_ones) * (1.0 / HG_K)
        on = o * lax.rsqrt(ms + EPS) * ng
        y_ref[sq, pl.ds(r0, ch), :] = on * (g * jax.nn.sigmoid(g))

    lax.fori_loop(0, tb // ch, chunk, 0)

    @pl.when(j == pl.num_programs(1) - 1)
    def _():
        stt_ref[...] = st_s[...]


def _hgrn(bz, length, hz, st0, lb, ng, ch, tb):
    nb = length // tb
    nseq = 4 if bz % 4 == 0 else 1
    st_spec = pl.BlockSpec((nseq, HG_W, HG_W), lambda b, j: (b, 0, 0))
    pair_rows = sum(ch - s // SUBLANES * SUBLANES for s in range(ch))
    y, st_t = pl.pallas_call(
        functools.partial(_hgrn_kernel, ch=ch, tb=tb, nseq=nseq),
        grid=(bz // nseq, nb),
        in_specs=[pl.BlockSpec((nseq, tb, 4 * HG_W), lambda b, j: (b, j, 0)), st_spec,
                  _const_spec((1, HG_W)), _const_spec((1, HG_W))],
        out_specs=[pl.BlockSpec((nseq, tb, HG_W), lambda b, j: (b, j, 0)), st_spec],
        out_shape=[jax.ShapeDtypeStruct((bz, length, HG_W), F32),
                   jax.ShapeDtypeStruct((bz, HG_W, HG_W), F32)],
        scratch_shapes=[pltpu.VMEM((nseq, HG_W, HG_W), F32), pltpu.VMEM((nseq, pair_rows, HG_W), F32)],
        compiler_params=_params(("arbitrary", "arbitrary")),
        name="hgrn2",
    )(hz.reshape(bz, length, 4 * HG_W), st0, lb, ng)
    return y.reshape(bz * length, HG_W), st_t


def _mix_kernel(*refs, moe):
    if moe:
        (x_ref, ya_ref, yb_ref, yc_ref, gates_ref, g1_ref, sh2_ref, sc2_ref, n2_ref,
         pa_ref, pb_ref, pc_ref, wo_ref, rw_ref, rb_ref,
         xo_ref, h2_ref, gate_ref, route_ref, wts_ref, cb_ref, cnt_ref, run_s) = refs
    else:
        (x_ref, ya_ref, yb_ref, yc_ref, gates_ref, g1_ref, sh2_ref, sc2_ref, n2_ref,
         pa_ref, pb_ref, pc_ref, wo_ref, xo_ref, h2_ref) = refs
    d = D_MODEL
    mix = (jax.nn.sigmoid(gates_ref[:, 0:d]) * _mm(ya_ref[...].astype(BF16), pa_ref[...])
           + jax.nn.sigmoid(gates_ref[:, d:2 * d]) * _mm(yb_ref[...], pb_ref[...])
           + jax.nn.sigmoid(gates_ref[:, 2 * d:3 * d]) * _mm(yc_ref[...].astype(BF16), pc_ref[...]))
    xo = x_ref[...] + g1_ref[...] * _mm(mix.astype(BF16), wo_ref[...])
    xo_ref[...] = xo
    h2 = _rms(xo, n2_ref[...]) * (1.0 + sc2_ref[...]) + sh2_ref[...]
    h2_ref[...] = h2.astype(BF16)
    if moe:
        logits = _mm_split(h2, rw_ref[...]) + rb_ref[...]
        lane = lax.broadcasted_iota(jnp.int32, logits.shape, 1)
        lg = jnp.where(lane < N_EXPERTS, logits, -jnp.inf)
        m1 = jnp.max(lg, axis=-1, keepdims=True)
        i1 = jnp.min(jnp.where(lg == m1, lane, LANES), axis=-1, keepdims=True)
        lg2 = jnp.where(lane == i1, -jnp.inf, lg)
        m2 = jnp.max(lg2, axis=-1, keepdims=True)
        i2 = jnp.min(jnp.where(lg2 == m2, lane, LANES), axis=-1, keepdims=True)
        e2 = jnp.exp(m2 - m1)
        den = 1.0 + e2
        w1, w2 = 1.0 / den, e2 / den
        gate_ref[...] = jnp.where(lane == i1, w1, 0.0) + jnp.where(lane == i2, w2, 0.0)
        @pl.when((pl.program_id(0) == 0) & (pl.program_id(1) == 0))
        def _():
            run_s[...] = jnp.zeros(run_s.shape, F32)

        tm = logits.shape[0]
        sel = jnp.where(lane == i1, 1.0, 0.0) + jnp.where(lane == i2, 1.0, 0.0)
        before = (lax.broadcasted_iota(jnp.int32, (tm, tm), 0) > lax.broadcasted_iota(jnp.int32, (tm, tm), 1))
        rank = _mm(jnp.where(before, 1.0, 0.0).astype(BF16), sel.astype(BF16))
        r1 = jnp.sum(jnp.where(lane == i1, rank, 0.0), axis=-1, keepdims=True).astype(jnp.int32)
        r2 = jnp.sum(jnp.where(lane == i2, rank, 0.0), axis=-1, keepdims=True).astype(jnp.int32)
        route_ref[...] = jnp.where(lane == 0, i1 * tm + r1, jnp.where(lane == 1, i2 * tm + r2, 0))
        wts_ref[...] = jnp.where(lane == 0, w1, jnp.where(lane == 1, w2, 0.0))
        cb_ref[...] = run_s[...]
        total = run_s[...] + jnp.sum(sel, axis=0, keepdims=True)
        run_s[...] = total
        cnt_ref[...] = total


def _mix(grp, x, ya, yb, yc, gates, mod, n2, pa, pb, pc, wo, router=None):
    m = grp.m
    moe = router is not None
    in_specs = [grp.rows(D_MODEL), grp.tmajor(SSM_WIDTH), grp.rows(MLA_HEADS * MLA_V), grp.rows(HG_W),
                grp.rows(3 * D_MODEL), grp.mod(2), grp.mod(3), grp.mod(4), _const_spec((1, D_MODEL)),
                _const_spec(pa.shape), _const_spec(pb.shape), _const_spec(pc.shape), _const_spec(wo.shape)]
    args = [x, ya, yb, yc, gates, mod, mod, mod, n2, pa, pb, pc, wo]
    out_specs = [grp.rows(D_MODEL), grp.rows(D_MODEL)]
    out_shape = [jax.ShapeDtypeStruct((m, D_MODEL), F32), jax.ShapeDtypeStruct((m, D_MODEL), BF16)]
    if moe:
        in_specs += [_const_spec(router[0].shape), _const_spec(router[1].shape)]
        args += list(router)
        nb, n_tiles = grp.nb, m // grp.tm
        out_specs += [grp.rows(LANES), grp.rows(LANES), grp.rows(LANES),
                      pl.BlockSpec((None, 1, LANES), lambda b, i: (b * nb + i, 0, 0)), _const_spec((1, LANES))]
        out_shape += [jax.ShapeDtypeStruct((m, LANES), F32), jax.ShapeDtypeStruct((m, LANES), jnp.int32),
                      jax.ShapeDtypeStruct((m, LANES), F32), jax.ShapeDtypeStruct((n_tiles, 1, LANES), F32),
                      jax.ShapeDtypeStruct((1, LANES), F32)]
    return pl.pallas_call(
        functools.partial(_mix_kernel, moe=moe),
        grid=grp.grid, in_specs=in_specs, out_specs=out_specs, out_shape=out_shape,
        scratch_shapes=[pltpu.VMEM((1, LANES), F32)] if moe else [],
        compiler_params=_params(("arbitrary", "arbitrary")),
        name="mix_moe" if moe else "mix",
    )(*args)


def _swiglu_tile(h2_ref, wg_ref, wu_ref, wd_ref):
    h = h2_ref[...]
    g = _mm(h, wg_ref[...])
    u = _mm(h, wu_ref[...])
    a = (g * jax.nn.sigmoid(g) * u).astype(BF16)
    return _mm(a, wd_ref[...])


def _ffn_kernel(h2_ref, x_ref, g2_ref, wg_ref, wu_ref, wd_ref, o_ref, acc_s):
    f = pl.program_id(2)

    @pl.when(f == 0)
    def _():
        acc_s[...] = jnp.zeros(acc_s.shape, F32)

    acc_s[...] += _swiglu_tile(h2_ref, wg_ref, wu_ref, wd_ref)

    @pl.when(f == pl.num_programs(2) - 1)
    def _():
        o_ref[...] = x_ref[...] + g2_ref[...] * acc_s[...]


def _ffn(grp, h2, x, mod, wg, wu, wd, j, tf):
    nf = FF_DIM // tf
    return pl.pallas_call(
        _ffn_kernel,
        grid=grp.grid + (nf,),
        in_specs=[grp.rows(D_MODEL), grp.rows(D_MODEL), grp.mod(5),
                  pl.BlockSpec((None, D_MODEL, tf), lambda b, i, f: (j, 0, f)),
                  pl.BlockSpec((None, D_MODEL, tf), lambda b, i, f: (j, 0, f)),
                  pl.BlockSpec((None, tf, D_MODEL), lambda b, i, f: (j, f, 0))],
        out_specs=grp.rows(D_MODEL),
        out_shape=jax.ShapeDtypeStruct((grp.m, D_MODEL), F32),
        scratch_shapes=[pltpu.VMEM((grp.tm, D_MODEL), F32)],
        compiler_params=_params(("arbitrary", "arbitrary", "arbitrary")),
        name="ffn_dense",
    )(h2, x, mod, wg, wu, wd)


def _moe_kernel(h2_ref, x_ref, g2_ref, gate_ref, wg_ref, wu_ref, wd_ref, o_ref, acc_s):
    e, f = pl.program_id(2), pl.program_id(3)

    @pl.when((e == 0) & (f == 0))
    def _():
        acc_s[...] = jnp.zeros(acc_s.shape, F32)

    gate = gate_ref[...]
    lane = lax.broadcasted_iota(jnp.int32, gate.shape, 1)
    gcol = jnp.sum(jnp.where(lane == e, gate, 0.0), axis=-1, keepdims=True)
    acc_s[...] += gcol * _swiglu_tile(h2_ref, wg_ref, wu_ref, wd_ref)

    @pl.when((e == pl.num_programs(2) - 1) & (f == pl.num_programs(3) - 1))
    def _():
        o_ref[...] = x_ref[...] + g2_ref[...] * acc_s[...]


def _moe(grp, h2, x, mod, gate, wg, wu, wd, j, tf):
    nf = FF_DIM // tf
    return pl.pallas_call(
        _moe_kernel,
        grid=grp.grid + (N_EXPERTS, nf),
        in_specs=[grp.rows(D_MODEL), grp.rows(D_MODEL), grp.mod(5), grp.rows(LANES),
                  pl.BlockSpec((None, None, D_MODEL, tf), lambda b, i, e, f: (j, e, 0, f)),
                  pl.BlockSpec((None, None, D_MODEL, tf), lambda b, i, e, f: (j, e, 0, f)),
                  pl.BlockSpec((None, None, tf, D_MODEL), lambda b, i, e, f: (j, e, f, 0))],
        out_specs=grp.rows(D_MODEL),
        out_shape=jax.ShapeDtypeStruct((grp.m, D_MODEL), F32),
        scratch_shapes=[pltpu.VMEM((grp.tm, D_MODEL), F32)],
        compiler_params=_params(("arbitrary",) * 4),
        name="ffn_moe",
    )(h2, x, mod, gate, wg, wu, wd)


def _route_plan(route, cb, cnt, n_tok):
    cnt8 = cnt[0, :N_EXPERTS].astype(jnp.int32)
    cb8 = cb[:, 0, :N_EXPERTS].astype(jnp.int32)
    p1, p2 = route[:, 0], route[:, 1]
    gp = (cnt8 + MOE_T - 1) // MOE_T * MOE_T
    off_end = jnp.cumsum(gp)
    off = off_end - gp
    start = (off[None, :] + cb8).reshape(-1)
    nbe = (jnp.concatenate([cb8[1:], cnt8[None]], axis=0) - cb8).reshape(-1)
    n_tiles = (2 * n_tok + N_EXPERTS * (MOE_T - 1)) // MOE_T
    idx = jnp.arange(n_tiles, dtype=jnp.int32)
    valid = idx * MOE_T < off_end[-1]
    blk = jnp.where(valid, idx, off_end[-1] // MOE_T - 1)
    te = jnp.sum((blk * MOE_T)[:, None] >= off_end[None, :], axis=1).astype(jnp.int32)
    fill = jnp.concatenate([off + cnt8, gp - cnt8, off_end[-1:] // MOE_T])
    return p1, p2, start, nbe, fill, te, blk, valid.astype(jnp.int32), n_tiles


def _segment_copies(n, src_of, dst_of, sem, max_rows=MOE_TM):
    out = []
    for k in range(max_rows.bit_length()):
        size = 1 << k
        pos = (n >> (k + 1)) << (k + 1)
        out.append((((n >> k) & 1) == 1, pltpu.make_async_copy(src_of(pos, size), dst_of(pos, size), sem)))
    return out


def _moe_gather_kernel(p1_ref, p2_ref, start_ref, nbe_ref, fill_ref, x_ref, xs_ref, buf, zeros_s, sem, zsem,
                       *, n_tiles):
    b = pl.program_id(0)
    half = b % 2

    def tok(i, carry):
        row = x_ref[i]
        buf[half, p1_ref[b * MOE_TM + i]] = row
        buf[half, p2_ref[b * MOE_TM + i]] = row
        return carry

    lax.fori_loop(0, MOE_TM, tok, 0, unroll=8)

    def tile_copies(tile):
        copies = []
        for e in range(N_EXPERTS):
            s = start_ref[tile * N_EXPERTS + e]
            copies += _segment_copies(
                nbe_ref[tile * N_EXPERTS + e],
                lambda pos, size, e=e: buf.at[tile % 2, pl.ds(e * MOE_TM + pos, size)],
                lambda pos, size, s=s: xs_ref.at[pl.ds(s + pos, size)], sem.at[tile % 2])
        return copies

    @pl.when(b > 0)
    def _():
        for cond, cp in tile_copies(b - 1):
            pl.when(cond)(cp.wait)

    mine = tile_copies(b)
    for cond, cp in mine:
        pl.when(cond)(cp.start)

    @pl.when(b == pl.num_programs(0) - 1)
    def _():
        for cond, cp in mine:
            pl.when(cond)(cp.wait)
        zeros_s[...] = jnp.zeros(zeros_s.shape, F32)
        fills = []
        for e in range(N_EXPERTS):
            s = fill_ref[e]
            fills += _segment_copies(
                fill_ref[N_EXPERTS + e],
                lambda pos, size: zeros_s.at[pl.ds(0, size)],
                lambda pos, size, s=s: xs_ref.at[pl.ds(s + pos, size)], zsem, max_rows=MOE_T - 1)
        for j in range(n_tiles):
            fills.append((j >= fill_ref[2 * N_EXPERTS],
                          pltpu.make_async_copy(zeros_s, xs_ref.at[pl.ds(j * MOE_T, MOE_T)], zsem)))
        for cond, cp in fills:
            pl.when(cond)(cp.start)
        for cond, cp in fills:
            pl.when(cond)(cp.wait)


def _moe_gather(x3, p1, p2, start, nbe, fill, n_tiles):
    n_tok = x3.shape[0]
    tile = pl.BlockSpec((MOE_TM, SUBLANES, LANES), lambda b, *_: (b, 0, 0))
    return pl.pallas_call(
        functools.partial(_moe_gather_kernel, n_tiles=n_tiles),
        grid_spec=pltpu.PrefetchScalarGridSpec(
            num_scalar_prefetch=5, grid=(n_tok // MOE_TM,),
            in_specs=[tile],
            out_specs=pl.BlockSpec(memory_space=pl.ANY),
            scratch_shapes=[pltpu.VMEM((2, N_EXPERTS * MOE_TM, SUBLANES, LANES), F32),
                            pltpu.VMEM((MOE_T, SUBLANES, LANES), F32),
                            pltpu.SemaphoreType.DMA((2,)), pltpu.SemaphoreType.DMA]),
        out_shape=jax.ShapeDtypeStruct((n_tiles * MOE_T, SUBLANES, LANES), F32),
        compiler_params=_params(("arbitrary",)),
        name="moe_gather",
    )(p1, p2, start, nbe, fill, x3)


def _moe_ffn_kernel(te_ref, blk_ref, valid_ref, x_ref, wg_ref, wu_ref, wd_ref, y_ref, xb_s, acc_s):
    del te_ref, blk_ref
    j, f = pl.program_id(0), pl.program_id(1)

    @pl.when(valid_ref[j] == 1)
    def _():
        @pl.when(f == 0)
        def _():
            xb_s[...] = x_ref[...].astype(BF16)
            acc_s[...] = jnp.zeros(acc_s.shape, F32)

        acc_s[...] += _swiglu_tile(xb_s, wg_ref, wu_ref, wd_ref)

        @pl.when(f == pl.num_programs(1) - 1)
        def _():
            y_ref[...] = acc_s[...]

    @pl.when((valid_ref[j] == 0) & (f == pl.num_programs(1) - 1))
    def _():
        y_ref[...] = jnp.zeros(y_ref.shape, F32)


def _moe_ffn(xs, te, blk, valid, wg, wu, wd, layer, tf):
    n_rows = xs.shape[0]
    nf = FF_DIM // tf
    rows = pl.BlockSpec((MOE_T, D_MODEL), lambda j, f, te, blk, valid: (blk[j], 0))

    def fsel(j, f, valid):
        return jnp.where(valid[j] == 1, f, nf - 1)

    return pl.pallas_call(
        _moe_ffn_kernel,
        grid_spec=pltpu.PrefetchScalarGridSpec(
            num_scalar_prefetch=3, grid=(n_rows // MOE_T, nf),
            in_specs=[rows,
                      pl.BlockSpec((None, None, D_MODEL, tf),
                                   lambda j, f, te, blk, valid: (layer, te[j], 0, fsel(j, f, valid))),
                      pl.BlockSpec((None, None, D_MODEL, tf),
                                   lambda j, f, te, blk, valid: (layer, te[j], 0, fsel(j, f, valid))),
                      pl.BlockSpec((None, None, tf, D_MODEL),
                                   lambda j, f, te, blk, valid: (layer, te[j], fsel(j, f, valid), 0))],
            out_specs=pl.BlockSpec((MOE_T, D_MODEL), lambda j, f, te, blk, valid: (j, 0)),
            scratch_shapes=[pltpu.VMEM((MOE_T, D_MODEL), BF16), pltpu.VMEM((MOE_T, D_MODEL), F32)]),
        out_shape=jax.ShapeDtypeStruct((n_rows, D_MODEL), F32),
        compiler_params=_params(("arbitrary", "arbitrary")),
        name="moe_ffn",
    )(te, blk, valid, xs, wg, wu, wd)


def _moe_combine_kernel(p1_ref, p2_ref, start_ref, nbe_ref, w1_ref, w2_ref, ys_ref, x_ref, g2_ref, o_ref,
                        buf, sem):
    b = pl.program_id(0)
    n_tiles = pl.num_programs(0)
    slot = b % DMA_SLOTS

    def tile_copies(tile):
        copies = []
        for e in range(N_EXPERTS):
            s = start_ref[tile * N_EXPERTS + e]
            copies += _segment_copies(
                nbe_ref[tile * N_EXPERTS + e],
                lambda pos, size, s=s: ys_ref.at[pl.ds(s + pos, size)],
                lambda pos, size, e=e: buf.at[tile % DMA_SLOTS, pl.ds(e * MOE_TM + pos, size)],
                sem.at[tile % DMA_SLOTS])
        return copies

    for first in range(PREFETCH_AHEAD):
        @pl.when((b == 0) & (first < n_tiles))
        def _(first=first):
            for cond, cp in tile_copies(first):
                pl.when(cond)(cp.start)

    for cond, cp in tile_copies(b):
        pl.when(cond)(cp.wait)

    @pl.when(b + PREFETCH_AHEAD < n_tiles)
    def _():
        for cond, cp in tile_copies(b + PREFETCH_AHEAD):
            pl.when(cond)(cp.start)

    g2 = g2_ref[...]

    def tok(i, carry):
        t = b * MOE_TM + i
        o_ref[i] = x_ref[i] + g2 * (w1_ref[t] * buf[slot, p1_ref[t]] + w2_ref[t] * buf[slot, p2_ref[t]])
        return carry

    lax.fori_loop(0, MOE_TM, tok, 0, unroll=8)


def _moe_combine(ys3, x3, g2_3, p1, p2, start, nbe, w1, w2, seq_len):
    n_tok = x3.shape[0]
    per_seq = seq_len // MOE_TM
    tile = pl.BlockSpec((MOE_TM, SUBLANES, LANES), lambda b, *_: (b, 0, 0))
    return pl.pallas_call(
        _moe_combine_kernel,
        grid_spec=pltpu.PrefetchScalarGridSpec(
            num_scalar_prefetch=6, grid=(n_tok // MOE_TM,),
            in_specs=[pl.BlockSpec(memory_space=pl.ANY), tile,
                      pl.BlockSpec((None, SUBLANES, LANES), lambda b, *_: (b // per_seq, 0, 0))],
            out_specs=tile,
            scratch_shapes=[pltpu.VMEM((DMA_SLOTS, N_EXPERTS * MOE_TM, SUBLANES, LANES), F32),
                            pltpu.SemaphoreType.DMA((DMA_SLOTS,))]),
        out_shape=jax.ShapeDtypeStruct(x3.shape, F32),
        compiler_params=_params(("arbitrary",)),
        name="moe_combine",
    )(p1, p2, start, nbe, w1, w2, ys3, x3, g2_3)


def _moe_routed(bz, length, h2, x, mod, route, wts, cb, cnt, wg, wu, wd, layer, tf):
    m = bz * length
    as_tiles = lambda a: a.reshape(a.shape[0], SUBLANES, LANES)
    p1, p2, start, nbe, fill, te, blk, valid, n_tiles = _route_plan(route, cb, cnt, m)
    xs3 = _moe_gather(as_tiles(h2.astype(F32)), p1, p2, start, nbe, fill, n_tiles)
    ys = _moe_ffn(xs3.reshape(n_tiles * MOE_T, D_MODEL), te, blk, valid, wg, wu, wd, layer, tf)
    g2_3 = mod[:, 0, 5 * D_MODEL:6 * D_MODEL].reshape(bz, SUBLANES, LANES)
    out3 = _moe_combine(as_tiles(ys), as_tiles(x), g2_3, p1, p2, start, nbe, wts[:, 0], wts[:, 1], length)
    return out3.reshape(m, D_MODEL)


def _final_kernel(x_ref, g_ref, o_ref):
    o_ref[...] = _rms(x_ref[...], g_ref[...])


def _final_norm(grp, x, g):
    return pl.pallas_call(
        _final_kernel, grid=grp.grid,
        in_specs=[grp.rows(D_MODEL), _const_spec((1, D_MODEL))],
        out_specs=grp.rows(D_MODEL),
        out_shape=jax.ShapeDtypeStruct((grp.m, D_MODEL), F32),
        compiler_params=_params(("arbitrary", "arbitrary")),
        name="final_norm",
    )(x, g)


def _relayout_weights(w):
    bf = lambda a: a.astype(BF16)
    depth = w["w_in"].shape[0]
    wi = w["w_in"]
    o = [0, 256, 640, 896, 928, 1184, 1440, 1696, 1952, 5024]
    kr = wi[:, :, o[3]:o[4]]
    half = MLA_ROPE // 2
    kr_rot = jnp.concatenate([-kr[..., half:], kr[..., :half]], axis=-1)
    zeros = lambda n: jnp.zeros(wi.shape[:2] + (n,), wi.dtype)
    w_in = bf(jnp.concatenate([
        wi[:, :, o[0]:o[1]], wi[:, :, o[2]:o[3]], wi[:, :, o[1]:o[2]], zeros(C_HZ - C_CQ - MLA_Q_LORA),
        wi[:, :, o[4]:o[8]], wi[:, :, o[8]:o[9]], kr, zeros(ROPE_PAD - MLA_ROPE),
        kr_rot, zeros(ROPE_PAD - MLA_ROPE)], axis=-1))
    uq = w["mla_w_uq"].reshape(depth, MLA_Q_LORA, MLA_HEADS, MLA_NOPE + MLA_ROPE)
    nope = uq[..., :MLA_NOPE].reshape(depth, MLA_Q_LORA, MLA_HEADS * MLA_NOPE)
    r1 = uq[..., MLA_NOPE:MLA_NOPE + half]
    r2 = uq[..., MLA_NOPE + half:]
    pad = jnp.zeros(uq.shape[:3] + (ROPE_PAD - MLA_ROPE,), uq.dtype)
    rope = jnp.concatenate([r1, r2, pad], axis=-1).reshape(depth, MLA_Q_LORA, MLA_HEADS * ROPE_PAD)
    rope_rot = jnp.concatenate([-r2, r1, pad], axis=-1).reshape(depth, MLA_Q_LORA, MLA_HEADS * ROPE_PAD)
    wuq = bf(jnp.concatenate([nope, rope, rope_rot], axis=-1))
    eye_h = jnp.eye(MLA_HEADS, dtype=F32)
    uk = w["mla_w_uk"].reshape(depth, MLA_KV_LORA, MLA_HEADS, MLA_NOPE)
    wuk = bf(jnp.einsum("lrhd,hg->lhdgr", uk, eye_h).reshape(
        depth, MLA_HEADS * MLA_NOPE, MLA_HEADS * MLA_KV_LORA))
    wuv = bf(w["mla_w_uv"].reshape(depth, MLA_KV_LORA, MLA_HEADS, MLA_V).transpose(0, 2, 1, 3))
    eye_g = jnp.eye(SSM_GROUPS, dtype=F32)
    b_blk = lambda b: bf(jnp.einsum("lgpc,gh->lgchp", b, eye_g).reshape(depth, SSM_WIDTH, SSM_FLAT))
    c_blk = lambda c: bf(jnp.einsum("lgcp,gh->lgphc", c, eye_g).reshape(depth, SSM_FLAT, SSM_WIDTH))
    n_moe = w["moe_router_w"].shape[0]
    rw = jnp.concatenate([w["moe_router_w"],
                          jnp.zeros((n_moe, D_MODEL, LANES - N_EXPERTS), F32)], axis=-1)
    rb = jnp.concatenate([w["moe_router_b"], jnp.zeros((n_moe, LANES - N_EXPERTS), F32)],
                         axis=-1).reshape(n_moe, 1, LANES)
    return dict(
        w_in=w_in, wuq=wuq, wuk=wuk, wuv=wuv,
        bre=b_blk(w["ssm_b_re"]), bim=b_blk(w["ssm_b_im"]),
        cre=c_blk(w["ssm_c_re"]), cim=c_blk(w["ssm_c_im"]),
        glu_w=bf(w["ssm_glu_w"]), proj_a=bf(w["proj_a"]), proj_b=bf(w["proj_b"]),
        proj_c=bf(w["proj_c"]), w_out=bf(w["w_out"]), rw=rw, rb=rb,
        ffn_g=bf(w["ffn_w_gate"]), ffn_u=bf(w["ffn_w_up"]), ffn_d=bf(w["ffn_w_down"]),
        moe_g=bf(w["moe_w_gate"]), moe_u=bf(w["moe_w_up"]), moe_d=bf(w["moe_w_down"]),
        hg_norm=jnp.tile(w["hgrn_norm_g"], (1, HG_HEADS)))


def _rope_tables(pos):
    half = MLA_ROPE // 2
    freq = ROPE_THETA ** (-jnp.arange(half, dtype=F32) / half)
    ang = pos.astype(F32)[:, None] * freq[None, :]
    cos, sin = jnp.cos(ang), jnp.sin(ang)
    pad = jnp.zeros((pos.shape[0], ROPE_PAD - MLA_ROPE), F32)
    cosk = jnp.concatenate([cos, cos, pad], axis=-1)
    sink = jnp.concatenate([sin, sin, pad], axis=-1)
    return jnp.tile(cosk, (1, MLA_HEADS)), jnp.tile(sink, (1, MLA_HEADS)), cosk, sink


def _state_to_blocks(s):
    eye = jnp.eye(HG_HEADS, dtype=F32)
    return jnp.einsum("bhkv,hg->bhvgk", s, eye).reshape(s.shape[0], HG_W, HG_W)


def _blocks_to_state(st):
    b = st.shape[0]
    return jnp.einsum("bhvhk->bhkv", st.reshape(b, HG_HEADS, HG_K, HG_HEADS, HG_K))


def _trunk(x, mod_all, w, rw, prm, pos0, ssm_re0, ssm_im0, hgrn0, cache_lat, cache_kr, page_table):
    bz, length, _ = x.shape
    m = bz * length
    depth = w["w_in"].shape[0]
    prompt = cache_lat is None
    abr, abi, cr, ci, lbs = prm
    if prompt:
        grp = _Group(bz, length, min(256, length), per_row=False)
        grp_in = _Group(bz, length, min(512, length), per_row=False)
        grp_f = _Group(bz, length, min(1024, length), per_row=False)
        tl = min(128, length)
        ch = min(32, length)
        tb = min(256, length)
    else:
        grp = _Group(bz, length, m, per_row=True)
        grp_in = grp_f = grp
        tl, ch, tb = length, length, length
    pos = pos0 + jnp.arange(length, dtype=jnp.int32)
    tables = _rope_tables(pos)
    if not prompt:
        tables = tuple(jnp.tile(t, (bz, 1)) for t in tables)
    cosq, sinq, cosk, sink = tables
    x = x.reshape(m, D_MODEL)
    lat_l, kr_l, sre_l, sim_l, hg_l = [], [], [], [], []
    row = lambda a: a.reshape(1, -1)
    for l in range(depth):
        mod = mod_all[l]
        mod = mod[:, None, :] if prompt else jnp.repeat(mod, length, axis=0)
        u, hz, gates, ql, qr, lat, latb, kr, krb = _in_proj(
            grp_in, x, mod, row(w["norm1_g"][l]), rw["w_in"][l], row(w["mla_q_norm_g"][l]), rw["wuq"][l],
            rw["wuk"][l], row(w["mla_kv_norm_g"][l]), cosq, sinq, cosk, sink)
        if prompt:
            h0r = jnp.zeros((bz, SSM_FLAT), F32)
            h0i = h0r
            u_tm = u.reshape(length, bz, SSM_WIDTH)
        else:
            h0r = ssm_re0[l].reshape(bz, SSM_FLAT)
            h0i = ssm_im0[l].reshape(bz, SSM_FLAT)
            u_tm = u.reshape(bz, length, SSM_WIDTH).transpose(1, 0, 2)
        ya, htr, hti = _s5(u_tm, h0r, h0i, row(abr[l]), row(abi[l]), row(cr[l]), row(ci[l]),
                           rw["bre"][l], rw["bim"][l], rw["cre"][l], rw["cim"][l], row(w["ssm_d"][l]),
                           rw["glu_w"][l], row(w["ssm_glu_b"][l]), tl)
        if prompt:
            ya = ya.reshape(length, bz * SSM_WIDTH)
        else:
            ya = ya.transpose(1, 0, 2).reshape(m, SSM_WIDTH)
        if prompt:
            yb = _attn_prompt(bz, length, ql, qr, latb, krb, rw["wuv"][l], grp.tm, min(512, length))
        else:
            yb = _attn_sample(l, page_table, ql, qr, latb, krb, cache_lat, cache_kr, rw["wuv"][l], bz, length)
        st0 = jnp.zeros((bz, HG_W, HG_W), F32) if prompt else _state_to_blocks(hgrn0[l])
        yc, st_t = _hgrn(bz, length, hz, st0, row(lbs[l]), row(rw["hg_norm"][l]), ch, tb)
        j = l // 2
        common = (grp, x, ya, yb, yc, gates, mod, row(w["norm2_g"][l]), rw["proj_a"][l], rw["proj_b"][l],
                  rw["proj_c"][l], rw["w_out"][l])
        if l % 2 == 0:
            x, h2 = _mix(*common)
            x = _ffn(grp_f, h2, x, mod, rw["ffn_g"], rw["ffn_u"], rw["ffn_d"], j, FF_DIM // 2)
        else:
            x, h2, gate, route, wts, cb, cnt = _mix(*common, router=(rw["rw"][j], rw["rb"][j]))
            experts = (rw["moe_g"], rw["moe_u"], rw["moe_d"], j, FF_DIM // 2)
            if prompt and grp.tm == MOE_TM:
                x = _moe_routed(bz, length, h2, x, mod, route, wts, cb, cnt, *experts)
            else:
                x = _moe(grp_f, h2, x, mod, gate, *experts)
        lat_l.append(lat.reshape(bz, length, MLA_KV_LORA))
        kr_l.append(kr.reshape(bz, length, MLA_ROPE))
        sre_l.append(htr.reshape(bz, SSM_GROUPS, SSM_STATE))
        sim_l.append(hti.reshape(bz, SSM_GROUPS, SSM_STATE))
        hg_l.append(_blocks_to_state(st_t))
    y = _final_norm(grp, x, row(w["final_norm_g"])).reshape(bz, length, D_MODEL)
    return y, jnp.stack(lat_l), jnp.stack(kr_l), jnp.stack(sre_l), jnp.stack(sim_l), jnp.stack(hg_l)


def kernel(x_prompt, x_sample, c_prompt, c_sample, cache_kv_latent, cache_k_rope, state_ssm_re, state_ssm_im,
           state_hgrn, page_table, ada_w, ada_b, norm1_g, norm2_g, w_in, ssm_a_re, ssm_a_im, ssm_log_dt,
           ssm_b_re, ssm_b_im, ssm_c_re, ssm_c_im, ssm_d, ssm_glu_w, ssm_glu_b, mla_q_norm_g, mla_w_uq,
           mla_kv_norm_g, mla_w_uk, mla_w_uv, hgrn_lb_logits, hgrn_norm_g, proj_a, proj_b, proj_c, w_out,
           ffn_w_gate, ffn_w_up, ffn_w_down, moe_router_w, moe_router_b, moe_w_gate, moe_w_up, moe_w_down,
           final_norm_g):
    w = dict(norm1_g=norm1_g, norm2_g=norm2_g, w_in=w_in, ssm_b_re=ssm_b_re, ssm_b_im=ssm_b_im,
             ssm_c_re=ssm_c_re, ssm_c_im=ssm_c_im, ssm_d=ssm_d, ssm_glu_w=ssm_glu_w, ssm_glu_b=ssm_glu_b,
             mla_q_norm_g=mla_q_norm_g, mla_w_uq=mla_w_uq, mla_kv_norm_g=mla_kv_norm_g, mla_w_uk=mla_w_uk,
             mla_w_uv=mla_w_uv, hgrn_norm_g=hgrn_norm_g, proj_a=proj_a, proj_b=proj_b, proj_c=proj_c,
             w_out=w_out, ffn_w_gate=ffn_w_gate, ffn_w_up=ffn_w_up, ffn_w_down=ffn_w_down,
             moe_router_w=moe_router_w, moe_router_b=moe_router_b, moe_w_gate=moe_w_gate,
             moe_w_up=moe_w_up, moe_w_down=moe_w_down, final_norm_g=final_norm_g)
    rw = _relayout_weights(w)
    prm = _prep_params(ssm_a_re, ssm_a_im, ssm_log_dt, hgrn_lb_logits)
    n_p = c_prompt.shape[0]
    mod_all = _ada_mod(jnp.concatenate([c_prompt, c_sample], axis=0), ada_w, ada_b)
    y_p, lat_p, kr_p, sre_p, sim_p, hg_p = _trunk(
        x_prompt, mod_all[:, :n_p], w, rw, prm, 0, None, None, None, None, None, None)
    past_len = page_table.shape[1] * cache_kv_latent.shape[2]
    y_s, lat_s, kr_s, sre_s, sim_s, hg_s = _trunk(
        x_sample, mod_all[:, n_p:], w, rw, prm, past_len, state_ssm_re, state_ssm_im, state_hgrn,
        cache_kv_latent, jnp.swapaxes(cache_k_rope, 2, 3), page_table)
    return (y_p, y_s, lat_p, kr_p, sre_p, sim_p, hg_p, lat_s, kr_s, sre_s, sim_s, hg_s)
```
